```python
import math
import jax, jax.numpy as jnp
from jax import lax
import numpy as np

D_MODEL = 2048
BATCH = 4
SEQ = 4096
DEPTH = 4

HEAD_DIM = 128
N_BRANCHES = 4
BRANCH_WIDTH = D_MODEL // N_BRANCHES
CONV_CH = BRANCH_WIDTH
CONV_WIDTH = 31
SGU_CH = BRANCH_WIDTH
SGU_CHUNK = 128
SGU_GROUPS = SGU_CH // HEAD_DIM
DIL_HEADS = BRANCH_WIDTH // HEAD_DIM
DIL_PATTERNS = ((128, 1), (512, 4), (2048, 16))
DIL_BLOCK = 128
DIFF_HEADS = BRANCH_WIDTH // HEAD_DIM
DIFF_QK_DIM = HEAD_DIM // 2
DIFF_BLOCK = 128
ROPE_THETA = 500000.0
ROPE_FRACTION = 4
D_FF = 5632
EPS = 1e-6
NEG_INF = -1e30

A_COLS = 2 * CONV_CH
B_COLS = 2 * SGU_CH
C_COLS = 3 * DIL_HEADS * HEAD_DIM
DQK_COLS = DIFF_HEADS * 2 * DIFF_QK_DIM
DV_COLS = DIFF_HEADS * HEAD_DIM
GATE_COLS = N_BRANCHES * D_MODEL
COL_WIDTHS = (A_COLS, B_COLS, C_COLS, DQK_COLS, DQK_COLS, DV_COLS, GATE_COLS)
IN_COLS = sum(COL_WIDTHS)
SPLIT_POINTS = tuple(sum(COL_WIDTHS[: i + 1]) for i in range(len(COL_WIDTHS) - 1))

kernel_name = "macaron_gated_hybrid_conv_sgu_dilated_diffattn"


def rmsnorm(x, g):
    xf = x.astype(jnp.float32)
    y = xf * lax.rsqrt(jnp.mean(xf * xf, axis=-1, keepdims=True) + EPS)
    return (y * g.astype(jnp.float32)).astype(x.dtype)


def layernorm(x, g, b):
    xf = x.astype(jnp.float32)
    mu = jnp.mean(xf, axis=-1, keepdims=True)
    xc = xf - mu
    y = xc * lax.rsqrt(jnp.mean(xc * xc, axis=-1, keepdims=True) + EPS)
    return (y * g.astype(jnp.float32) + b.astype(jnp.float32)).astype(x.dtype)


def partial_rope(x, positions):
    dh = x.shape[-1]
    rot = dh // ROPE_FRACTION
    half = rot // 2
    inv_freq = 1.0 / (ROPE_THETA ** (jnp.arange(half, dtype=jnp.float32) * 2.0 / rot))
    ang = positions.astype(jnp.float32)[..., None] * inv_freq
    ang = ang.reshape(ang.shape[:2] + (1,) * (x.ndim - 3) + (half,))
    cos, sin = jnp.cos(ang), jnp.sin(ang)
    xf = x.astype(jnp.float32)
    x1, x2, xp = xf[..., :half], xf[..., half:rot], xf[..., rot:]
    out = jnp.concatenate([x1 * cos - x2 * sin, x2 * cos + x1 * sin, xp], axis=-1)
    return out.astype(x.dtype)


def swiglu_ffn(x, g, w13, w2):
    h = rmsnorm(x, g)
    gate, up = jnp.split(h @ w13, 2, axis=-1)
    return (jax.nn.silu(gate) * up) @ w2


def conformer_conv(a_in, conv_w, conv_b, ln_g, ln_b):
    a, gate = jnp.split(a_in, 2, axis=-1)
    z = a * jax.nn.sigmoid(gate)
    z = lax.conv_general_dilated(
        z, conv_w[:, None, :], window_strides=(1,),
        padding=((CONV_WIDTH - 1, 0),),
        dimension_numbers=("NWC", "WIO", "NWC"),
        feature_group_count=CONV_CH) + conv_b
    z = layernorm(z, ln_g, ln_b)
    return jax.nn.silu(z)


def spatial_gating(b_in, ln_g, ln_b, w_s, b_s):
    z = jax.nn.gelu(b_in, approximate=False)
    u, v = jnp.split(z, 2, axis=-1)
    v = layernorm(v, ln_g, ln_b)
    bn, s, _ = v.shape
    nc = s // SGU_CHUNK
    v = v.reshape(bn, nc, SGU_CHUNK, SGU_GROUPS, HEAD_DIM)
    causal = jnp.tril(jnp.ones((SGU_CHUNK, SGU_CHUNK), dtype=bool))
    w = jnp.where(causal[None], w_s, jnp.zeros((), w_s.dtype))
    mixed = jnp.einsum("gts,bcsgd->bctgd", w, v) + b_s.T[None, None, :, :, None]
    return u * mixed.reshape(bn, s, SGU_CH)


def _strided_window(q, k, v, span, dil):
    bn, s, h, dh = q.shape
    L = s // dil
    nb = -(-L // DIL_BLOCK)
    lp = nb * DIL_BLOCK

    def to_sub(t):
        return jnp.swapaxes(t.reshape(bn, L, dil, h, dh), 1, 2)

    qs, ks, vs = to_sub(q), to_sub(k), to_sub(v)
    qb = jnp.pad(qs, ((0, 0), (0, 0), (0, lp - L), (0, 0), (0, 0))).reshape(bn, dil, nb, DIL_BLOCK, h, dh)
    kp = jnp.pad(ks, ((0, 0), (0, 0), (DIL_BLOCK, lp - L), (0, 0), (0, 0))).reshape(bn, dil, nb + 1, DIL_BLOCK, h, dh)
    vp = jnp.pad(vs, ((0, 0), (0, 0), (DIL_BLOCK, lp - L), (0, 0), (0, 0))).reshape(bn, dil, nb + 1, DIL_BLOCK, h, dh)
    kband = jnp.concatenate([kp[:, :, :-1], kp[:, :, 1:]], axis=3)
    vband = jnp.concatenate([vp[:, :, :-1], vp[:, :, 1:]], axis=3)

    scores = jnp.einsum("brnqhd,brnkhd->brnqhk", qb, kband) * (dh ** -0.5)
    qi = jnp.arange(DIL_BLOCK)[:, None]
    kk = jnp.arange(2 * DIL_BLOCK)[None, :]
    dist = qi + DIL_BLOCK - kk
    in_band = (dist >= 0) & (dist <= span)
    key_sub = jnp.arange(nb)[:, None, None] * DIL_BLOCK - DIL_BLOCK + kk[None]
    mask = in_band[None] & (key_sub >= 0)
    scores = jnp.where(mask[None, None, :, :, None, :], scores, NEG_INF)
    m = jnp.max(scores, axis=-1)
    p = jnp.exp(scores - m[..., None])
    den = jnp.sum(p, axis=-1)
    o = jnp.einsum("brnqhk,brnkhd->brnqhd", p, vband) / den[..., None]

    def from_sub(t):
        t = t.reshape((bn, dil, lp) + t.shape[4:])[:, :, :L]
        t = jnp.swapaxes(t, 1, 2)
        return t.reshape((bn, s) + t.shape[3:])

    return from_sub(o), from_sub(m), from_sub(den)


def dilated_window_attention(q, k, v):
    dtype = v.dtype
    qf, kf, vf = q.astype(jnp.float32), k.astype(jnp.float32), v.astype(jnp.float32)
    res = [_strided_window(qf, kf, vf, w // d, d) for (w, d) in DIL_PATTERNS]
    m_all = jnp.max(jnp.stack([r[1] for r in res], axis=0), axis=0)
    wts = [r[2] * jnp.exp(r[1] - m_all) for r in res]
    num = sum(wg[..., None] * r[0] for wg, r in zip(wts, res))
    out = num / sum(wts)[..., None]
    bn, s, h, dh = q.shape
    return out.reshape(bn, s, h * dh).astype(dtype)


def diff_attention(q, k, v, lam, lam_init, subln_g):
    bn, s, h, _, dq = q.shape
    nb = s // DIFF_BLOCK
    qb = jnp.moveaxis(q.astype(jnp.float32).reshape(bn, nb, DIFF_BLOCK, h, 2, dq), 1, 0)
    kf = k.astype(jnp.float32)
    vf = v.astype(jnp.float32)
    kpos = jnp.arange(s)
    scale = dq ** -0.5

    def block(args):
        i, qi = args
        sc = jnp.einsum("bqhcd,bkhcd->bhcqk", qi, kf) * scale
        qpos = i * DIFF_BLOCK + jnp.arange(DIFF_BLOCK)
        causal = kpos[None, :] <= qpos[:, None]
        sc = jnp.where(causal[None, None, None], sc, NEG_INF)
        p = jax.nn.softmax(sc, axis=-1)
        a = p[:, :, 0] - lam * p[:, :, 1]
        return jnp.einsum("bhqk,bkhd->bqhd", a, vf)

    o = lax.map(block, (jnp.arange(nb), qb))
    o = jnp.moveaxis(o, 0, 1).reshape(bn, s, h, HEAD_DIM)
    o = rmsnorm(o, subln_g) * (1.0 - lam_init)
    return o.reshape(bn, s, h * HEAD_DIM).astype(v.dtype)


def setup_inputs(seed: int = 0) -> dict:
    key = jax.random.key(seed)
    ks = jax.random.split(key, 32)
    f32 = jnp.float32

    def nrm(k, shape, scale):
        return jax.random.normal(k, shape, f32) * scale

    def gain(k, shape):
        return 1.0 + 0.02 * jax.random.normal(k, shape, f32)

    x = jax.random.normal(ks[0], (BATCH, SEQ, D_MODEL), f32)
    offset = jax.random.randint(ks[1], (BATCH, 1), 0, 1024, dtype=jnp.int32)
    positions = offset + jnp.arange(SEQ, dtype=jnp.int32)[None, :]
    return {
        "x": x,
        "positions": positions,
        "ffn1_norm": gain(ks[2], (DEPTH, D_MODEL)),
        "ffn1_w13": nrm(ks[3], (DEPTH, D_MODEL, 2 * D_FF), D_MODEL ** -0.5),
        "ffn1_w2": nrm(ks[4], (DEPTH, D_FF, D_MODEL), D_FF ** -0.5),
        "mix_norm": gain(ks[5], (DEPTH, D_MODEL)),
        "w_in": nrm(ks[6], (DEPTH, D_MODEL, IN_COLS), D_MODEL ** -0.5),
        "conv_w": nrm(ks[7], (DEPTH, CONV_WIDTH, CONV_CH), CONV_WIDTH ** -0.5),
        "conv_b": nrm(ks[8], (DEPTH, CONV_CH), 0.02),
        "conv_ln_g": gain(ks[9], (DEPTH, CONV_CH)),
        "conv_ln_b": nrm(ks[10], (DEPTH, CONV_CH), 0.02),
        "sgu_ln_g": gain(ks[11], (DEPTH, SGU_CH)),
        "sgu_ln_b": nrm(ks[12], (DEPTH, SGU_CH), 0.02),
        "sgu_w": nrm(ks[13], (DEPTH, SGU_GROUPS, SGU_CHUNK, SGU_CHUNK), SGU_CHUNK ** -0.5),
        "sgu_b": gain(ks[14], (DEPTH, SGU_GROUPS, SGU_CHUNK)),
        "dil_q_norm": gain(ks[15], (DEPTH, HEAD_DIM)),
        "dil_k_norm": gain(ks[16], (DEPTH, HEAD_DIM)),
        "diff_q_norm": gain(ks[17], (DEPTH, DIFF_QK_DIM)),
        "diff_k_norm": gain(ks[18], (DEPTH, DIFF_QK_DIM)),
        "diff_lq1": nrm(ks[19], (DEPTH, DIFF_QK_DIM), 0.1),
        "diff_lk1": nrm(ks[20], (DEPTH, DIFF_QK_DIM), 0.1),
        "diff_lq2": nrm(ks[21], (DEPTH, DIFF_QK_DIM), 0.1),
        "diff_lk2": nrm(ks[22], (DEPTH, DIFF_QK_DIM), 0.1),
        "diff_subln": gain(ks[23], (DEPTH, HEAD_DIM)),
        "w_branch": nrm(ks[24], (DEPTH, N_BRANCHES, BRANCH_WIDTH, D_MODEL), BRANCH_WIDTH ** -0.5),
        "w_out": nrm(ks[25], (DEPTH, D_MODEL, D_MODEL), D_MODEL ** -0.5),
        "ffn2_norm": gain(ks[26], (DEPTH, D_MODEL)),
        "ffn2_w13": nrm(ks[27], (DEPTH, D_MODEL, 2 * D_FF), D_MODEL ** -0.5),
        "ffn2_w2": nrm(ks[28], (DEPTH, D_FF, D_MODEL), D_FF ** -0.5),
    }


def reference(x, positions, ffn1_norm, ffn1_w13, ffn1_w2, mix_norm, w_in, conv_w, conv_b,
              conv_ln_g, conv_ln_b, sgu_ln_g, sgu_ln_b, sgu_w, sgu_b, dil_q_norm, dil_k_norm,
              diff_q_norm, diff_k_norm, diff_lq1, diff_lk1, diff_lq2, diff_lk2, diff_subln,
              w_branch, w_out, ffn2_norm, ffn2_w13, ffn2_w2):
    bn, s, _ = x.shape
    for l in range(DEPTH):
        x = x + 0.5 * swiglu_ffn(x, ffn1_norm[l], ffn1_w13[l], ffn1_w2[l])

        h = rmsnorm(x, mix_norm[l])
        proj = h @ w_in[l]
        a_in, b_in, c_qkv, d_q, d_k, d_v, gates = jnp.split(proj, SPLIT_POINTS, axis=-1)

        ya = conformer_conv(a_in, conv_w[l], conv_b[l], conv_ln_g[l], conv_ln_b[l])

        yb = spatial_gating(b_in, sgu_ln_g[l], sgu_ln_b[l], sgu_w[l], sgu_b[l])

        cq, ck, cv = jnp.split(c_qkv.reshape(bn, s, 3, DIL_HEADS, HEAD_DIM), 3, axis=2)
        cq = partial_rope(rmsnorm(cq[:, :, 0], dil_q_norm[l]), positions)
        ck = partial_rope(rmsnorm(ck[:, :, 0], dil_k_norm[l]), positions)
        yc = dilated_window_attention(cq, ck, cv[:, :, 0])

        dq = partial_rope(rmsnorm(d_q.reshape(bn, s, DIFF_HEADS, 2, DIFF_QK_DIM), diff_q_norm[l]), positions)
        dk = partial_rope(rmsnorm(d_k.reshape(bn, s, DIFF_HEADS, 2, DIFF_QK_DIM), diff_k_norm[l]), positions)
        dv = d_v.reshape(bn, s, DIFF_HEADS, HEAD_DIM)
        lam_init = 0.8 - 0.6 * math.exp(-0.3 * l)
        lam = (jnp.exp(jnp.sum(diff_lq1[l].astype(jnp.float32) * diff_lk1[l].astype(jnp.float32)))
               - jnp.exp(jnp.sum(diff_lq2[l].astype(jnp.float32) * diff_lk2[l].astype(jnp.float32)))
               + lam_init)
        yd = diff_attention(dq, dk, dv, lam, lam_init, diff_subln[l])

        g = jax.nn.sigmoid(gates).reshape(bn, s, N_BRANCHES, D_MODEL)
        branches = (ya, yb, yc, yd)
        merged = g[:, :, 0] * (branches[0] @ w_branch[l, 0])
        for i in range(1, N_BRANCHES):
            merged = merged + g[:, :, i] * (branches[i] @ w_branch[l, i])
        x = x + merged @ w_out[l]

        x = x + 0.5 * swiglu_ffn(x, ffn2_norm[l], ffn2_w13[l], ffn2_w2[l])
    return x
```

```python
import functools
import math

import jax
import jax.numpy as jnp
import numpy as np
from jax import lax
from jax.experimental import pallas as pl
from jax.experimental.pallas import tpu as pltpu

HEAD_DIM = 128
DIFF_QK_DIM = HEAD_DIM // 2
SGU_CHUNK = 128
DIL_PATTERNS = ((128, 1), (512, 4), (2048, 16))
DIL_SPAN = 128
ROPE_THETA = 500000.0
ROPE_FRACTION = 4
EPS = 1e-6
NEG_INF = -1e30
CONV_HALO = 32

VMEM_LIMIT_BYTES = 56 * 1024 * 1024

BF16 = jnp.bfloat16
F32 = jnp.float32


def _params(*sem):
    return pltpu.CompilerParams(dimension_semantics=sem, vmem_limit_bytes=VMEM_LIMIT_BYTES)


def _dot(a, b):
    return jnp.dot(a, b, preferred_element_type=F32)


def _dot_nt(a, b):
    return lax.dot_general(a, b, (((1,), (1,)), ((), ())), preferred_element_type=F32)


def _rms(x, g):
    return x * lax.rsqrt(jnp.mean(x * x, axis=-1, keepdims=True) + EPS) * g


def _layernorm(x, g, b):
    mu = jnp.mean(x, axis=-1, keepdims=True)
    xc = x - mu
    return xc * lax.rsqrt(jnp.mean(xc * xc, axis=-1, keepdims=True) + EPS) * g + b


def _ffn_kernel(x_ref, g_ref, wg_ref, wu_ref, w2_ref, o_ref, h_ref):
    @pl.when(pl.program_id(1) == 0)
    def _():
        x = x_ref[...]
        h_ref[...] = _rms(x, g_ref[...]).astype(BF16)
        o_ref[...] = x

    h = h_ref[...]
    gate = _dot(h, wg_ref[...])
    up = _dot(h, wu_ref[...])
    act = (0.5 * gate * jax.nn.sigmoid(gate) * up).astype(BF16)
    o_ref[...] += _dot(act, w2_ref[...])


def _ffn(x2, g, w13, w2, *, tm, tf):
    m, d = x2.shape
    dff = w2.shape[0]
    nf = dff // tf
    return pl.pallas_call(
        _ffn_kernel,
        grid=(m // tm, nf),
        in_specs=[
            pl.BlockSpec((tm, d), lambda i, j: (i, 0)),
            pl.BlockSpec((1, d), lambda i, j: (0, 0)),
            pl.BlockSpec((d, tf), lambda i, j: (0, j)),
            pl.BlockSpec((d, tf), lambda i, j: (0, j + nf)),
            pl.BlockSpec((tf, d), lambda i, j: (j, 0)),
        ],
        out_specs=pl.BlockSpec((tm, d), lambda i, j: (i, 0)),
        out_shape=jax.ShapeDtypeStruct((m, d), F32),
        scratch_shapes=[pltpu.VMEM((tm, d), BF16)],
        compiler_params=_params("parallel", "arbitrary"),
        name="swiglu_ffn",
    )(x2, g, w13, w13, w2)


def _inproj_kernel(x_ref, g_ref, w_ref, o_ref, h_ref):
    @pl.when(pl.program_id(1) == 0)
    def _():
        h_ref[...] = _rms(x_ref[...], g_ref[...]).astype(BF16)

    o_ref[...] = _dot(h_ref[...], w_ref[...])


def _inproj(x2, g, w_in, *, tm, tn):
    m, d = x2.shape
    n = w_in.shape[1]
    return pl.pallas_call(
        _inproj_kernel,
        grid=(m // tm, n // tn),
        in_specs=[
            pl.BlockSpec((tm, d), lambda i, j: (i, 0)),
            pl.BlockSpec((1, d), lambda i, j: (0, 0)),
            pl.BlockSpec((d, tn), lambda i, j: (0, j)),
        ],
        out_specs=pl.BlockSpec((tm, tn), lambda i, j: (i, j)),
        out_shape=jax.ShapeDtypeStruct((m, n), F32),
        scratch_shapes=[pltpu.VMEM((tm, d), BF16)],
        compiler_params=_params("parallel", "arbitrary"),
        name="in_proj",
    )(x2, g, w_in)


def _rope_table_kernel(pos_ref, inv_c_ref, sgn_c_ref, inv_d_ref, sgn_d_ref,
                       cos_c_ref, sin_c_ref, cos_d_ref, sin_d_ref):
    p = pos_ref[...].astype(F32)
    ang_c = p * inv_c_ref[...]
    cos_c_ref[...] = jnp.cos(ang_c)
    sin_c_ref[...] = jnp.sin(ang_c) * sgn_c_ref[...]
    ang_d = p * inv_d_ref[...]
    cos_d_ref[...] = jnp.cos(ang_d)
    sin_d_ref[...] = jnp.sin(ang_d) * sgn_d_ref[...]


def _rope_lane_consts(width):
    rot = width // ROPE_FRACTION
    half = rot // 2
    inv_freq = 1.0 / (ROPE_THETA ** (jnp.arange(half, dtype=F32) * 2.0 / rot))
    lane = np.arange(HEAD_DIM) % width
    in_rot = lane < rot
    inv = jnp.where(in_rot, inv_freq[lane % half], 0.0).astype(F32)
    sgn = np.where(lane < half, -1.0, np.where(in_rot, 1.0, 0.0)).astype(np.float32)
    return inv.reshape(1, HEAD_DIM), jnp.asarray(sgn).reshape(1, HEAD_DIM)


def _rope_tables(positions, *, ts):
    m = positions.size
    inv_c, sgn_c = _rope_lane_consts(HEAD_DIM)
    inv_d, sgn_d = _rope_lane_consts(DIFF_QK_DIM)
    lane_spec = pl.BlockSpec((1, HEAD_DIM), lambda i: (0, 0))
    tab_spec = pl.BlockSpec((ts, HEAD_DIM), lambda i: (i, 0))
    tab = jax.ShapeDtypeStruct((m, HEAD_DIM), F32)
    return pl.pallas_call(
        _rope_table_kernel,
        grid=(m // ts,),
        in_specs=[pl.BlockSpec((ts, 1), lambda i: (i, 0)), lane_spec, lane_spec, lane_spec, lane_spec],
        out_specs=[tab_spec] * 4,
        out_shape=[tab] * 4,
        compiler_params=_params("parallel"),
        name="rope_tables",
    )(positions.reshape(m, 1), inv_c, sgn_c, inv_d, sgn_d)


def _rope(x, cos, sin, half):
    lane = lax.broadcasted_iota(jnp.int32, x.shape, 1)
    lower = (lane % (2 * half)) < half
    partner = jnp.where(lower, pltpu.roll(x, HEAD_DIM - half, 1), pltpu.roll(x, half, 1))
    return x * cos + partner * sin


def _conv_kernel(a_ref, gate_ref, w_ref, b_ref, lg_ref, lb_ref, o_ref, z_ref, *, ts, width, rows):
    @pl.when(pl.program_id(1) == 0)
    def _():
        z_ref[0:CONV_HALO, :] = jnp.zeros((CONV_HALO, z_ref.shape[1]), F32)

    z_ref[CONV_HALO:CONV_HALO + ts, :] = a_ref[...] * jax.nn.sigmoid(gate_ref[...])
    ch = z_ref.shape[1]
    for c in range(ts // rows):
        r0 = c * rows
        acc = jnp.broadcast_to(b_ref[...], (rows, ch))
        for k in range(width):
            off = CONV_HALO - (width - 1) + k + r0
            acc = acc + w_ref[k:k + 1, :] * z_ref[off:off + rows, :]
        y = _layernorm(acc, lg_ref[...], lb_ref[...])
        o_ref[r0:r0 + rows, :] = (y * jax.nn.sigmoid(y)).astype(o_ref.dtype)
    z_ref[0:CONV_HALO, :] = z_ref[ts:ts + CONV_HALO, :]


def _conv_mixer(proj, conv_w, conv_b, ln_g, ln_b, *, batch, seq, bw, ts):
    m = proj.shape[0]
    width = conv_w.shape[0]
    assert width - 1 <= CONV_HALO
    ns = seq // ts
    vec = pl.BlockSpec((1, bw), lambda b, s: (0, 0))
    return pl.pallas_call(
        functools.partial(_conv_kernel, ts=ts, width=width, rows=64),
        grid=(batch, ns),
        in_specs=[
            pl.BlockSpec((ts, bw), lambda b, s: (b * ns + s, 0)),
            pl.BlockSpec((ts, bw), lambda b, s: (b * ns + s, 1)),
            pl.BlockSpec((width, bw), lambda b, s: (0, 0)),
            vec, vec, vec,
        ],
        out_specs=pl.BlockSpec((ts, bw), lambda b, s: (b * ns + s, 0)),
        out_shape=jax.ShapeDtypeStruct((m, bw), BF16),
        scratch_shapes=[pltpu.VMEM((ts + CONV_HALO, bw), F32)],
        compiler_params=_params("arbitrary", "arbitrary"),
        name="conv_mixer",
    )(proj, proj, conv_w, conv_b, ln_g, ln_b)


def _gelu(x):
    return 0.5 * x * (1.0 + lax.erf(x * np.float32(math.sqrt(0.5))))


def _sgu_kernel(u_ref, v_ref, lg_ref, lb_ref, ws_ref, bias_ref, o_ref, *, ts, groups):
    u = _gelu(u_ref[...])
    v = _layernorm(_gelu(v_ref[...]), lg_ref[...], lb_ref[...]).astype(BF16)
    row = lax.broadcasted_iota(jnp.int32, (SGU_CHUNK, SGU_CHUNK), 0)
    col = lax.broadcasted_iota(jnp.int32, (SGU_CHUNK, SGU_CHUNK), 1)
    for g in range(groups):
        lanes = slice(g * HEAD_DIM, (g + 1) * HEAD_DIM)
        w = jnp.where(row >= col, ws_ref[g], 0.0).astype(BF16)
        for c in range(ts // SGU_CHUNK):
            rws = slice(c * SGU_CHUNK, (c + 1) * SGU_CHUNK)
            mixed = _dot(w, v[rws, lanes]) + bias_ref[:, lanes]
            o_ref[rws, lanes] = (u[rws, lanes] * mixed).astype(o_ref.dtype)


def _sgu_mixer(proj, ln_g, ln_b, w_s, b_s, *, bw, ts):
    m = proj.shape[0]
    groups = w_s.shape[0]
    bias = jnp.repeat(b_s.T, HEAD_DIM, axis=1)
    vec = pl.BlockSpec((1, bw), lambda i: (0, 0))
    return pl.pallas_call(
        functools.partial(_sgu_kernel, ts=ts, groups=groups),
        grid=(m // ts,),
        in_specs=[
            pl.BlockSpec((ts, bw), lambda i: (i, 2)),
            pl.BlockSpec((ts, bw), lambda i: (i, 3)),
            vec, vec,
            pl.BlockSpec((groups, SGU_CHUNK, SGU_CHUNK), lambda i: (0, 0, 0)),
            pl.BlockSpec((SGU_CHUNK, bw), lambda i: (0, 0)),
        ],
        out_specs=pl.BlockSpec((ts, bw), lambda i: (i, 0)),
        out_shape=jax.ShapeDtypeStruct((m, bw), BF16),
        compiler_params=_params("parallel"),
        name="sgu_mixer",
    )(proj, proj, ln_g, ln_b, w_s, bias)


def _dil_prep_kernel(q_ref, k_ref, v_ref, qg_ref, kg_ref, cos_ref, sin_ref, qo_ref, ko_ref, vo_ref, *, heads):
    cos = cos_ref[...]
    sin = sin_ref[...]
    half = HEAD_DIM // ROPE_FRACTION // 2
    for h in range(heads):
        lanes = slice(h * HEAD_DIM, (h + 1) * HEAD_DIM)
        qo_ref[:, lanes] = _rope(_rms(q_ref[:, lanes], qg_ref[...]), cos, sin, half).astype(BF16)
        ko_ref[:, lanes] = _rope(_rms(k_ref[:, lanes], kg_ref[...]), cos, sin, half).astype(BF16)
    vo_ref[...] = v_ref[...].astype(BF16)


def _dil_prep(proj, q_norm, k_norm, cos, sin, *, bw, ts):
    m = proj.shape[0]
    heads = bw // HEAD_DIM
    lane_spec = pl.BlockSpec((1, HEAD_DIM), lambda i: (0, 0))
    tab_spec = pl.BlockSpec((ts, HEAD_DIM), lambda i: (i, 0))
    out_spec = pl.BlockSpec((ts, bw), lambda i: (i, 0))
    out = jax.ShapeDtypeStruct((m, bw), BF16)
    return pl.pallas_call(
        functools.partial(_dil_prep_kernel, heads=heads),
        grid=(m // ts,),
        in_specs=[
            pl.BlockSpec((ts, bw), lambda i: (i, 4)),
            pl.BlockSpec((ts, bw), lambda i: (i, 5)),
            pl.BlockSpec((ts, bw), lambda i: (i, 6)),
            lane_spec, lane_spec, tab_spec, tab_spec,
        ],
        out_specs=[out_spec] * 3,
        out_shape=[out] * 3,
        compiler_params=_params("parallel"),
        name="dil_prep",
    )(proj, proj, proj, q_norm, k_norm, cos, sin)


def _dil_attn_kernel(q_ref, k_ref, v_ref, o_ref, *, tq, win):
    i = pl.program_id(2)
    start = pl.multiple_of(jnp.maximum(i * tq + tq - win, 0), HEAD_DIM)
    kw = k_ref[pl.ds(start, win), :]
    vw = v_ref[pl.ds(start, win), :]
    s = _dot_nt(q_ref[...], kw) * np.float32(HEAD_DIM ** -0.5)
    row = lax.broadcasted_iota(jnp.int32, (tq, win), 0)
    col = lax.broadcasted_iota(jnp.int32, (tq, win), 1)
    dist = (i * tq - start) + row - col
    cnt = jnp.zeros((tq, win), F32)
    for window, dil in DIL_PATTERNS:
        hit = (dist >= 0) & (dist <= window) & ((dist & (dil - 1)) == 0)
        cnt = cnt + jnp.where(hit, 1.0, 0.0)
    s = jnp.where(cnt > 0.0, s, NEG_INF)
    mx = jnp.max(s, axis=-1, keepdims=True)
    p = jnp.exp(s - mx) * cnt
    den = jnp.sum(p, axis=-1, keepdims=True)
    o_ref[...] = (_dot(p.astype(BF16), vw) / den).astype(o_ref.dtype)


def _dil_attn(q, k, v, *, batch, seq, tq):
    m, bw = q.shape
    heads = bw // HEAD_DIM
    nq = seq // tq
    max_window = max(w for w, _ in DIL_PATTERNS)
    win = -(-(max_window + tq) // HEAD_DIM) * HEAD_DIM
    assert win <= seq and all(d & (d - 1) == 0 for _, d in DIL_PATTERNS)
    q_spec = pl.BlockSpec((tq, HEAD_DIM), lambda b, h, i: (b * nq + i, h))
    kv_spec = pl.BlockSpec((seq, HEAD_DIM), lambda b, h, i: (b, h))
    return pl.pallas_call(
        functools.partial(_dil_attn_kernel, tq=tq, win=win),
        grid=(batch, heads, nq),
        in_specs=[q_spec, kv_spec, kv_spec],
        out_specs=q_spec,
        out_shape=jax.ShapeDtypeStruct((m, bw), BF16),
        compiler_params=_params("parallel", "parallel", "arbitrary"),
        name="dil_attn",
    )(q, k, v)


def _segment_rms(x, g):
    lane = lax.broadcasted_iota(jnp.int32, x.shape, 1)
    lower = lane < DIFF_QK_DIM
    sq = x * x
    lo = jnp.sum(jnp.where(lower, sq, 0.0), axis=-1, keepdims=True)
    hi = jnp.sum(jnp.where(lower, 0.0, sq), axis=-1, keepdims=True)
    ms = jnp.where(lower, lo, hi) * np.float32(1.0 / DIFF_QK_DIM)
    return x * lax.rsqrt(ms + EPS) * g


def _diff_prep_kernel(q_ref, k_ref, v_ref, qg_ref, kg_ref, cos_ref, sin_ref, qo_ref, ko_ref, vo_ref, *, heads):
    cos = cos_ref[...]
    sin = sin_ref[...]
    half = DIFF_QK_DIM // ROPE_FRACTION // 2
    scale = np.float32(DIFF_QK_DIM ** -0.5)
    for h in range(heads):
        lanes = slice(h * HEAD_DIM, (h + 1) * HEAD_DIM)
        q = _rope(_segment_rms(q_ref[:, lanes], qg_ref[...]), cos, sin, half)
        qo_ref[:, lanes] = (q * scale).astype(BF16)
        ko_ref[:, lanes] = _rope(_segment_rms(k_ref[:, lanes], kg_ref[...]), cos, sin, half).astype(BF16)
    vo_ref[...] = v_ref[...].astype(BF16)


def _diff_prep(proj, q_norm, k_norm, cos, sin, *, bw, ts):
    m = proj.shape[0]
    heads = bw // HEAD_DIM
    lane_spec = pl.BlockSpec((1, HEAD_DIM), lambda i: (0, 0))
    tab_spec = pl.BlockSpec((ts, HEAD_DIM), lambda i: (i, 0))
    out_spec = pl.BlockSpec((ts, bw), lambda i: (i, 0))
    out = jax.ShapeDtypeStruct((m, bw), BF16)
    return pl.pallas_call(
        functools.partial(_diff_prep_kernel, heads=heads),
        grid=(m // ts,),
        in_specs=[
            pl.BlockSpec((ts, bw), lambda i: (i, 7)),
            pl.BlockSpec((ts, bw), lambda i: (i, 8)),
            pl.BlockSpec((ts, bw), lambda i: (i, 9)),
            lane_spec, lane_spec, tab_spec, tab_spec,
        ],
        out_specs=[out_spec] * 3,
        out_shape=[out] * 3,
        compiler_params=_params("parallel"),
        name="diff_prep",
    )(proj, proj, proj, jnp.tile(q_norm, (1, 2)), jnp.tile(k_norm, (1, 2)), cos, sin)


def _diff_attn_kernel(q_ref, k_ref, v_ref, lq1_ref, lk1_ref, lq2_ref, lk2_ref, sub_ref, o_ref,
                      m_ref, l_ref, acc_ref, *, tq, lam_init):
    i = pl.program_id(2)
    q = q_ref[...]
    lane = lax.broadcasted_iota(jnp.int32, q.shape, 1)
    zero = jnp.zeros_like(q)
    qq = jnp.concatenate([jnp.where(lane < DIFF_QK_DIM, q, zero), jnp.where(lane < DIFF_QK_DIM, zero, q)], axis=0)

    m_ref[...] = jnp.full(m_ref.shape, NEG_INF, F32)
    l_ref[...] = jnp.zeros(l_ref.shape, F32)
    acc_ref[...] = jnp.zeros(acc_ref.shape, F32)

    def step(j, masked):
        k0 = pl.multiple_of(j * tq, tq)
        s = _dot_nt(qq, k_ref[pl.ds(k0, tq), :])
        if masked:
            row = lax.broadcasted_iota(jnp.int32, s.shape, 0)
            col = lax.broadcasted_iota(jnp.int32, s.shape, 1)
            row = jnp.where(row >= tq, row - tq, row)
            s = jnp.where(col <= row, s, NEG_INF)
        m_old = m_ref[...]
        m_new = jnp.maximum(m_old, jnp.max(s, axis=-1, keepdims=True))
        alpha = jnp.exp(m_old - m_new)
        p = jnp.exp(s - m_new)
        l_ref[...] = alpha * l_ref[...] + jnp.sum(p, axis=-1, keepdims=True)
        acc_ref[...] = alpha * acc_ref[...] + _dot(p.astype(BF16), v_ref[pl.ds(k0, tq), :])
        m_ref[...] = m_new

    def body(j, carry):
        step(j, False)
        return carry

    lax.fori_loop(0, i, body, 0)
    step(i, True)

    o = acc_ref[...] / l_ref[...]
    lam = (jnp.exp(jnp.sum(lq1_ref[...] * lk1_ref[...], axis=-1, keepdims=True))
           - jnp.exp(jnp.sum(lq2_ref[...] * lk2_ref[...], axis=-1, keepdims=True)) + np.float32(lam_init))
    od = o[0:tq, :] - lam * o[tq:2 * tq, :]
    o_ref[...] = (_rms(od, sub_ref[...]) * np.float32(1.0 - lam_init)).astype(o_ref.dtype)


def _diff_attn(q, k, v, lq1, lk1, lq2, lk2, subln, *, batch, seq, tq, lam_init):
    m, bw = q.shape
    heads = bw // HEAD_DIM
    nq = seq // tq
    q_spec = pl.BlockSpec((tq, HEAD_DIM), lambda b, h, i: (b * nq + i, h))
    kv_spec = pl.BlockSpec((seq, HEAD_DIM), lambda b, h, i: (b, h))
    lam_spec = pl.BlockSpec((1, DIFF_QK_DIM), lambda b, h, i: (0, 0))
    return pl.pallas_call(
        functools.partial(_diff_attn_kernel, tq=tq, lam_init=lam_init),
        grid=(batch, heads, nq),
        in_specs=[q_spec, kv_spec, kv_spec, lam_spec, lam_spec, lam_spec, lam_spec,
                  pl.BlockSpec((1, HEAD_DIM), lambda b, h, i: (0, 0))],
        out_specs=q_spec,
        out_shape=jax.ShapeDtypeStruct((m, bw), BF16),
        scratch_shapes=[pltpu.VMEM((2 * tq, 1), F32), pltpu.VMEM((2 * tq, 1), F32),
                        pltpu.VMEM((2 * tq, HEAD_DIM), F32)],
        compiler_params=_params("parallel", "parallel", "arbitrary"),
        name="diff_attn",
    )(q, k, v, lq1, lk1, lq2, lk2, subln)


def _merge_kernel(y_ref, gate_ref, wb_ref, wo_ref, x_ref, o_ref, acc_ref):
    i = pl.program_id(1)

    @pl.when(i == 0)
    def _():
        acc_ref[...] = jnp.zeros(acc_ref.shape, F32)

    acc_ref[...] += jax.nn.sigmoid(gate_ref[...]) * _dot(y_ref[...], wb_ref[0])

    @pl.when(i == pl.num_programs(1) - 1)
    def _():
        o_ref[...] = x_ref[...] + _dot(acc_ref[...].astype(BF16), wo_ref[...])


def _merge(ycat, gates, w_branch, w_out, x2, *, tm, bw):
    m, d = x2.shape
    nb = w_branch.shape[0]
    return pl.pallas_call(
        _merge_kernel,
        grid=(m // tm, nb),
        in_specs=[
            pl.BlockSpec((tm, bw), lambda r, i: (r, i)),
            pl.BlockSpec((tm, d), lambda r, i: (r, i)),
            pl.BlockSpec((1, bw, d), lambda r, i: (i, 0, 0)),
            pl.BlockSpec((d, d), lambda r, i: (0, 0)),
            pl.BlockSpec((tm, d), lambda r, i: (r, 0)),
        ],
        out_specs=pl.BlockSpec((tm, d), lambda r, i: (r, 0)),
        out_shape=jax.ShapeDtypeStruct((m, d), F32),
        scratch_shapes=[pltpu.VMEM((tm, d), F32)],
        compiler_params=_params("parallel", "arbitrary"),
        name="merge_out",
    )(ycat, gates, w_branch, w_out, x2)


def kernel(x, positions, ffn1_norm, ffn1_w13, ffn1_w2, mix_norm, w_in, conv_w, conv_b, conv_ln_g, conv_ln_b, sgu_ln_g, sgu_ln_b, sgu_w, sgu_b, dil_q_norm, dil_k_norm, diff_q_norm, diff_k_norm, diff_lq1, diff_lk1, diff_lq2, diff_lk2, diff_subln, w_branch, w_out, ffn2_norm, ffn2_w13, ffn2_w2):
    batch, seq, d = x.shape
    depth = w_in.shape[0]
    m = batch * seq
    bw = d // 4
    assert bw % HEAD_DIM == 0 and w_in.shape[2] == 10 * bw + 4 * d

    tm = min(512, m)
    dff = ffn1_w2.shape[1]
    tf = 512 if dff % 512 == 0 else 256
    ts = min(256, seq)

    def row(v, l):
        return v[l].reshape(1, -1)

    x2 = x.reshape(m, d)
    cos_c, sin_c, cos_d, sin_d = _rope_tables(positions, ts=min(512, m))

    for l in range(depth):
        lam_init = 0.8 - 0.6 * math.exp(-0.3 * l)
        x2 = _ffn(x2, row(ffn1_norm, l), ffn1_w13[l].astype(BF16), ffn1_w2[l].astype(BF16), tm=tm, tf=tf)

        n_mix = w_in.shape[2] - 4 * d
        proj = _inproj(x2, row(mix_norm, l), w_in[l, :, :n_mix].astype(BF16), tm=tm, tn=bw)
        gates = _inproj(x2, row(mix_norm, l), w_in[l, :, n_mix:].astype(BF16), tm=tm, tn=bw)

        ya = _conv_mixer(proj, conv_w[l], row(conv_b, l), row(conv_ln_g, l), row(conv_ln_b, l),
                         batch=batch, seq=seq, bw=bw, ts=ts)
        yb = _sgu_mixer(proj, row(sgu_ln_g, l), row(sgu_ln_b, l), sgu_w[l], sgu_b[l], bw=bw, ts=ts)
        cq, ck, cv = _dil_prep(proj, row(dil_q_norm, l), row(dil_k_norm, l), cos_c, sin_c, bw=bw, ts=ts)
        yc = _dil_attn(cq, ck, cv, batch=batch, seq=seq, tq=128)
        dq, dk, dv = _diff_prep(proj, row(diff_q_norm, l), row(diff_k_norm, l), cos_d, sin_d, bw=bw, ts=ts)
        yd = _diff_attn(dq, dk, dv, row(diff_lq1, l), row(diff_lk1, l), row(diff_lq2, l), row(diff_lk2, l),
                        row(diff_subln, l), batch=batch, seq=seq, tq=256, lam_init=lam_init)

        ycat = jnp.concatenate([ya, yb, yc, yd], axis=1)
        x2 = _merge(ycat, gates, w_branch[l].astype(BF16), w_out[l].astype(BF16), x2, tm=min(256, m), bw=bw)

        x2 = _ffn(x2, row(ffn2_norm, l), ffn2_w13[l].astype(BF16), ffn2_w2[l].astype(BF16), tm=tm, tf=tf)
    return x2.reshape(batch, seq, d)
```

```python
import functools
import math

import jax
import jax.numpy as jnp
import numpy as np
from jax import lax
from jax.experimental import pallas as pl
from jax.experimental.pallas import tpu as pltpu

HEAD_DIM = 128
LANES = 128
DIFF_QK_DIM = HEAD_DIM // 2
SGU_CHUNK = 128
DIL_PATTERNS = ((128, 1), (512, 4), (2048, 16))
DIL_SPAN = 128
DIL_BLOCK = 128
ROPE_THETA = 500000.0
ROPE_FRACTION = 4
EPS = 1e-6
NEG_INF = -1e30
CONV_HALO = 32
SUBLANES = 8

VMEM_LIMIT_BYTES = 56 * 1024 * 1024

BF16 = jnp.bfloat16
F32 = jnp.float32


def _params(*sem):
    return pltpu.CompilerParams(dimension_semantics=sem, vmem_limit_bytes=VMEM_LIMIT_BYTES)


def _resident(shape, index_map):
    return pl.BlockSpec(shape, index_map, pipeline_mode=pl.Buffered(1))


def _dot(a, b):
    return jnp.dot(a, b, preferred_element_type=F32)


def _dot_nt(a, b):
    return lax.dot_general(a, b, (((1,), (1,)), ((), ())), preferred_element_type=F32)


def _rms(x, g):
    return x * lax.rsqrt(jnp.mean(x * x, axis=-1, keepdims=True) + EPS) * g


def _layernorm(x, g, b):
    mu = jnp.mean(x, axis=-1, keepdims=True)
    xc = x - mu
    return xc * lax.rsqrt(jnp.mean(xc * xc, axis=-1, keepdims=True) + EPS) * g + b


def _gelu(x):
    return 0.5 * x * (1.0 + lax.erf(x * np.float32(math.sqrt(0.5))))


def _rmsnorm_kernel(x_ref, g_ref, o_ref):
    o_ref[...] = _rms(x_ref[...], g_ref[...]).astype(BF16)


def _rmsnorm(x2, g, *, tm):
    m, d = x2.shape
    return pl.pallas_call(
        _rmsnorm_kernel,
        grid=(m // tm,),
        in_specs=[pl.BlockSpec((tm, d), lambda i: (i, 0)), pl.BlockSpec((1, d), lambda i: (0, 0))],
        out_specs=pl.BlockSpec((tm, d), lambda i: (i, 0)),
        out_shape=jax.ShapeDtypeStruct((m, d), BF16),
        compiler_params=_params("parallel"),
        name="rmsnorm",
    )(x2, g)


def _ffn_kernel(x_ref, h_ref, gn_ref, wg_ref, wu_ref, w2_ref, o_ref, hn_ref):
    j = pl.program_id(1)

    @pl.when(j == 0)
    def _():
        o_ref[...] = x_ref[...]

    h = h_ref[...]
    gate = _dot(h, wg_ref[...])
    up = _dot(h, wu_ref[...])
    act = (0.5 * gate * jax.nn.sigmoid(gate) * up).astype(BF16)
    o_ref[...] += _dot(act, w2_ref[...])

    @pl.when(j == pl.num_programs(1) - 1)
    def _():
        hn_ref[...] = _rms(o_ref[...], gn_ref[...]).astype(BF16)


def _ffn(x2, h, g_next, w13, w2, *, tm, tf):
    m, d = x2.shape
    dff = w2.shape[0]
    nf = dff // tf
    row = pl.BlockSpec((tm, d), lambda i, j: (i, 0))
    return pl.pallas_call(
        _ffn_kernel,
        grid=(m // tm, nf),
        in_specs=[
            row, row,
            pl.BlockSpec((1, d), lambda i, j: (0, 0)),
            pl.BlockSpec((d, tf), lambda i, j: (0, j)),
            pl.BlockSpec((d, tf), lambda i, j: (0, j + nf)),
            pl.BlockSpec((tf, d), lambda i, j: (j, 0)),
        ],
        out_specs=[row, row],
        out_shape=[jax.ShapeDtypeStruct((m, d), F32), jax.ShapeDtypeStruct((m, d), BF16)],
        compiler_params=_params("parallel", "arbitrary"),
        name="swiglu_ffn",
    )(x2, h, g_next, w13, w13, w2)


def _rope_table_kernel(pos_ref, inv_c_ref, sgn_c_ref, inv_d_ref, sgn_d_ref,
                       cos_c_ref, sin_c_ref, cos_d_ref, sin_d_ref):
    p = pos_ref[...].astype(F32)
    ang_c = p * inv_c_ref[...]
    cos_c_ref[...] = jnp.cos(ang_c)
    sin_c_ref[...] = jnp.sin(ang_c) * sgn_c_ref[...]
    ang_d = p * inv_d_ref[...]
    cos_d_ref[...] = jnp.cos(ang_d)
    sin_d_ref[...] = jnp.sin(ang_d) * sgn_d_ref[...]


def _rope_lane_consts(width):
    rot = width // ROPE_FRACTION
    half = rot // 2
    inv_freq = 1.0 / (ROPE_THETA ** (jnp.arange(half, dtype=F32) * 2.0 / rot))
    lane = np.arange(LANES) % width
    in_rot = lane < rot
    inv = jnp.where(in_rot, inv_freq[lane % half], 0.0).astype(F32)
    sgn = np.where(lane < half, -1.0, np.where(in_rot, 1.0, 0.0)).astype(np.float32)
    return inv.reshape(1, LANES), jnp.asarray(sgn).reshape(1, LANES)


def _rope_tables(positions, *, ts):
    m = positions.size
    inv_c, sgn_c = _rope_lane_consts(HEAD_DIM)
    inv_d, sgn_d = _rope_lane_consts(DIFF_QK_DIM)
    lane_spec = pl.BlockSpec((1, LANES), lambda i: (0, 0))
    tab_spec = pl.BlockSpec((ts, LANES), lambda i: (i, 0))
    tab = jax.ShapeDtypeStruct((m, LANES), F32)
    return pl.pallas_call(
        _rope_table_kernel,
        grid=(m // ts,),
        in_specs=[pl.BlockSpec((ts, 1), lambda i: (i, 0)), lane_spec, lane_spec, lane_spec, lane_spec],
        out_specs=[tab_spec] * 4,
        out_shape=[tab] * 4,
        compiler_params=_params("parallel"),
        name="rope_tables",
    )(positions.reshape(m, 1), inv_c, sgn_c, inv_d, sgn_d)


def _rope(x, cos, sin, half):
    lane = lax.broadcasted_iota(jnp.int32, x.shape, 1)
    lower = (lane % (2 * half)) < half
    partner = jnp.where(lower, pltpu.roll(x, LANES - half, 1), pltpu.roll(x, half, 1))
    return x * cos + partner * sin


def _segment_rms(x, g):
    lane = lax.broadcasted_iota(jnp.int32, x.shape, 1)
    lower = lane < DIFF_QK_DIM
    sq = x * x
    lo = jnp.sum(jnp.where(lower, sq, 0.0), axis=-1, keepdims=True)
    hi = jnp.sum(jnp.where(lower, 0.0, sq), axis=-1, keepdims=True)
    ms = jnp.where(lower, lo, hi) * np.float32(1.0 / DIFF_QK_DIM)
    return x * lax.rsqrt(ms + EPS) * g


def _mix_proj_kernel(h_ref, w_ref, sg_ref, sb_ref, ws_ref, bias_ref, cqg_ref, ckg_ref, cos_c_ref, sin_c_ref,
                     dqg_ref, dkg_ref, cos_d_ref, sin_d_ref,
                     z_ref, yb_ref, cq_ref, ck_ref, cv_ref, dq_ref, dk_ref, dv_ref, *, bw, tm):
    h = h_ref[...]
    heads = bw // HEAD_DIM

    def seg(c0, n):
        return _dot(h, w_ref[:, c0 * bw:(c0 + n) * bw])

    r = seg(0, 2)
    z_ref[...] = r[:, :bw] * jax.nn.sigmoid(r[:, bw:])

    r = seg(2, 2)
    u = _gelu(r[:, :bw])
    v = _layernorm(_gelu(r[:, bw:]), sg_ref[...], sb_ref[...]).astype(BF16)
    row = lax.broadcasted_iota(jnp.int32, (SGU_CHUNK, SGU_CHUNK), 0)
    col = lax.broadcasted_iota(jnp.int32, (SGU_CHUNK, SGU_CHUNK), 1)
    for g in range(heads):
        lanes = slice(g * HEAD_DIM, (g + 1) * HEAD_DIM)
        w = jnp.where(row >= col, ws_ref[g], 0.0).astype(BF16)
        for c in range(tm // SGU_CHUNK):
            rws = slice(c * SGU_CHUNK, (c + 1) * SGU_CHUNK)
            mixed = _dot(w, v[rws, lanes]) + bias_ref[:, lanes]
            yb_ref[rws, lanes] = (u[rws, lanes] * mixed).astype(BF16)

    r = seg(4, 3)
    cos, sin = cos_c_ref[...], sin_c_ref[...]
    half = HEAD_DIM // ROPE_FRACTION // 2
    for hd in range(heads):
        lanes = slice(hd * HEAD_DIM, (hd + 1) * HEAD_DIM)
        klanes = slice(bw + hd * HEAD_DIM, bw + (hd + 1) * HEAD_DIM)
        cq_ref[:, lanes] = _rope(_rms(r[:, lanes], cqg_ref[...]), cos, sin, half).astype(BF16)
        ck_ref[:, lanes] = _rope(_rms(r[:, klanes], ckg_ref[...]), cos, sin, half).astype(BF16)
    cv_ref[...] = r[:, 2 * bw:].astype(BF16)

    r = seg(7, 3)
    cos, sin = cos_d_ref[...], sin_d_ref[...]
    half = DIFF_QK_DIM // ROPE_FRACTION // 2
    scale = np.float32(DIFF_QK_DIM ** -0.5)
    for hd in range(heads):
        lanes = slice(hd * HEAD_DIM, (hd + 1) * HEAD_DIM)
        klanes = slice(bw + hd * HEAD_DIM, bw + (hd + 1) * HEAD_DIM)
        q = _rope(_segment_rms(r[:, lanes], dqg_ref[...]), cos, sin, half)
        dq_ref[:, lanes] = (q * scale).astype(BF16)
        dk_ref[:, lanes] = _rope(_segment_rms(r[:, klanes], dkg_ref[...]), cos, sin, half).astype(BF16)
    dv_ref[...] = r[:, 2 * bw:].astype(BF16)


def _mix_proj(h, w_mix, sgu_g, sgu_b, w_s, b_s, cq_g, ck_g, cos_c, sin_c, dq_g, dk_g, cos_d, sin_d, *, bw, tm):
    m, d = h.shape
    groups = w_s.shape[0]
    assert groups * HEAD_DIM == bw and w_mix.shape[1] == 10 * bw
    bias = jnp.repeat(b_s.T, HEAD_DIM, axis=1)
    vec = pl.BlockSpec((1, bw), lambda i: (0, 0))
    lane_spec = pl.BlockSpec((1, LANES), lambda i: (0, 0))
    tab_spec = pl.BlockSpec((tm, LANES), lambda i: (i, 0))
    out_spec = pl.BlockSpec((tm, bw), lambda i: (i, 0))
    out_bf = jax.ShapeDtypeStruct((m, bw), BF16)
    return pl.pallas_call(
        functools.partial(_mix_proj_kernel, bw=bw, tm=tm),
        grid=(m // tm,),
        in_specs=[
            pl.BlockSpec((tm, d), lambda i: (i, 0)),
            _resident((d, 10 * bw), lambda i: (0, 0)),
            vec, vec,
            pl.BlockSpec((groups, SGU_CHUNK, SGU_CHUNK), lambda i: (0, 0, 0)),
            pl.BlockSpec((SGU_CHUNK, bw), lambda i: (0, 0)),
            lane_spec, lane_spec, tab_spec, tab_spec,
            lane_spec, lane_spec, tab_spec, tab_spec,
        ],
        out_specs=[out_spec] * 8,
        out_shape=[jax.ShapeDtypeStruct((m, bw), F32)] + [out_bf] * 7,
        compiler_params=_params("parallel"),
        name="mix_proj",
    )(h, w_mix, sgu_g, sgu_b, w_s, bias, cq_g, ck_g, cos_c, sin_c,
      jnp.tile(dq_g, (1, 2)), jnp.tile(dk_g, (1, 2)), cos_d, sin_d)


def _gate_proj_kernel(h_ref, w_ref, o_ref):
    o_ref[...] = jax.nn.sigmoid(_dot(h_ref[...], w_ref[...])).astype(BF16)


def _gate_proj(h, w_gate, *, tm, tn):
    m, d = h.shape
    n = w_gate.shape[1]
    return pl.pallas_call(
        _gate_proj_kernel,
        grid=(m // tm, n // tn),
        in_specs=[pl.BlockSpec((tm, d), lambda i, j: (i, 0)), pl.BlockSpec((d, tn), lambda i, j: (0, j))],
        out_specs=pl.BlockSpec((tm, tn), lambda i, j: (i, j)),
        out_shape=jax.ShapeDtypeStruct((m, n), BF16),
        compiler_params=_params("parallel", "arbitrary"),
        name="gate_proj",
    )(h, w_gate)


def _conv_kernel(zin_ref, w_ref, b_ref, lg_ref, lb_ref, o_ref, z_ref, *, ts, width, rows):
    ext = ts + CONV_HALO

    @pl.when(pl.program_id(1) == 0)
    def _():
        z_ref[0, 0:CONV_HALO, :] = jnp.zeros((CONV_HALO, z_ref.shape[2]), F32)

    z_ref[0, CONV_HALO:ext, :] = zin_ref[...]
    for s in range(1, SUBLANES):
        z_ref[s, 0:ext - SUBLANES, :] = z_ref[0, s:s + ext - SUBLANES, :]
    ch = z_ref.shape[2]
    for c in range(ts // rows):
        r0 = c * rows
        acc = jnp.broadcast_to(b_ref[...], (rows, ch))
        for k in range(width):
            off = CONV_HALO - (width - 1) + k
            base = r0 + off // SUBLANES * SUBLANES
            acc = acc + w_ref[k:k + 1, :] * z_ref[off % SUBLANES, base:base + rows, :]
        y = _layernorm(acc, lg_ref[...], lb_ref[...])
        o_ref[r0:r0 + rows, :] = (y * jax.nn.sigmoid(y)).astype(o_ref.dtype)
    z_ref[0, 0:CONV_HALO, :] = z_ref[0, ts:ext, :]


def _conv_mixer(z, conv_w, conv_b, ln_g, ln_b, *, batch, seq, ts):
    m, bw = z.shape
    width = conv_w.shape[0]
    assert width - 1 <= CONV_HALO
    ns = seq // ts
    vec = pl.BlockSpec((1, bw), lambda b, s: (0, 0))
    tile = pl.BlockSpec((ts, bw), lambda b, s: (b * ns + s, 0))
    return pl.pallas_call(
        functools.partial(_conv_kernel, ts=ts, width=width, rows=64),
        grid=(batch, ns),
        in_specs=[tile, pl.BlockSpec((width, bw), lambda b, s: (0, 0)), vec, vec, vec],
        out_specs=tile,
        out_shape=jax.ShapeDtypeStruct((m, bw), BF16),
        scratch_shapes=[pltpu.VMEM((SUBLANES, ts + CONV_HALO, bw), F32)],
        compiler_params=_params("arbitrary", "arbitrary"),
        name="conv_mixer",
    )(z, conv_w, conv_b, ln_g, ln_b)


def _dil_band_kernel(*refs, lt, heads, combine):
    if combine:
        q_ref, k_ref, v_ref, o_b_ref, lse_b_ref, o_c_ref, lse_c_ref, y_ref = refs
    else:
        q_ref, k_ref, v_ref, o_ref, lse_ref = refs
    i = pl.program_id(2)
    blk = DIL_BLOCK
    scale = np.float32(HEAD_DIM ** -0.5)
    ones = jnp.ones((2 * blk, HEAD_DIM), BF16)
    row = lax.broadcasted_iota(jnp.int32, (blk, 2 * blk), 0)
    col = lax.broadcasted_iota(jnp.int32, (blk, 2 * blk), 1)
    for jb in range(lt // blk):
        rws = slice(jb * blk, (jb + 1) * blk)
        l0 = i * lt + jb * blk
        ks = pl.multiple_of(jnp.maximum(l0 - blk, 0), blk)
        dist = (l0 - ks) + row - col
        valid = (dist >= 0) & (dist <= DIL_SPAN)
        for hd in range(heads):
            lanes = slice(hd * HEAD_DIM, (hd + 1) * HEAD_DIM)
            s = _dot_nt(q_ref[rws, lanes], k_ref[pl.ds(ks, 2 * blk), lanes]) * scale
            s = jnp.where(valid, s, NEG_INF)
            mx = jnp.max(s, axis=-1, keepdims=True)
            p = jnp.exp(s - mx).astype(BF16)
            r = _dot(p, jnp.concatenate([v_ref[pl.ds(ks, 2 * blk), lanes], ones], axis=1))
            den = r[:, HEAD_DIM:]
            o = r[:, :HEAD_DIM] / den
            lse = mx + jnp.log(den)
            if combine:
                o_b = o_b_ref[rws, lanes].astype(F32)
                o_c = o_c_ref[rws, lanes].astype(F32)
                lse_b = lse_b_ref[rws, lanes]
                lse_c = lse_c_ref[rws, lanes]
                top = jnp.maximum(jnp.maximum(lse, lse_b), lse_c)
                w_a, w_b, w_c = jnp.exp(lse - top), jnp.exp(lse_b - top), jnp.exp(lse_c - top)
                y_ref[rws, lanes] = ((w_a * o + w_b * o_b + w_c * o_c) / (w_a + w_b + w_c)).astype(BF16)
            else:
                o_ref[rws, lanes] = o.astype(BF16)
                lse_ref[rws, lanes] = lse


def _dil_band(q, k, v, others, *, batch, seq, dil, lt):
    m, bw = q.shape
    heads = bw // HEAD_DIM
    ln = seq // dil
    lt = min(lt, ln)
    nl = ln // lt
    assert ln % lt == 0 and lt % DIL_BLOCK == 0 and ln >= 2 * DIL_BLOCK
    view = (m // dil, dil * bw)
    tile = pl.BlockSpec((lt, bw), lambda b, r, i: (b * nl + i, r))
    whole = pl.BlockSpec((ln, bw), lambda b, r, i: (b, r))
    combine = bool(others)
    if combine:
        assert dil == 1
        extra = [a for pair in others for a in pair]
        out_specs, out_shape = tile, jax.ShapeDtypeStruct(view, BF16)
    else:
        extra = []
        out_specs = [tile, tile]
        out_shape = [jax.ShapeDtypeStruct(view, BF16), jax.ShapeDtypeStruct(view, F32)]
    out = pl.pallas_call(
        functools.partial(_dil_band_kernel, lt=lt, heads=heads, combine=combine),
        grid=(batch, dil, nl),
        in_specs=[tile, whole, whole] + [tile] * len(extra),
        out_specs=out_specs,
        out_shape=out_shape,
        compiler_params=_params("parallel", "parallel", "arbitrary"),
        name=f"dil_band_{dil}",
    )(q.reshape(view), k.reshape(view), v.reshape(view), *extra)
    if combine:
        return out
    return out[0].reshape(m, bw), out[1].reshape(m, bw)


def _dil_attn(q, k, v, *, batch, seq):
    assert all(w // d == DIL_SPAN for w, d in DIL_PATTERNS) and DIL_PATTERNS[0][1] == 1
    others = [_dil_band(q, k, v, (), batch=batch, seq=seq, dil=d, lt=512) for _, d in DIL_PATTERNS[1:]]
    return _dil_band(q, k, v, others, batch=batch, seq=seq, dil=1, lt=512)


def _diff_attn_kernel(q_ref, k_ref, v_ref, lq1_ref, lk1_ref, lq2_ref, lk2_ref, sub_ref, o_ref,
                      s_ref, m_ref, l_ref, acc_ref, *, tq, lam_init):
    i = pl.program_id(2)
    q = q_ref[...]
    lane = lax.broadcasted_iota(jnp.int32, q.shape, 1)
    zero = jnp.zeros_like(q)
    qq = jnp.concatenate([jnp.where(lane < DIFF_QK_DIM, q, zero), jnp.where(lane < DIFF_QK_DIM, zero, q)], axis=0)
    tiles = tq // LANES

    def lane_fold(x, op):
        r = x[:, 0:LANES]
        for t in range(1, tiles):
            r = op(r, x[:, t * LANES:(t + 1) * LANES])
        return r

    m_ref[...] = jnp.full(m_ref.shape, NEG_INF, F32)

    def scores(j, masked):
        k0 = pl.multiple_of(j * tq, tq)
        s = _dot_nt(qq, k_ref[pl.ds(k0, tq), :])
        if masked:
            row = lax.broadcasted_iota(jnp.int32, s.shape, 0)
            col = lax.broadcasted_iota(jnp.int32, s.shape, 1)
            row = jnp.where(row >= tq, row - tq, row)
            s = jnp.where(col <= row, s, NEG_INF)
        s_ref[j] = s
        m_ref[...] = jnp.maximum(m_ref[...], lane_fold(s, jnp.maximum))

    def scores_body(j, carry):
        scores(j, False)
        return carry

    lax.fori_loop(0, i, scores_body, 0)
    scores(i, True)
    m_ref[...] = jnp.broadcast_to(jnp.max(m_ref[...], axis=-1, keepdims=True), m_ref.shape)

    l_ref[...] = jnp.zeros(l_ref.shape, F32)
    acc_ref[...] = jnp.zeros(acc_ref.shape, F32)

    def accumulate(j, carry):
        k0 = pl.multiple_of(j * tq, tq)
        s = s_ref[j]
        mx = m_ref[...]
        ps = [jnp.exp(s[:, t * LANES:(t + 1) * LANES] - mx) for t in range(tiles)]
        part = ps[0]
        for t in range(1, tiles):
            part = part + ps[t]
        l_ref[...] += part
        p = jnp.concatenate(ps, axis=1).astype(BF16)
        acc_ref[...] += _dot(p, v_ref[pl.ds(k0, tq), :])
        return carry

    lax.fori_loop(0, i + 1, accumulate, 0)

    o = acc_ref[...] / jnp.sum(l_ref[...], axis=-1, keepdims=True)
    lam = (jnp.exp(jnp.sum(lq1_ref[...] * lk1_ref[...], axis=-1, keepdims=True))
           - jnp.exp(jnp.sum(lq2_ref[...] * lk2_ref[...], axis=-1, keepdims=True)) + np.float32(lam_init))
    od = o[0:tq, :] - lam * o[tq:2 * tq, :]
    o_ref[...] = (_rms(od, sub_ref[...]) * np.float32(1.0 - lam_init)).astype(o_ref.dtype)


def _diff_attn(q, k, v, lq1, lk1, lq2, lk2, subln, *, batch, seq, tq, lam_init):
    m, bw = q.shape
    heads = bw // HEAD_DIM
    nq = seq // tq
    q_spec = pl.BlockSpec((tq, HEAD_DIM), lambda b, h, i: (b * nq + i, h))
    kv_spec = pl.BlockSpec((seq, HEAD_DIM), lambda b, h, i: (b, h))
    lam_spec = pl.BlockSpec((1, DIFF_QK_DIM), lambda b, h, i: (0, 0))
    stat = pltpu.VMEM((2 * tq, LANES), F32)
    return pl.pallas_call(
        functools.partial(_diff_attn_kernel, tq=tq, lam_init=lam_init),
        grid=(batch, heads, nq),
        in_specs=[q_spec, kv_spec, kv_spec, lam_spec, lam_spec, lam_spec, lam_spec,
                  pl.BlockSpec((1, HEAD_DIM), lambda b, h, i: (0, 0))],
        out_specs=q_spec,
        out_shape=jax.ShapeDtypeStruct((m, bw), BF16),
        scratch_shapes=[pltpu.VMEM((nq, 2 * tq, tq), F32), stat, stat, stat],
        compiler_params=_params("parallel", "parallel", "arbitrary"),
        name="diff_attn",
    )(q, k, v, lq1, lk1, lq2, lk2, subln)


def _merge_kernel(ya_ref, yb_ref, yc_ref, yd_ref, gate_ref, wb_ref, wo_ref, x_ref, gn_ref, o_ref, hn_ref, acc_ref):
    i = pl.program_id(1)

    @pl.when(i == 0)
    def _():
        acc_ref[...] = jnp.zeros(acc_ref.shape, F32)

    for b, y_ref in enumerate((ya_ref, yb_ref, yc_ref, yd_ref)):
        @pl.when(i == b)
        def _():
            acc_ref[...] += gate_ref[...].astype(F32) * _dot(y_ref[...], wb_ref[0])

    @pl.when(i == pl.num_programs(1) - 1)
    def _():
        out = x_ref[...] + _dot(acc_ref[...].astype(BF16), wo_ref[...])
        o_ref[...] = out
        hn_ref[...] = _rms(out, gn_ref[...]).astype(BF16)


def _merge(ys, gates, w_branch, w_out, x2, g_next, *, tm):
    m, d = x2.shape
    nb, bw, _ = w_branch.shape
    assert nb == len(ys) == 4
    y_spec = pl.BlockSpec((tm, bw), lambda r, i: (r, 0))
    row = pl.BlockSpec((tm, d), lambda r, i: (r, 0))
    return pl.pallas_call(
        _merge_kernel,
        grid=(m // tm, nb),
        in_specs=[
            y_spec, y_spec, y_spec, y_spec,
            pl.BlockSpec((tm, d), lambda r, i: (r, i)),
            pl.BlockSpec((1, bw, d), lambda r, i: (i, 0, 0)),
            _resident((d, d), lambda r, i: (0, 0)),
            row,
            pl.BlockSpec((1, d), lambda r, i: (0, 0)),
        ],
        out_specs=[row, row],
        out_shape=[jax.ShapeDtypeStruct((m, d), F32), jax.ShapeDtypeStruct((m, d), BF16)],
        scratch_shapes=[pltpu.VMEM((tm, d), F32)],
        compiler_params=_params("parallel", "arbitrary"),
        name="merge_out",
    )(*ys, gates, w_branch, w_out, x2, g_next)


def kernel(x, positions, ffn1_norm, ffn1_w13, ffn1_w2, mix_norm, w_in, conv_w, conv_b, conv_ln_g, conv_ln_b, sgu_ln_g, sgu_ln_b, sgu_w, sgu_b, dil_q_norm, dil_k_norm, diff_q_norm, diff_k_norm, diff_lq1, diff_lk1, diff_lq2, diff_lk2, diff_subln, w_branch, w_out, ffn2_norm, ffn2_w13, ffn2_w2):
    batch, seq, d = x.shape
    depth = w_in.shape[0]
    m = batch * seq
    bw = d // 4
    n_mix = 10 * bw
    assert bw % HEAD_DIM == 0 and w_in.shape[2] == n_mix + 4 * d

    tm = min(512, m)
    dff = ffn1_w2.shape[1]
    tf = 512 if dff % 512 == 0 else 256

    def row(v, l):
        return v[l].reshape(1, -1)

    x2 = x.reshape(m, d)
    cos_c, sin_c, cos_d, sin_d = _rope_tables(positions, ts=min(512, m))
    h = _rmsnorm(x2, row(ffn1_norm, 0), tm=tm)

    for l in range(depth):
        lam_init = 0.8 - 0.6 * math.exp(-0.3 * l)
        x2, h = _ffn(x2, h, row(mix_norm, l), ffn1_w13[l].astype(BF16), ffn1_w2[l].astype(BF16), tm=tm, tf=tf)

        z, yb, cq, ck, cv, dq, dk, dv = _mix_proj(
            h, w_in[l, :, :n_mix].astype(BF16), row(sgu_ln_g, l), row(sgu_ln_b, l), sgu_w[l], sgu_b[l],
            row(dil_q_norm, l), row(dil_k_norm, l), cos_c, sin_c,
            row(diff_q_norm, l), row(diff_k_norm, l), cos_d, sin_d, bw=bw, tm=tm)
        gates = _gate_proj(h, w_in[l, :, n_mix:].astype(BF16), tm=min(1024, m), tn=min(1024, 4 * d))

        ya = _conv_mixer(z, conv_w[l], row(conv_b, l), row(conv_ln_g, l), row(conv_ln_b, l),
                         batch=batch, seq=seq, ts=min(256, seq))
        yc = _dil_attn(cq, ck, cv, batch=batch, seq=seq)
        yd = _diff_attn(dq, dk, dv, row(diff_lq1, l), row(diff_lk1, l), row(diff_lq2, l), row(diff_lk2, l),
                        row(diff_subln, l), batch=batch, seq=seq, tq=min(512, seq), lam_init=lam_init)

        x2, h = _merge((ya, yb, yc, yd), gates, w_branch[l].astype(BF16), w_out[l].astype(BF16), x2,
                       row(ffn2_norm, l), tm=tm)

        g_next = row(ffn1_norm, l + 1) if l + 1 < depth else row(ffn2_norm, l)
        x2, h = _ffn(x2, h, g_next, ffn2_w13[l].astype(BF16), ffn2_w2[l].astype(BF16), tm=tm, tf=tf)
    return x2.reshape(batch, seq, d)
```

```python
import functools
import math

import jax
import jax.numpy as jnp
import numpy as np
from jax import lax
from jax.experimental import pallas as pl
from jax.experimental.pallas import tpu as pltpu

HEAD_DIM = 128
LANES = 128
SUBLANES = 8
DIFF_QK_DIM = HEAD_DIM // 2
SGU_CHUNK = 128
DIL_PATTERNS = ((128, 1), (512, 4), (2048, 16))
DIL_SPAN = 128
DIL_BLOCK = 128
ROPE_THETA = 500000.0
ROPE_FRACTION = 4
EPS = 1e-6
NEG_INF = -1e30
CONV_HALO = 32

VMEM_LIMIT_BYTES = 56 * 1024 * 1024

BF16 = jnp.bfloat16
F32 = jnp.float32


def _params(*sem):
    return pltpu.CompilerParams(dimension_semantics=sem, vmem_limit_bytes=VMEM_LIMIT_BYTES)


def _resident(shape, index_map):
    return pl.BlockSpec(shape, index_map, pipeline_mode=pl.Buffered(1))


def _dot(a, b):
    return jnp.dot(a, b, preferred_element_type=F32)


def _dot_nt(a, b):
    return lax.dot_general(a, b, (((1,), (1,)), ((), ())), preferred_element_type=F32)


def _rms(x, g):
    return x * lax.rsqrt(jnp.mean(x * x, axis=-1, keepdims=True) + EPS) * g


def _layernorm(x, g, b):
    mu = jnp.mean(x, axis=-1, keepdims=True)
    xc = x - mu
    return xc * lax.rsqrt(jnp.mean(xc * xc, axis=-1, keepdims=True) + EPS) * g + b


def _gelu(x):
    return 0.5 * x * (1.0 + lax.erf(x * np.float32(math.sqrt(0.5))))


def _ffn_up_kernel(x_ref, g_ref, wg_ref, wu_ref, a_ref, h_ref):
    @pl.when(pl.program_id(1) == 0)
    def _():
        h_ref[...] = _rms(x_ref[...], g_ref[...]).astype(BF16)

    h = h_ref[...]
    gate = _dot(h, wg_ref[...])
    up = _dot(h, wu_ref[...])
    a_ref[...] = (0.5 * gate * jax.nn.sigmoid(gate) * up).astype(BF16)


def _ffn_up(x2, g, w13, *, tm, tf):
    m, d = x2.shape
    dff = w13.shape[1] // 2
    nf = dff // tf
    return pl.pallas_call(
        _ffn_up_kernel,
        grid=(m // tm, nf),
        in_specs=[
            pl.BlockSpec((tm, d), lambda i, j: (i, 0)),
            pl.BlockSpec((1, d), lambda i, j: (0, 0)),
            pl.BlockSpec((d, tf), lambda i, j: (0, j)),
            pl.BlockSpec((d, tf), lambda i, j: (0, j + nf)),
        ],
        out_specs=pl.BlockSpec((tm, tf), lambda i, j: (i, j)),
        out_shape=jax.ShapeDtypeStruct((m, dff), BF16),
        scratch_shapes=[pltpu.VMEM((tm, d), BF16)],
        compiler_params=_params("parallel", "arbitrary"),
        name="ffn_up",
    )(x2, g, w13, w13)


def _residual_matmul_kernel(a_ref, w_ref, x_ref, o_ref):
    o_ref[...] = x_ref[...] + _dot(a_ref[...], w_ref[...])


def _residual_matmul(a, w, x2, *, tm, tn, name):
    m, kdim = a.shape
    d = w.shape[1]
    tile = pl.BlockSpec((tm, tn), lambda i, j: (i, j))
    return pl.pallas_call(
        _residual_matmul_kernel,
        grid=(m // tm, d // tn),
        in_specs=[pl.BlockSpec((tm, kdim), lambda i, j: (i, 0)), pl.BlockSpec((kdim, tn), lambda i, j: (0, j)), tile],
        out_specs=tile,
        out_shape=jax.ShapeDtypeStruct((m, d), F32),
        compiler_params=_params("parallel", "arbitrary"),
        name=name,
    )(a, w, x2)


def _rope_table_kernel(pos_ref, inv_c_ref, sgn_c_ref, inv_d_ref, sgn_d_ref,
                       cos_c_ref, sin_c_ref, cos_d_ref, sin_d_ref):
    p = pos_ref[...].astype(F32)
    ang_c = p * inv_c_ref[...]
    cos_c_ref[...] = jnp.cos(ang_c)
    sin_c_ref[...] = jnp.sin(ang_c) * sgn_c_ref[...]
    ang_d = p * inv_d_ref[...]
    cos_d_ref[...] = jnp.cos(ang_d)
    sin_d_ref[...] = jnp.sin(ang_d) * sgn_d_ref[...]


def _rope_lane_consts(width):
    rot = width // ROPE_FRACTION
    half = rot // 2
    inv_freq = 1.0 / (ROPE_THETA ** (jnp.arange(half, dtype=F32) * 2.0 / rot))
    lane = np.arange(LANES) % width
    in_rot = lane < rot
    inv = jnp.where(in_rot, inv_freq[lane % half], 0.0).astype(F32)
    sgn = np.where(lane < half, -1.0, np.where(in_rot, 1.0, 0.0)).astype(np.float32)
    return inv.reshape(1, LANES), jnp.asarray(sgn).reshape(1, LANES)


def _rope_tables(positions, *, ts):
    m = positions.size
    inv_c, sgn_c = _rope_lane_consts(HEAD_DIM)
    inv_d, sgn_d = _rope_lane_consts(DIFF_QK_DIM)
    lane_spec = pl.BlockSpec((1, LANES), lambda i: (0, 0))
    tab_spec = pl.BlockSpec((ts, LANES), lambda i: (i, 0))
    tab = jax.ShapeDtypeStruct((m, LANES), F32)
    return pl.pallas_call(
        _rope_table_kernel,
        grid=(m // ts,),
        in_specs=[pl.BlockSpec((ts, 1), lambda i: (i, 0)), lane_spec, lane_spec, lane_spec, lane_spec],
        out_specs=[tab_spec] * 4,
        out_shape=[tab] * 4,
        compiler_params=_params("parallel"),
        name="rope_tables",
    )(positions.reshape(m, 1), inv_c, sgn_c, inv_d, sgn_d)


def _rope(x, cos, sin, half):
    lane = lax.broadcasted_iota(jnp.int32, x.shape, 1)
    lower = (lane % (2 * half)) < half
    partner = jnp.where(lower, pltpu.roll(x, LANES - half, 1), pltpu.roll(x, half, 1))
    return x * cos + partner * sin


def _segment_rms(x, g):
    lane = lax.broadcasted_iota(jnp.int32, x.shape, 1)
    lower = lane < DIFF_QK_DIM
    sq = x * x
    lo = jnp.sum(jnp.where(lower, sq, 0.0), axis=-1, keepdims=True)
    hi = jnp.sum(jnp.where(lower, 0.0, sq), axis=-1, keepdims=True)
    ms = jnp.where(lower, lo, hi) * np.float32(1.0 / DIFF_QK_DIM)
    return x * lax.rsqrt(ms + EPS) * g


def _mix_proj_kernel(x_ref, g_ref, w_ref, sg_ref, sb_ref, ws_ref, bias_ref, cqg_ref, ckg_ref, cos_c_ref, sin_c_ref,
                     dqg_ref, dkg_ref, cos_d_ref, sin_d_ref,
                     z_ref, yb_ref, dq_ref, dk_ref, dv_ref, *c_refs_and_scratch, bw, tm, dils):
    c_refs, stage_ref = c_refs_and_scratch[:-1], c_refs_and_scratch[-1]
    h = _rms(x_ref[...], g_ref[...]).astype(BF16)
    heads = bw // HEAD_DIM

    def seg(c0, n):
        return _dot(h, w_ref[:, c0 * bw:(c0 + n) * bw])

    r = seg(0, 2)
    z_ref[...] = r[:, :bw] * jax.nn.sigmoid(r[:, bw:])

    r = seg(2, 2)
    u = _gelu(r[:, :bw])
    v = _layernorm(_gelu(r[:, bw:]), sg_ref[...], sb_ref[...]).astype(BF16)
    row = lax.broadcasted_iota(jnp.int32, (SGU_CHUNK, SGU_CHUNK), 0)
    col = lax.broadcasted_iota(jnp.int32, (SGU_CHUNK, SGU_CHUNK), 1)
    for g in range(heads):
        lanes = slice(g * HEAD_DIM, (g + 1) * HEAD_DIM)
        w = jnp.where(row >= col, ws_ref[g], 0.0).astype(BF16)
        for c in range(tm // SGU_CHUNK):
            rws = slice(c * SGU_CHUNK, (c + 1) * SGU_CHUNK)
            mixed = _dot(w, v[rws, lanes]) + bias_ref[:, lanes]
            yb_ref[rws, lanes] = (u[rws, lanes] * mixed).astype(BF16)

    r = seg(4, 3)
    cos, sin = cos_c_ref[...], sin_c_ref[...]
    half = HEAD_DIM // ROPE_FRACTION // 2
    for part in range(3):
        outs = c_refs[part * len(dils):(part + 1) * len(dils)]
        for hd in range(heads):
            lanes = slice(hd * HEAD_DIM, (hd + 1) * HEAD_DIM)
            t = r[:, part * bw + hd * HEAD_DIM:part * bw + (hd + 1) * HEAD_DIM]
            if part < 2:
                t = _rope(_rms(t, (cqg_ref, ckg_ref)[part][...]), cos, sin, half)
            stage_ref[hd] = t
            for dil, o_ref in zip(dils, outs):
                if dil == 1:
                    o_ref[:, lanes] = t.astype(BF16)
        for dil, o_ref in zip(dils, outs):
            if dil == 1:
                continue
            for res in range(dil):
                for hd in range(heads):
                    o_ref[:, res * bw + hd * HEAD_DIM:res * bw + (hd + 1) * HEAD_DIM] = (
                        stage_ref[hd, pl.ds(res, tm // dil, stride=dil), :].astype(BF16))

    r = seg(7, 3)
    cos, sin = cos_d_ref[...], sin_d_ref[...]
    half = DIFF_QK_DIM // ROPE_FRACTION // 2
    scale = np.float32(DIFF_QK_DIM ** -0.5)
    for hd in range(heads):
        lanes = slice(hd * HEAD_DIM, (hd + 1) * HEAD_DIM)
        klanes = slice(bw + hd * HEAD_DIM, bw + (hd + 1) * HEAD_DIM)
        q = _rope(_segment_rms(r[:, lanes], dqg_ref[...]), cos, sin, half)
        dq_ref[:, lanes] = (q * scale).astype(BF16)
        dk_ref[:, lanes] = _rope(_segment_rms(r[:, klanes], dkg_ref[...]), cos, sin, half).astype(BF16)
    dv_ref[...] = r[:, 2 * bw:].astype(BF16)


def _mix_proj(x2, g, w_mix, sgu_g, sgu_b, w_s, b_s, cq_g, ck_g, cos_c, sin_c, dq_g, dk_g, cos_d, sin_d, *, bw, tm):
    m, d = x2.shape
    groups = w_s.shape[0]
    heads = bw // HEAD_DIM
    dils = tuple(dil for _, dil in DIL_PATTERNS)
    assert groups == heads and w_mix.shape[1] == 10 * bw and all(tm % (dil * 2 * SUBLANES) == 0 for dil in dils)
    bias = jnp.repeat(b_s.T, HEAD_DIM, axis=1)
    vec = pl.BlockSpec((1, bw), lambda i: (0, 0))
    lane_spec = pl.BlockSpec((1, LANES), lambda i: (0, 0))
    tab_spec = pl.BlockSpec((tm, LANES), lambda i: (i, 0))
    out_spec = pl.BlockSpec((tm, bw), lambda i: (i, 0))
    out_bf = jax.ShapeDtypeStruct((m, bw), BF16)
    view_specs = [pl.BlockSpec((tm // dil, dil * bw), lambda i: (i, 0)) for dil in dils] * 3
    view_shapes = [jax.ShapeDtypeStruct((m // dil, dil * bw), BF16) for dil in dils] * 3
    outs = pl.pallas_call(
        functools.partial(_mix_proj_kernel, bw=bw, tm=tm, dils=dils),
        grid=(m // tm,),
        in_specs=[
            pl.BlockSpec((tm, d), lambda i: (i, 0)),
            pl.BlockSpec((1, d), lambda i: (0, 0)),
            _resident((d, 10 * bw), lambda i: (0, 0)),
            vec, vec,
            pl.BlockSpec((groups, SGU_CHUNK, SGU_CHUNK), lambda i: (0, 0, 0)),
            pl.BlockSpec((SGU_CHUNK, bw), lambda i: (0, 0)),
            lane_spec, lane_spec, tab_spec, tab_spec,
            lane_spec, lane_spec, tab_spec, tab_spec,
        ],
        out_specs=[out_spec] * 5 + view_specs,
        out_shape=[jax.ShapeDtypeStruct((m, bw), F32)] + [out_bf] * 4 + view_shapes,
        scratch_shapes=[pltpu.VMEM((heads, tm, HEAD_DIM), F32)],
        compiler_params=_params("parallel"),
        name="mix_proj",
    )(x2, g, w_mix, sgu_g, sgu_b, w_s, bias, cq_g, ck_g, cos_c, sin_c,
      jnp.tile(dq_g, (1, 2)), jnp.tile(dk_g, (1, 2)), cos_d, sin_d)
    n = len(dils)
    z, yb, dq, dk, dv = outs[:5]
    cq, ck, cv = outs[5:5 + n], outs[5 + n:5 + 2 * n], outs[5 + 2 * n:]
    return z, yb, (cq, ck, cv), (dq, dk, dv)


def _gate_proj_kernel(x_ref, g_ref, w_ref, o_ref, h_ref):
    @pl.when(pl.program_id(1) == 0)
    def _():
        h_ref[...] = _rms(x_ref[...], g_ref[...]).astype(BF16)

    o_ref[...] = jax.nn.sigmoid(_dot(h_ref[...], w_ref[...])).astype(BF16)


def _gate_proj(x2, g, w_gate, *, tm, tn):
    m, d = x2.shape
    n = w_gate.shape[1]
    return pl.pallas_call(
        _gate_proj_kernel,
        grid=(m // tm, n // tn),
        in_specs=[pl.BlockSpec((tm, d), lambda i, j: (i, 0)), pl.BlockSpec((1, d), lambda i, j: (0, 0)),
                  pl.BlockSpec((d, tn), lambda i, j: (0, j))],
        out_specs=pl.BlockSpec((tm, tn), lambda i, j: (i, j)),
        out_shape=jax.ShapeDtypeStruct((m, n), BF16),
        scratch_shapes=[pltpu.VMEM((tm, d), BF16)],
        compiler_params=_params("parallel", "arbitrary"),
        name="gate_proj",
    )(x2, g, w_gate)


def _conv_kernel(zin_ref, w_ref, b_ref, lg_ref, lb_ref, o_ref, z_ref, *, ts, width, rows):
    ext = ts + CONV_HALO

    @pl.when(pl.program_id(1) == 0)
    def _():
        z_ref[0, 0:CONV_HALO, :] = jnp.zeros((CONV_HALO, z_ref.shape[2]), F32)

    z_ref[0, CONV_HALO:ext, :] = zin_ref[...]
    for s in range(1, SUBLANES):
        z_ref[s, 0:ext - SUBLANES, :] = z_ref[0, s:s + ext - SUBLANES, :]
    ch = z_ref.shape[2]
    for c in range(ts // rows):
        r0 = c * rows
        acc = jnp.broadcast_to(b_ref[...], (rows, ch))
        for k in range(width):
            off = CONV_HALO - (width - 1) + k
            base = r0 + off // SUBLANES * SUBLANES
            acc = acc + w_ref[k:k + 1, :] * z_ref[off % SUBLANES, base:base + rows, :]
        y = _layernorm(acc, lg_ref[...], lb_ref[...])
        o_ref[r0:r0 + rows, :] = (y * jax.nn.sigmoid(y)).astype(o_ref.dtype)
    z_ref[0, 0:CONV_HALO, :] = z_ref[0, ts:ext, :]


def _conv_mixer(z, conv_w, conv_b, ln_g, ln_b, *, batch, seq, ts):
    m, bw = z.shape
    width = conv_w.shape[0]
    assert width - 1 <= CONV_HALO
    ns = seq // ts
    vec = pl.BlockSpec((1, bw), lambda b, s: (0, 0))
    tile = pl.BlockSpec((ts, bw), lambda b, s: (b * ns + s, 0))
    return pl.pallas_call(
        functools.partial(_conv_kernel, ts=ts, width=width, rows=64),
        grid=(batch, ns),
        in_specs=[tile, pl.BlockSpec((width, bw), lambda b, s: (0, 0)), vec, vec, vec],
        out_specs=tile,
        out_shape=jax.ShapeDtypeStruct((m, bw), BF16),
        scratch_shapes=[pltpu.VMEM((SUBLANES, ts + CONV_HALO, bw), F32)],
        compiler_params=_params("arbitrary", "arbitrary"),
        name="conv_mixer",
    )(z, conv_w, conv_b, ln_g, ln_b)


def _dil_band_kernel(*refs, lt, heads, others):
    bw = heads * HEAD_DIM
    if others:
        q_ref, k_ref, v_ref = refs[:3]
        other_refs = refs[3:3 + 2 * len(others)]
        y_ref = refs[3 + 2 * len(others)]
        stage_refs = refs[4 + 2 * len(others):]
        for dil, o_ref, lse_ref, o_st, lse_st in zip(others, other_refs[0::2], other_refs[1::2],
                                                     stage_refs[0::2], stage_refs[1::2]):
            for res in range(dil):
                for hd in range(heads):
                    cols = slice(res * bw + hd * HEAD_DIM, res * bw + (hd + 1) * HEAD_DIM)
                    o_st[hd, pl.ds(res, lt // dil, stride=dil), :] = o_ref[:, cols]
                    lse_st[hd, pl.ds(res, lt // dil, stride=dil), :] = lse_ref[:, cols]
    else:
        q_ref, k_ref, v_ref, o_ref, lse_ref = refs
    i = pl.program_id(2)
    blk = DIL_BLOCK
    scale = np.float32(HEAD_DIM ** -0.5)
    ones = jnp.ones((2 * blk, HEAD_DIM), BF16)
    row = lax.broadcasted_iota(jnp.int32, (blk, 2 * blk), 0)
    col = lax.broadcasted_iota(jnp.int32, (blk, 2 * blk), 1)
    for jb in range(lt // blk):
        rws = slice(jb * blk, (jb + 1) * blk)
        l0 = i * lt + jb * blk
        ks = pl.multiple_of(jnp.maximum(l0 - blk, 0), blk)
        dist = (l0 - ks) + row - col
        valid = (dist >= 0) & (dist <= DIL_SPAN)
        for hd in range(heads):
            lanes = slice(hd * HEAD_DIM, (hd + 1) * HEAD_DIM)
            s = _dot_nt(q_ref[rws, lanes], k_ref[pl.ds(ks, 2 * blk), lanes]) * scale
            s = jnp.where(valid, s, NEG_INF)
            mx = jnp.max(s, axis=-1, keepdims=True)
            p = jnp.exp(s - mx).astype(BF16)
            r = _dot(p, jnp.concatenate([v_ref[pl.ds(ks, 2 * blk), lanes], ones], axis=1))
            den = r[:, HEAD_DIM:]
            o = r[:, :HEAD_DIM] / den
            lse = mx + jnp.log(den)
            if others:
                o_all = [o] + [st[hd, rws, :] for st in stage_refs[0::2]]
                lse_all = [lse] + [st[hd, rws, :] for st in stage_refs[1::2]]
                top = functools.reduce(jnp.maximum, lse_all)
                wts = [jnp.exp(t - top) for t in lse_all]
                num = wts[0] * o_all[0]
                tot = wts[0]
                for wg, og in zip(wts[1:], o_all[1:]):
                    num = num + wg * og
                    tot = tot + wg
                y_ref[rws, lanes] = (num / tot).astype(BF16)
            else:
                o_ref[rws, lanes] = o
                lse_ref[rws, lanes] = lse


def _dil_band(q, k, v, others, *, batch, seq, dil, lt):
    rows, cols = q.shape
    bw = cols // dil
    heads = bw // HEAD_DIM
    ln = seq // dil
    lt = min(lt, ln)
    nl = ln // lt
    assert ln % lt == 0 and lt % DIL_BLOCK == 0 and ln >= 2 * DIL_BLOCK
    tile = pl.BlockSpec((lt, bw), lambda b, r, i: (b * nl + i, r))
    whole = pl.BlockSpec((ln, bw), lambda b, r, i: (b, r))
    if others:
        assert dil == 1 and all(lt % (od * SUBLANES) == 0 for od, _, _ in others)
        extra = [a for _, o, lse in others for a in (o, lse)]
        extra_specs = [pl.BlockSpec((lt // od, od * bw), lambda b, r, i: (b * nl + i, 0))
                       for od, _, _ in others for _ in range(2)]
        out_specs, out_shape = tile, jax.ShapeDtypeStruct((rows, cols), BF16)
        scratch = [pltpu.VMEM((heads, lt, HEAD_DIM), F32)] * len(extra)
    else:
        extra, extra_specs, scratch = [], [], []
        out_specs = [tile, tile]
        out_shape = [jax.ShapeDtypeStruct((rows, cols), F32)] * 2
    return pl.pallas_call(
        functools.partial(_dil_band_kernel, lt=lt, heads=heads, others=tuple(od for od, _, _ in others)),
        grid=(batch, dil, nl),
        in_specs=[tile, whole, whole] + extra_specs,
        out_specs=out_specs,
        out_shape=out_shape,
        scratch_shapes=scratch,
        compiler_params=_params("parallel", "parallel", "arbitrary"),
        name=f"dil_band_{dil}",
    )(q, k, v, *extra)


def _dil_attn(cq, ck, cv, *, batch, seq):
    dils = [d for _, d in DIL_PATTERNS]
    assert all(w // d == DIL_SPAN for w, d in DIL_PATTERNS) and dils[0] == 1
    others = []
    for idx in range(1, len(dils)):
        o, lse = _dil_band(cq[idx], ck[idx], cv[idx], [], batch=batch, seq=seq, dil=dils[idx], lt=512)
        others.append((dils[idx], o, lse))
    return _dil_band(cq[0], ck[0], cv[0], others, batch=batch, seq=seq, dil=1, lt=512)


def _diff_attn_kernel(q_ref, k_ref, v_ref, lq1_ref, lk1_ref, lq2_ref, lk2_ref, sub_ref, o_ref,
                      s_ref, m_ref, l_ref, acc_ref, *, tq, lam_init):
    i = pl.program_id(2)
    q = q_ref[...]
    lane = lax.broadcasted_iota(jnp.int32, q.shape, 1)
    zero = jnp.zeros_like(q)
    qq = jnp.concatenate([jnp.where(lane < DIFF_QK_DIM, q, zero), jnp.where(lane < DIFF_QK_DIM, zero, q)], axis=0)
    tiles = tq // LANES

    def lane_fold(x, op):
        r = x[:, 0:LANES]
        for t in range(1, tiles):
            r = op(r, x[:, t * LANES:(t + 1) * LANES])
        return r

    m_ref[...] = jnp.full(m_ref.shape, NEG_INF, F32)

    def scores(j, masked):
        k0 = pl.multiple_of(j * tq, tq)
        s = _dot_nt(qq, k_ref[pl.ds(k0, tq), :])
        if masked:
            row = lax.broadcasted_iota(jnp.int32, s.shape, 0)
            col = lax.broadcasted_iota(jnp.int32, s.shape, 1)
            row = jnp.where(row >= tq, row - tq, row)
            s = jnp.where(col <= row, s, NEG_INF)
        s_ref[j] = s
        m_ref[...] = jnp.maximum(m_ref[...], lane_fold(s, jnp.maximum))

    def scores_body(j, carry):
        scores(j, False)
        return carry

    lax.fori_loop(0, i, scores_body, 0)
    scores(i, True)
    m_ref[...] = jnp.broadcast_to(jnp.max(m_ref[...], axis=-1, keepdims=True), m_ref.shape)

    l_ref[...] = jnp.zeros(l_ref.shape, F32)
    acc_ref[...] = jnp.zeros(acc_ref.shape, F32)

    def accumulate(j, carry):
        k0 = pl.multiple_of(j * tq, tq)
        s = s_ref[j]
        mx = m_ref[...]
        ps = [jnp.exp(s[:, t * LANES:(t + 1) * LANES] - mx) for t in range(tiles)]
        part = ps[0]
        for t in range(1, tiles):
            part = part + ps[t]
        l_ref[...] += part
        p = jnp.concatenate(ps, axis=1).astype(BF16)
        acc_ref[...] += _dot(p, v_ref[pl.ds(k0, tq), :])
        return carry

    lax.fori_loop(0, i + 1, accumulate, 0)

    o = acc_ref[...] / jnp.sum(l_ref[...], axis=-1, keepdims=True)
    lam = (jnp.exp(jnp.sum(lq1_ref[...] * lk1_ref[...], axis=-1, keepdims=True))
           - jnp.exp(jnp.sum(lq2_ref[...] * lk2_ref[...], axis=-1, keepdims=True)) + np.float32(lam_init))
    od = o[0:tq, :] - lam * o[tq:2 * tq, :]
    o_ref[...] = (_rms(od, sub_ref[...]) * np.float32(1.0 - lam_init)).astype(o_ref.dtype)


def _diff_attn(q, k, v, lq1, lk1, lq2, lk2, subln, *, batch, seq, tq, lam_init):
    m, bw = q.shape
    heads = bw // HEAD_DIM
    nq = seq // tq
    q_spec = pl.BlockSpec((tq, HEAD_DIM), lambda b, h, i: (b * nq + i, h))
    kv_spec = pl.BlockSpec((seq, HEAD_DIM), lambda b, h, i: (b, h))
    lam_spec = pl.BlockSpec((1, DIFF_QK_DIM), lambda b, h, i: (0, 0))
    stat = pltpu.VMEM((2 * tq, LANES), F32)
    return pl.pallas_call(
        functools.partial(_diff_attn_kernel, tq=tq, lam_init=lam_init),
        grid=(batch, heads, nq),
        in_specs=[q_spec, kv_spec, kv_spec, lam_spec, lam_spec, lam_spec, lam_spec,
                  pl.BlockSpec((1, HEAD_DIM), lambda b, h, i: (0, 0))],
        out_specs=q_spec,
        out_shape=jax.ShapeDtypeStruct((m, bw), BF16),
        scratch_shapes=[pltpu.VMEM((nq, 2 * tq, tq), F32), stat, stat, stat],
        compiler_params=_params("parallel", "parallel", "arbitrary"),
        name="diff_attn",
    )(q, k, v, lq1, lk1, lq2, lk2, subln)


def _merge_kernel(ya_ref, yb_ref, yc_ref, yd_ref, ga_ref, gb_ref, gc_ref, gd_ref, wb_ref, o_ref):
    acc = None
    for b, (y_ref, gate_ref) in enumerate(((ya_ref, ga_ref), (yb_ref, gb_ref), (yc_ref, gc_ref), (yd_ref, gd_ref))):
        term = gate_ref[...].astype(F32) * _dot(y_ref[...], wb_ref[b])
        acc = term if acc is None else acc + term
    o_ref[...] = acc.astype(BF16)


def _merge(ys, gates, w_branch, *, tm, tn):
    m, bw = ys[0].shape
    nb, _, d = w_branch.shape
    assert nb == len(ys) == 4
    y_spec = pl.BlockSpec((tm, bw), lambda r, c: (r, 0))
    nc = d // tn
    gate_specs = [pl.BlockSpec((tm, tn), lambda r, c, b=b: (r, b * nc + c)) for b in range(nb)]
    return pl.pallas_call(
        _merge_kernel,
        grid=(m // tm, nc),
        in_specs=[y_spec] * nb + gate_specs + [pl.BlockSpec((nb, bw, tn), lambda r, c: (0, 0, c))],
        out_specs=pl.BlockSpec((tm, tn), lambda r, c: (r, c)),
        out_shape=jax.ShapeDtypeStruct((m, d), BF16),
        compiler_params=_params("parallel", "arbitrary"),
        name="merge_branches",
    )(*ys, gates, gates, gates, gates, w_branch)


def kernel(x, positions, ffn1_norm, ffn1_w13, ffn1_w2, mix_norm, w_in, conv_w, conv_b, conv_ln_g, conv_ln_b, sgu_ln_g, sgu_ln_b, sgu_w, sgu_b, dil_q_norm, dil_k_norm, diff_q_norm, diff_k_norm, diff_lq1, diff_lk1, diff_lq2, diff_lk2, diff_subln, w_branch, w_out, ffn2_norm, ffn2_w13, ffn2_w2):
    batch, seq, d = x.shape
    depth = w_in.shape[0]
    m = batch * seq
    bw = d // 4
    n_mix = 10 * bw
    assert bw % HEAD_DIM == 0 and w_in.shape[2] == n_mix + 4 * d

    tm = min(1024, m)
    dff = ffn1_w2.shape[1]
    tf = 512 if dff % 512 == 0 else 256
    tn = min(512, d)

    def row(v, l):
        return v[l].reshape(1, -1)

    def ffn(x2, g, w13, w2):
        act = _ffn_up(x2, g, w13.astype(BF16), tm=tm, tf=tf)
        return _residual_matmul(act, w2.astype(BF16), x2, tm=tm, tn=tn, name="ffn_down")

    x2 = x.reshape(m, d)
    cos_c, sin_c, cos_d, sin_d = _rope_tables(positions, ts=min(512, m))

    for l in range(depth):
        lam_init = 0.8 - 0.6 * math.exp(-0.3 * l)
        x2 = ffn(x2, row(ffn1_norm, l), ffn1_w13[l], ffn1_w2[l])

        z, yb, (cq, ck, cv), (dq, dk, dv) = _mix_proj(
            x2, row(mix_norm, l), w_in[l, :, :n_mix].astype(BF16), row(sgu_ln_g, l), row(sgu_ln_b, l),
            sgu_w[l], sgu_b[l], row(dil_q_norm, l), row(dil_k_norm, l), cos_c, sin_c,
            row(diff_q_norm, l), row(diff_k_norm, l), cos_d, sin_d, bw=bw, tm=min(512, m))
        gates = _gate_proj(x2, row(mix_norm, l), w_in[l, :, n_mix:].astype(BF16), tm=tm, tn=min(1024, 4 * d))

        ya = _conv_mixer(z, conv_w[l], row(conv_b, l), row(conv_ln_g, l), row(conv_ln_b, l),
                         batch=batch, seq=seq, ts=min(256, seq))
        yc = _dil_attn(cq, ck, cv, batch=batch, seq=seq)
        yd = _diff_attn(dq, dk, dv, row(diff_lq1, l), row(diff_lk1, l), row(diff_lq2, l), row(diff_lk2, l),
                        row(diff_subln, l), batch=batch, seq=seq, tq=min(512, seq), lam_init=lam_init)

        merged = _merge((ya, yb, yc, yd), gates, w_branch[l].astype(BF16), tm=tm, tn=tn)
        x2 = _residual_matmul(merged, w_out[l].astype(BF16), x2, tm=tm, tn=tn, name="out_proj")

        x2 = ffn(x2, row(ffn2_norm, l), ffn2_w13[l], ffn2_w2[l])
    return x2.reshape(batch, seq, d)
```

```python
import functools
import math

import jax
import jax.numpy as jnp
import numpy as np
from jax import lax
from jax.experimental import pallas as pl
from jax.experimental.pallas import tpu as pltpu

HEAD_DIM = 128
LANES = 128
SUBLANES = 8
DIFF_QK_DIM = HEAD_DIM // 2
SGU_CHUNK = 128
DIL_PATTERNS = ((128, 1), (512, 4), (2048, 16))
DIL_SPAN = 128
DIL_BLOCK = 128
ROPE_THETA = 500000.0
ROPE_FRACTION = 4
EPS = 1e-6
NEG_INF = -1e30
CONV_HALO = 32

VMEM_LIMIT_BYTES = 56 * 1024 * 1024

BF16 = jnp.bfloat16
F32 = jnp.float32


def _params(*sem):
    return pltpu.CompilerParams(dimension_semantics=sem, vmem_limit_bytes=VMEM_LIMIT_BYTES)


def _resident(shape, index_map):
    return pl.BlockSpec(shape, index_map, pipeline_mode=pl.Buffered(1))


def _dot(a, b):
    return jnp.dot(a, b, preferred_element_type=F32)


def _dot_nt(a, b):
    return lax.dot_general(a, b, (((1,), (1,)), ((), ())), preferred_element_type=F32)


def _rms(x, g):
    return x * lax.rsqrt(jnp.mean(x * x, axis=-1, keepdims=True) + EPS) * g


def _layernorm(x, g, b):
    mu = jnp.mean(x, axis=-1, keepdims=True)
    xc = x - mu
    return xc * lax.rsqrt(jnp.mean(xc * xc, axis=-1, keepdims=True) + EPS) * g + b


def _gelu(x):
    return 0.5 * x * (1.0 + lax.erf(x * np.float32(math.sqrt(0.5))))


def _ffn_up_kernel(x_ref, g_ref, wg_ref, wu_ref, a_ref, h_ref):
    @pl.when(pl.program_id(1) == 0)
    def _():
        h_ref[...] = _rms(x_ref[...], g_ref[...]).astype(BF16)

    h = h_ref[...]
    gate = _dot(h, wg_ref[...])
    up = _dot(h, wu_ref[...])
    a_ref[...] = (0.5 * gate * jax.nn.sigmoid(gate) * up).astype(BF16)


def _ffn_up(x2, g, w13, *, tm, tf):
    m, d = x2.shape
    dff = w13.shape[1] // 2
    nf = dff // tf
    return pl.pallas_call(
        _ffn_up_kernel,
        grid=(m // tm, nf),
        in_specs=[
            pl.BlockSpec((tm, d), lambda i, j: (i, 0)),
            pl.BlockSpec((1, d), lambda i, j: (0, 0)),
            pl.BlockSpec((d, tf), lambda i, j: (0, j)),
            pl.BlockSpec((d, tf), lambda i, j: (0, j + nf)),
        ],
        out_specs=pl.BlockSpec((tm, tf), lambda i, j: (i, j)),
        out_shape=jax.ShapeDtypeStruct((m, dff), BF16),
        scratch_shapes=[pltpu.VMEM((tm, d), BF16)],
        compiler_params=_params("parallel", "arbitrary"),
        name="ffn_up",
    )(x2, g, w13, w13)


def _residual_matmul_kernel(a_ref, w_ref, x_ref, o_ref):
    o_ref[...] = x_ref[...] + _dot(a_ref[...], w_ref[...])


def _residual_matmul(a, w, x2, *, tm, tn, name):
    m, kdim = a.shape
    d = w.shape[1]
    tile = pl.BlockSpec((tm, tn), lambda i, j: (i, j))
    return pl.pallas_call(
        _residual_matmul_kernel,
        grid=(m // tm, d // tn),
        in_specs=[pl.BlockSpec((tm, kdim), lambda i, j: (i, 0)), pl.BlockSpec((kdim, tn), lambda i, j: (0, j)), tile],
        out_specs=tile,
        out_shape=jax.ShapeDtypeStruct((m, d), F32),
        compiler_params=_params("parallel", "arbitrary"),
        name=name,
    )(a, w, x2)


def _rope_table_kernel(pos_ref, inv_c_ref, sgn_c_ref, inv_d_ref, sgn_d_ref,
                       cos_c_ref, sin_c_ref, cos_d_ref, sin_d_ref):
    p = pos_ref[...].astype(F32)
    ang_c = p * inv_c_ref[...]
    cos_c_ref[...] = jnp.cos(ang_c)
    sin_c_ref[...] = jnp.sin(ang_c) * sgn_c_ref[...]
    ang_d = p * inv_d_ref[...]
    cos_d_ref[...] = jnp.cos(ang_d)
    sin_d_ref[...] = jnp.sin(ang_d) * sgn_d_ref[...]


def _rope_lane_consts(width):
    rot = width // ROPE_FRACTION
    half = rot // 2
    inv_freq = 1.0 / (ROPE_THETA ** (jnp.arange(half, dtype=F32) * 2.0 / rot))
    lane = np.arange(LANES) % width
    in_rot = lane < rot
    inv = jnp.where(in_rot, inv_freq[lane % half], 0.0).astype(F32)
    sgn = np.where(lane < half, -1.0, np.where(in_rot, 1.0, 0.0)).astype(np.float32)
    return inv.reshape(1, LANES), jnp.asarray(sgn).reshape(1, LANES)


def _rope_tables(positions, *, ts):
    m = positions.size
    inv_c, sgn_c = _rope_lane_consts(HEAD_DIM)
    inv_d, sgn_d = _rope_lane_consts(DIFF_QK_DIM)
    lane_spec = pl.BlockSpec((1, LANES), lambda i: (0, 0))
    tab_spec = pl.BlockSpec((ts, LANES), lambda i: (i, 0))
    tab = jax.ShapeDtypeStruct((m, LANES), F32)
    return pl.pallas_call(
        _rope_table_kernel,
        grid=(m // ts,),
        in_specs=[pl.BlockSpec((ts, 1), lambda i: (i, 0)), lane_spec, lane_spec, lane_spec, lane_spec],
        out_specs=[tab_spec] * 4,
        out_shape=[tab] * 4,
        compiler_params=_params("parallel"),
        name="rope_tables",
    )(positions.reshape(m, 1), inv_c, sgn_c, inv_d, sgn_d)


def _rope(x, cos, sin, half):
    lane = lax.broadcasted_iota(jnp.int32, x.shape, 1)
    lower = (lane % (2 * half)) < half
    partner = jnp.where(lower, pltpu.roll(x, LANES - half, 1), pltpu.roll(x, half, 1))
    return x * cos + partner * sin


def _segment_rms(x, g):
    lane = lax.broadcasted_iota(jnp.int32, x.shape, 1)
    lower = lane < DIFF_QK_DIM
    sq = x * x
    lo = jnp.sum(jnp.where(lower, sq, 0.0), axis=-1, keepdims=True)
    hi = jnp.sum(jnp.where(lower, 0.0, sq), axis=-1, keepdims=True)
    ms = jnp.where(lower, lo, hi) * np.float32(1.0 / DIFF_QK_DIM)
    return x * lax.rsqrt(ms + EPS) * g


def _mix_proj_kernel(x_ref, g_ref, w_ref, sg_ref, sb_ref, ws_ref, bias_ref, cqg_ref, ckg_ref, cos_c_ref, sin_c_ref,
                     dqg_ref, dkg_ref, cos_d_ref, sin_d_ref,
                     z_ref, yb_ref, dq_ref, dk_ref, dv_ref, *c_refs_and_scratch, bw, tm, dils):
    c_refs, stage_ref = c_refs_and_scratch[:-1], c_refs_and_scratch[-1]
    h = _rms(x_ref[...], g_ref[...]).astype(BF16)
    heads = bw // HEAD_DIM

    def seg(c0, n):
        return _dot(h, w_ref[:, c0 * bw:(c0 + n) * bw])

    r = seg(2, 2)
    u = _gelu(r[:, :bw])
    v = _layernorm(_gelu(r[:, bw:]), sg_ref[...], sb_ref[...]).astype(BF16)
    row = lax.broadcasted_iota(jnp.int32, (SGU_CHUNK, SGU_CHUNK), 0)
    col = lax.broadcasted_iota(jnp.int32, (SGU_CHUNK, SGU_CHUNK), 1)
    for g in range(heads):
        lanes = slice(g * HEAD_DIM, (g + 1) * HEAD_DIM)
        w = jnp.where(row >= col, ws_ref[g], 0.0).astype(BF16)
        for c in range(tm // SGU_CHUNK):
            rws = slice(c * SGU_CHUNK, (c + 1) * SGU_CHUNK)
            mixed = _dot(w, v[rws, lanes]) + bias_ref[:, lanes]
            yb_ref[rws, lanes] = (u[rws, lanes] * mixed).astype(BF16)

    r = seg(4, 3)
    cos, sin = cos_c_ref[...], sin_c_ref[...]
    half = HEAD_DIM // ROPE_FRACTION // 2
    for part in range(3):
        outs = c_refs[part * len(dils):(part + 1) * len(dils)]
        for hd in range(heads):
            lanes = slice(hd * HEAD_DIM, (hd + 1) * HEAD_DIM)
            t = r[:, part * bw + hd * HEAD_DIM:part * bw + (hd + 1) * HEAD_DIM]
            if part < 2:
                t = _rope(_rms(t, (cqg_ref, ckg_ref)[part][...]), cos, sin, half)
            stage_ref[hd] = t
            for dil, o_ref in zip(dils, outs):
                if dil == 1:
                    o_ref[:, lanes] = t.astype(BF16)
        for dil, o_ref in zip(dils, outs):
            if dil == 1:
                continue
            for res in range(dil):
                for hd in range(heads):
                    o_ref[:, res * bw + hd * HEAD_DIM:res * bw + (hd + 1) * HEAD_DIM] = (
                        stage_ref[hd, pl.ds(res, tm // dil, stride=dil), :].astype(BF16))

    r = seg(7, 3)
    cos, sin = cos_d_ref[...], sin_d_ref[...]
    half = DIFF_QK_DIM // ROPE_FRACTION // 2
    scale = np.float32(DIFF_QK_DIM ** -0.5)
    for hd in range(heads):
        lanes = slice(hd * HEAD_DIM, (hd + 1) * HEAD_DIM)
        klanes = slice(bw + hd * HEAD_DIM, bw + (hd + 1) * HEAD_DIM)
        q = _rope(_segment_rms(r[:, lanes], dqg_ref[...]), cos, sin, half)
        dq_ref[:, lanes] = (q * scale).astype(BF16)
        dk_ref[:, lanes] = _rope(_segment_rms(r[:, klanes], dkg_ref[...]), cos, sin, half).astype(BF16)
    dv_ref[...] = r[:, 2 * bw:].astype(BF16)

    r = seg(0, 2)
    z_ref[...] = r[:, :bw] * jax.nn.sigmoid(r[:, bw:])


def _mix_proj(x2, g, w_mix, sgu_g, sgu_b, w_s, b_s, cq_g, ck_g, cos_c, sin_c, dq_g, dk_g, cos_d, sin_d, *, bw, tm):
    m, d = x2.shape
    groups = w_s.shape[0]
    heads = bw // HEAD_DIM
    dils = tuple(dil for _, dil in DIL_PATTERNS)
    assert groups == heads and w_mix.shape[1] == 10 * bw and all(tm % (dil * 2 * SUBLANES) == 0 for dil in dils)
    bias = jnp.repeat(b_s.T, HEAD_DIM, axis=1)
    vec = pl.BlockSpec((1, bw), lambda i: (0, 0))
    lane_spec = pl.BlockSpec((1, LANES), lambda i: (0, 0))
    tab_spec = pl.BlockSpec((tm, LANES), lambda i: (i, 0))
    out_spec = pl.BlockSpec((tm, bw), lambda i: (i, 0))
    out_bf = jax.ShapeDtypeStruct((m, bw), BF16)
    view_specs = [pl.BlockSpec((tm // dil, dil * bw), lambda i: (i, 0)) for dil in dils] * 3
    view_shapes = [jax.ShapeDtypeStruct((m // dil, dil * bw), BF16) for dil in dils] * 3
    outs = pl.pallas_call(
        functools.partial(_mix_proj_kernel, bw=bw, tm=tm, dils=dils),
        grid=(m // tm,),
        in_specs=[
            pl.BlockSpec((tm, d), lambda i: (i, 0)),
            pl.BlockSpec((1, d), lambda i: (0, 0)),
            _resident((d, 10 * bw), lambda i: (0, 0)),
            vec, vec,
            pl.BlockSpec((groups, SGU_CHUNK, SGU_CHUNK), lambda i: (0, 0, 0)),
            pl.BlockSpec((SGU_CHUNK, bw), lambda i: (0, 0)),
            lane_spec, lane_spec, tab_spec, tab_spec,
            lane_spec, lane_spec, tab_spec, tab_spec,
        ],
        out_specs=[out_spec] * 5 + view_specs,
        out_shape=[jax.ShapeDtypeStruct((m, bw), F32)] + [out_bf] * 4 + view_shapes,
        scratch_shapes=[pltpu.VMEM((heads, tm, HEAD_DIM), F32)],
        compiler_params=_params("parallel"),
        name="mix_proj",
    )(x2, g, w_mix, sgu_g, sgu_b, w_s, bias, cq_g, ck_g, cos_c, sin_c,
      jnp.tile(dq_g, (1, 2)), jnp.tile(dk_g, (1, 2)), cos_d, sin_d)
    n = len(dils)
    z, yb, dq, dk, dv = outs[:5]
    cq, ck, cv = outs[5:5 + n], outs[5 + n:5 + 2 * n], outs[5 + 2 * n:]
    return z, yb, (cq, ck, cv), (dq, dk, dv)


def _gate_proj_kernel(x_ref, g_ref, w_ref, o_ref, h_ref):
    @pl.when(pl.program_id(1) == 0)
    def _():
        h_ref[...] = _rms(x_ref[...], g_ref[...]).astype(BF16)

    o_ref[...] = jax.nn.sigmoid(_dot(h_ref[...], w_ref[...])).astype(BF16)


def _gate_proj(x2, g, w_gate, *, tm, tn):
    m, d = x2.shape
    n = w_gate.shape[1]
    return pl.pallas_call(
        _gate_proj_kernel,
        grid=(m // tm, n // tn),
        in_specs=[pl.BlockSpec((tm, d), lambda i, j: (i, 0)), pl.BlockSpec((1, d), lambda i, j: (0, 0)),
                  pl.BlockSpec((d, tn), lambda i, j: (0, j))],
        out_specs=pl.BlockSpec((tm, tn), lambda i, j: (i, j)),
        out_shape=jax.ShapeDtypeStruct((m, n), BF16),
        scratch_shapes=[pltpu.VMEM((tm, d), BF16)],
        compiler_params=_params("parallel", "arbitrary"),
        name="gate_proj",
    )(x2, g, w_gate)


def _conv_kernel(zin_ref, w_ref, b_ref, lg_ref, lb_ref, o_ref, z_ref, *, ts, width, rows):
    ext = ts + CONV_HALO

    @pl.when(pl.program_id(1) == 0)
    def _():
        z_ref[0, 0:CONV_HALO, :] = jnp.zeros((CONV_HALO, z_ref.shape[2]), F32)

    z_ref[0, CONV_HALO:ext, :] = zin_ref[...]
    for s in range(1, SUBLANES):
        z_ref[s, 0:ext - SUBLANES, :] = z_ref[0, s:s + ext - SUBLANES, :]
    ch = z_ref.shape[2]
    for c in range(ts // rows):
        r0 = c * rows
        acc = jnp.broadcast_to(b_ref[...], (rows, ch))
        for k in range(width):
            off = CONV_HALO - (width - 1) + k
            base = r0 + off // SUBLANES * SUBLANES
            acc = acc + w_ref[k:k + 1, :] * z_ref[off % SUBLANES, base:base + rows, :]
        y = _layernorm(acc, lg_ref[...], lb_ref[...])
        o_ref[r0:r0 + rows, :] = (y * jax.nn.sigmoid(y)).astype(o_ref.dtype)
    z_ref[0, 0:CONV_HALO, :] = z_ref[0, ts:ext, :]


def _conv_mixer(z, conv_w, conv_b, ln_g, ln_b, *, batch, seq, ts):
    m, bw = z.shape
    width = conv_w.shape[0]
    assert width - 1 <= CONV_HALO
    ns = seq // ts
    vec = pl.BlockSpec((1, bw), lambda b, s: (0, 0))
    tile = pl.BlockSpec((ts, bw), lambda b, s: (b * ns + s, 0))
    return pl.pallas_call(
        functools.partial(_conv_kernel, ts=ts, width=width, rows=64),
        grid=(batch, ns),
        in_specs=[tile, pl.BlockSpec((width, bw), lambda b, s: (0, 0)), vec, vec, vec],
        out_specs=tile,
        out_shape=jax.ShapeDtypeStruct((m, bw), BF16),
        scratch_shapes=[pltpu.VMEM((SUBLANES, ts + CONV_HALO, bw), F32)],
        compiler_params=_params("arbitrary", "arbitrary"),
        name="conv_mixer",
    )(z, conv_w, conv_b, ln_g, ln_b)


def _dil_band_kernel(*refs, lt, heads, others):
    bw = heads * HEAD_DIM
    if others:
        q_ref, k_ref, v_ref = refs[:3]
        other_refs = refs[3:3 + 2 * len(others)]
        y_ref = refs[3 + 2 * len(others)]
        stage_refs = refs[4 + 2 * len(others):]
        for dil, o_ref, lse_ref, o_st, lse_st in zip(others, other_refs[0::2], other_refs[1::2],
                                                     stage_refs[0::2], stage_refs[1::2]):
            for res in range(dil):
                for hd in range(heads):
                    cols = slice(res * bw + hd * HEAD_DIM, res * bw + (hd + 1) * HEAD_DIM)
                    o_st[hd, pl.ds(res, lt // dil, stride=dil), :] = o_ref[:, cols]
                    lse_st[hd, pl.ds(res, lt // dil, stride=dil), :] = lse_ref[:, cols]
    else:
        q_ref, k_ref, v_ref, o_ref, lse_ref = refs
    i = pl.program_id(2)
    blk = DIL_BLOCK
    scale = np.float32(HEAD_DIM ** -0.5)
    ones = jnp.ones((2 * blk, HEAD_DIM), BF16)
    row = lax.broadcasted_iota(jnp.int32, (blk, 2 * blk), 0)
    col = lax.broadcasted_iota(jnp.int32, (blk, 2 * blk), 1)
    for jb in range(lt // blk):
        rws = slice(jb * blk, (jb + 1) * blk)
        l0 = i * lt + jb * blk
        ks = pl.multiple_of(jnp.maximum(l0 - blk, 0), blk)
        dist = (l0 - ks) + row - col
        valid = (dist >= 0) & (dist <= DIL_SPAN)
        for hd in range(heads):
            lanes = slice(hd * HEAD_DIM, (hd + 1) * HEAD_DIM)
            s = _dot_nt(q_ref[rws, lanes], k_ref[pl.ds(ks, 2 * blk), lanes]) * scale
            s = jnp.where(valid, s, NEG_INF)
            mx = jnp.max(s, axis=-1, keepdims=True)
            p = jnp.exp(s - mx).astype(BF16)
            r = _dot(p, jnp.concatenate([v_ref[pl.ds(ks, 2 * blk), lanes], ones], axis=1))
            den = r[:, HEAD_DIM:]
            o = r[:, :HEAD_DIM] / den
            lse = mx + jnp.log(den)
            if others:
                o_all = [o] + [st[hd, rws, :] for st in stage_refs[0::2]]
                lse_all = [lse] + [st[hd, rws, :] for st in stage_refs[1::2]]
                top = functools.reduce(jnp.maximum, lse_all)
                wts = [jnp.exp(t - top) for t in lse_all]
                num = wts[0] * o_all[0]
                tot = wts[0]
                for wg, og in zip(wts[1:], o_all[1:]):
                    num = num + wg * og
                    tot = tot + wg
                y_ref[rws, lanes] = (num / tot).astype(BF16)
            else:
                o_ref[rws, lanes] = o
                lse_ref[rws, lanes] = lse


def _dil_band(q, k, v, others, *, batch, seq, dil, lt):
    rows, cols = q.shape
    bw = cols // dil
    heads = bw // HEAD_DIM
    ln = seq // dil
    lt = min(lt, ln)
    nl = ln // lt
    assert ln % lt == 0 and lt % DIL_BLOCK == 0 and ln >= 2 * DIL_BLOCK
    tile = pl.BlockSpec((lt, bw), lambda b, r, i: (b * nl + i, r))
    whole = pl.BlockSpec((ln, bw), lambda b, r, i: (b, r))
    if others:
        assert dil == 1 and all(lt % (od * SUBLANES) == 0 for od, _, _ in others)
        extra = [a for _, o, lse in others for a in (o, lse)]
        extra_specs = [pl.BlockSpec((lt // od, od * bw), lambda b, r, i: (b * nl + i, 0))
                       for od, _, _ in others for _ in range(2)]
        out_specs, out_shape = tile, jax.ShapeDtypeStruct((rows, cols), BF16)
        scratch = [pltpu.VMEM((heads, lt, HEAD_DIM), F32)] * len(extra)
    else:
        extra, extra_specs, scratch = [], [], []
        out_specs = [tile, tile]
        out_shape = [jax.ShapeDtypeStruct((rows, cols), F32)] * 2
    return pl.pallas_call(
        functools.partial(_dil_band_kernel, lt=lt, heads=heads, others=tuple(od for od, _, _ in others)),
        grid=(batch, dil, nl),
        in_specs=[tile, whole, whole] + extra_specs,
        out_specs=out_specs,
        out_shape=out_shape,
        scratch_shapes=scratch,
        compiler_params=_params("parallel", "parallel", "arbitrary"),
        name=f"dil_band_{dil}",
    )(q, k, v, *extra)


def _dil_attn(cq, ck, cv, *, batch, seq):
    dils = [d for _, d in DIL_PATTERNS]
    assert all(w // d == DIL_SPAN for w, d in DIL_PATTERNS) and dils[0] == 1
    others = []
    for idx in range(1, len(dils)):
        o, lse = _dil_band(cq[idx], ck[idx], cv[idx], [], batch=batch, seq=seq, dil=dils[idx], lt=512)
        others.append((dils[idx], o, lse))
    return _dil_band(cq[0], ck[0], cv[0], others, batch=batch, seq=seq, dil=1, lt=512)


def _diff_attn_kernel(q_ref, k_ref, v_ref, lq1_ref, lk1_ref, lq2_ref, lk2_ref, sub_ref, o_ref,
                      s_ref, m_ref, l_ref, acc_ref, *, tq, lam_init):
    i = pl.program_id(2)
    q = q_ref[...]
    lane = lax.broadcasted_iota(jnp.int32, q.shape, 1)
    zero = jnp.zeros_like(q)
    qq = jnp.concatenate([jnp.where(lane < DIFF_QK_DIM, q, zero), jnp.where(lane < DIFF_QK_DIM, zero, q)], axis=0)
    tiles = tq // LANES

    def lane_fold(x, op):
        r = x[:, 0:LANES]
        for t in range(1, tiles):
            r = op(r, x[:, t * LANES:(t + 1) * LANES])
        return r

    m_ref[...] = jnp.full(m_ref.shape, NEG_INF, F32)

    def scores(j, masked):
        k0 = pl.multiple_of(j * tq, tq)
        s = _dot_nt(qq, k_ref[pl.ds(k0, tq), :]) * np.float32(math.log2(math.e))
        if masked:
            row = lax.broadcasted_iota(jnp.int32, s.shape, 0)
            col = lax.broadcasted_iota(jnp.int32, s.shape, 1)
            row = jnp.where(row >= tq, row - tq, row)
            s = jnp.where(col <= row, s, NEG_INF)
        s_ref[j] = s
        m_ref[...] = jnp.maximum(m_ref[...], lane_fold(s, jnp.maximum))

    def accumulate(j):
        k0 = pl.multiple_of(j * tq, tq)
        s = s_ref[j]
        mx = m_ref[...]
        ps = [jnp.exp2(s[:, t * LANES:(t + 1) * LANES] - mx) for t in range(tiles)]
        part = ps[0]
        for t in range(1, tiles):
            part = part + ps[t]
        l_ref[...] += part
        p = jnp.concatenate(ps, axis=1).astype(BF16)
        acc_ref[...] += _dot(p, v_ref[pl.ds(k0, tq), :])

    def in_pairs(fn, count):
        def body(jp, carry):
            fn(2 * jp)
            fn(2 * jp + 1)
            return carry

        lax.fori_loop(0, count // 2, body, 0)

        @pl.when(count % 2 == 1)
        def _():
            fn(count - 1)

    in_pairs(lambda j: scores(j, False), i)
    scores(i, True)
    m_ref[...] = jnp.broadcast_to(jnp.max(m_ref[...], axis=-1, keepdims=True), m_ref.shape)

    l_ref[...] = jnp.zeros(l_ref.shape, F32)
    acc_ref[...] = jnp.zeros(acc_ref.shape, F32)
    in_pairs(accumulate, i + 1)

    o = acc_ref[...] / jnp.sum(l_ref[...], axis=-1, keepdims=True)
    lam = (jnp.exp(jnp.sum(lq1_ref[...] * lk1_ref[...], axis=-1, keepdims=True))
           - jnp.exp(jnp.sum(lq2_ref[...] * lk2_ref[...], axis=-1, keepdims=True)) + np.float32(lam_init))
    od = o[0:tq, :] - lam * o[tq:2 * tq, :]
    o_ref[...] = (_rms(od, sub_ref[...]) * np.float32(1.0 - lam_init)).astype(o_ref.dtype)


def _diff_attn(q, k, v, lq1, lk1, lq2, lk2, subln, *, batch, seq, tq, lam_init):
    m, bw = q.shape
    heads = bw // HEAD_DIM
    nq = seq // tq
    q_spec = pl.BlockSpec((tq, HEAD_DIM), lambda b, h, i: (b * nq + i, h))
    kv_spec = pl.BlockSpec((seq, HEAD_DIM), lambda b, h, i: (b, h))
    lam_spec = pl.BlockSpec((1, DIFF_QK_DIM), lambda b, h, i: (0, 0))
    stat = pltpu.VMEM((2 * tq, LANES), F32)
    return pl.pallas_call(
        functools.partial(_diff_attn_kernel, tq=tq, lam_init=lam_init),
        grid=(batch, heads, nq),
        in_specs=[q_spec, kv_spec, kv_spec, lam_spec, lam_spec, lam_spec, lam_spec,
                  pl.BlockSpec((1, HEAD_DIM), lambda b, h, i: (0, 0))],
        out_specs=q_spec,
        out_shape=jax.ShapeDtypeStruct((m, bw), BF16),
        scratch_shapes=[pltpu.VMEM((nq, 2 * tq, tq), F32), stat, stat, stat],
        compiler_params=_params("parallel", "parallel", "arbitrary"),
        name="diff_attn",
    )(q, k, v, lq1, lk1, lq2, lk2, subln)


def _merge_kernel(ya_ref, yb_ref, yc_ref, yd_ref, ga_ref, gb_ref, gc_ref, gd_ref, wb_ref, o_ref):
    acc = None
    for b, (y_ref, gate_ref) in enumerate(((ya_ref, ga_ref), (yb_ref, gb_ref), (yc_ref, gc_ref), (yd_ref, gd_ref))):
        term = gate_ref[...].astype(F32) * _dot(y_ref[...], wb_ref[b])
        acc = term if acc is None else acc + term
    o_ref[...] = acc.astype(BF16)


def _merge(ys, gates, w_branch, *, tm, tn):
    m, bw = ys[0].shape
    nb, _, d = w_branch.shape
    assert nb == len(ys) == 4
    y_spec = pl.BlockSpec((tm, bw), lambda r, c: (r, 0))
    nc = d // tn
    gate_specs = [pl.BlockSpec((tm, tn), lambda r, c, b=b: (r, b * nc + c)) for b in range(nb)]
    return pl.pallas_call(
        _merge_kernel,
        grid=(m // tm, nc),
        in_specs=[y_spec] * nb + gate_specs + [pl.BlockSpec((nb, bw, tn), lambda r, c: (0, 0, c))],
        out_specs=pl.BlockSpec((tm, tn), lambda r, c: (r, c)),
        out_shape=jax.ShapeDtypeStruct((m, d), BF16),
        compiler_params=_params("parallel", "arbitrary"),
        name="merge_branches",
    )(*ys, gates, gates, gates, gates, w_branch)


def kernel(x, positions, ffn1_norm, ffn1_w13, ffn1_w2, mix_norm, w_in, conv_w, conv_b, conv_ln_g, conv_ln_b, sgu_ln_g, sgu_ln_b, sgu_w, sgu_b, dil_q_norm, dil_k_norm, diff_q_norm, diff_k_norm, diff_lq1, diff_lk1, diff_lq2, diff_lk2, diff_subln, w_branch, w_out, ffn2_norm, ffn2_w13, ffn2_w2):
    batch, seq, d = x.shape
    depth = w_in.shape[0]
    m = batch * seq
    bw = d // 4
    n_mix = 10 * bw
    assert bw % HEAD_DIM == 0 and w_in.shape[2] == n_mix + 4 * d

    tm = min(1024, m)
    dff = ffn1_w2.shape[1]
    tf = 512 if dff % 512 == 0 else 256
    tn = min(512, d)

    def row(v, l):
        return v[l].reshape(1, -1)

    def ffn(x2, g, w13, w2):
        act = _ffn_up(x2, g, w13.astype(BF16), tm=tm, tf=tf)
        return _residual_matmul(act, w2.astype(BF16), x2, tm=tm, tn=tn, name="ffn_down")

    x2 = x.reshape(m, d)
    cos_c, sin_c, cos_d, sin_d = _rope_tables(positions, ts=min(512, m))

    for l in range(depth):
        lam_init = 0.8 - 0.6 * math.exp(-0.3 * l)
        x2 = ffn(x2, row(ffn1_norm, l), ffn1_w13[l], ffn1_w2[l])

        z, yb, (cq, ck, cv), (dq, dk, dv) = _mix_proj(
            x2, row(mix_norm, l), w_in[l, :, :n_mix].astype(BF16), row(sgu_ln_g, l), row(sgu_ln_b, l),
            sgu_w[l], sgu_b[l], row(dil_q_norm, l), row(dil_k_norm, l), cos_c, sin_c,
            row(diff_q_norm, l), row(diff_k_norm, l), cos_d, sin_d, bw=bw, tm=min(512, m))
        gates = _gate_proj(x2, row(mix_norm, l), w_in[l, :, n_mix:].astype(BF16), tm=tm, tn=min(1024, 4 * d))

        ya = _conv_mixer(z, conv_w[l], row(conv_b, l), row(conv_ln_g, l), row(conv_ln_b, l),
                         batch=batch, seq=seq, ts=min(512, seq))
        yc = _dil_attn(cq, ck, cv, batch=batch, seq=seq)
        yd = _diff_attn(dq, dk, dv, row(diff_lq1, l), row(diff_lk1, l), row(diff_lq2, l), row(diff_lk2, l),
                        row(diff_subln, l), batch=batch, seq=seq, tq=min(512, seq), lam_init=lam_init)

        merged = _merge((ya, yb, yc, yd), gates, w_branch[l].astype(BF16), tm=tm, tn=min(1024, d))
        x2 = _residual_matmul(merged, w_out[l].astype(BF16), x2, tm=tm, tn=min(1024, d), name="out_proj")

        x2 = ffn(x2, row(ffn2_norm, l), ffn2_w13[l], ffn2_w2[l])
    return x2.reshape(batch, seq, d)
```

```python
import functools
import math

import jax
import jax.numpy as jnp
import numpy as np
from jax import lax
from jax.experimental import pallas as pl
from jax.experimental.pallas import tpu as pltpu

HEAD_DIM = 128
LANES = 128
SUBLANES = 8
DIFF_QK_DIM = HEAD_DIM // 2
SGU_CHUNK = 128
DIL_PATTERNS = ((128, 1), (512, 4), (2048, 16))
DIL_SPAN = 128
DIL_BLOCK = 128
ROPE_THETA = 500000.0
ROPE_FRACTION = 4
EPS = 1e-6
NEG_INF = -1e30
CONV_HALO = 32

VMEM_LIMIT_BYTES = 56 * 1024 * 1024

BF16 = jnp.bfloat16
F32 = jnp.float32


def _params(*sem):
    return pltpu.CompilerParams(dimension_semantics=sem, vmem_limit_bytes=VMEM_LIMIT_BYTES)


def _resident(shape, index_map):
    return pl.BlockSpec(shape, index_map, pipeline_mode=pl.Buffered(1))


def _dot(a, b):
    return jnp.dot(a, b, preferred_element_type=F32)


def _dot_nt(a, b):
    return lax.dot_general(a, b, (((1,), (1,)), ((), ())), preferred_element_type=F32)


def _rms(x, g):
    return x * lax.rsqrt(jnp.mean(x * x, axis=-1, keepdims=True) + EPS) * g


def _layernorm(x, g, b):
    mu = jnp.mean(x, axis=-1, keepdims=True)
    xc = x - mu
    return xc * lax.rsqrt(jnp.mean(xc * xc, axis=-1, keepdims=True) + EPS) * g + b


def _gelu(x):
    return 0.5 * x * (1.0 + lax.erf(x * np.float32(math.sqrt(0.5))))


def _ffn_up_kernel(x_ref, g_ref, wg_ref, wu_ref, a_ref, h_ref):
    @pl.when(pl.program_id(1) == 0)
    def _():
        h_ref[...] = _rms(x_ref[...], g_ref[...]).astype(BF16)

    h = h_ref[...]
    gate = _dot(h, wg_ref[...].astype(BF16))
    up = _dot(h, wu_ref[...].astype(BF16))
    a_ref[...] = (0.5 * gate * jax.nn.sigmoid(gate) * up).astype(BF16)


def _ffn_up(x2, g, w13_layers, layer, *, tm, tf):
    m, d = x2.shape
    dff = w13_layers.shape[2] // 2
    nf = dff // tf
    return pl.pallas_call(
        _ffn_up_kernel,
        grid=(m // tm, nf),
        in_specs=[
            pl.BlockSpec((tm, d), lambda i, j: (i, 0)),
            pl.BlockSpec((1, d), lambda i, j: (0, 0)),
            pl.BlockSpec((None, d, tf), lambda i, j: (layer, 0, j)),
            pl.BlockSpec((None, d, tf), lambda i, j: (layer, 0, j + nf)),
        ],
        out_specs=pl.BlockSpec((tm, tf), lambda i, j: (i, j)),
        out_shape=jax.ShapeDtypeStruct((m, dff), BF16),
        scratch_shapes=[pltpu.VMEM((tm, d), BF16)],
        compiler_params=_params("parallel", "arbitrary"),
        name="ffn_up",
    )(x2, g, w13_layers, w13_layers)


def _residual_matmul_kernel(a_ref, w_ref, x_ref, o_ref):
    o_ref[...] = x_ref[...] + _dot(a_ref[...], w_ref[...])


def _residual_matmul(a, w, x2, *, tm, tn, name):
    m, kdim = a.shape
    d = w.shape[1]
    tile = pl.BlockSpec((tm, tn), lambda i, j: (i, j))
    w_spec = _resident if tn == d else pl.BlockSpec
    return pl.pallas_call(
        _residual_matmul_kernel,
        grid=(m // tm, d // tn),
        in_specs=[pl.BlockSpec((tm, kdim), lambda i, j: (i, 0)), w_spec((kdim, tn), lambda i, j: (0, j)), tile],
        out_specs=tile,
        out_shape=jax.ShapeDtypeStruct((m, d), F32),
        compiler_params=_params("parallel", "arbitrary"),
        name=name,
    )(a, w, x2)


def _rope_table_kernel(pos_ref, inv_c_ref, sgn_c_ref, inv_d_ref, sgn_d_ref,
                       cos_c_ref, sin_c_ref, cos_d_ref, sin_d_ref):
    p = pos_ref[...].astype(F32)
    ang_c = p * inv_c_ref[...]
    cos_c_ref[...] = jnp.cos(ang_c)
    sin_c_ref[...] = jnp.sin(ang_c) * sgn_c_ref[...]
    ang_d = p * inv_d_ref[...]
    cos_d_ref[...] = jnp.cos(ang_d)
    sin_d_ref[...] = jnp.sin(ang_d) * sgn_d_ref[...]


def _rope_lane_consts(width):
    rot = width // ROPE_FRACTION
    half = rot // 2
    inv_freq = 1.0 / (ROPE_THETA ** (jnp.arange(half, dtype=F32) * 2.0 / rot))
    lane = np.arange(LANES) % width
    in_rot = lane < rot
    inv = jnp.where(in_rot, inv_freq[lane % half], 0.0).astype(F32)
    sgn = np.where(lane < half, -1.0, np.where(in_rot, 1.0, 0.0)).astype(np.float32)
    return inv.reshape(1, LANES), jnp.asarray(sgn).reshape(1, LANES)


def _rope_tables(positions, *, ts):
    m = positions.size
    inv_c, sgn_c = _rope_lane_consts(HEAD_DIM)
    inv_d, sgn_d = _rope_lane_consts(DIFF_QK_DIM)
    lane_spec = pl.BlockSpec((1, LANES), lambda i: (0, 0))
    tab_spec = pl.BlockSpec((ts, LANES), lambda i: (i, 0))
    tab = jax.ShapeDtypeStruct((m, LANES), F32)
    return pl.pallas_call(
        _rope_table_kernel,
        grid=(m // ts,),
        in_specs=[pl.BlockSpec((ts, 1), lambda i: (i, 0)), lane_spec, lane_spec, lane_spec, lane_spec],
        out_specs=[tab_spec] * 4,
        out_shape=[tab] * 4,
        compiler_params=_params("parallel"),
        name="rope_tables",
    )(positions.reshape(m, 1), inv_c, sgn_c, inv_d, sgn_d)


def _rope(x, cos, sin, half):
    lane = lax.broadcasted_iota(jnp.int32, x.shape, 1)
    lower = (lane % (2 * half)) < half
    partner = jnp.where(lower, pltpu.roll(x, LANES - half, 1), pltpu.roll(x, half, 1))
    return x * cos + partner * sin


def _segment_rms(x, g):
    lane = lax.broadcasted_iota(jnp.int32, x.shape, 1)
    lower = lane < DIFF_QK_DIM
    sq = x * x
    lo = jnp.sum(jnp.where(lower, sq, 0.0), axis=-1, keepdims=True)
    hi = jnp.sum(jnp.where(lower, 0.0, sq), axis=-1, keepdims=True)
    ms = jnp.where(lower, lo, hi) * np.float32(1.0 / DIFF_QK_DIM)
    return x * lax.rsqrt(ms + EPS) * g


def _mix_proj_kernel(x_ref, g_ref, w_ref, sg_ref, sb_ref, ws_ref, bias_ref, cqg_ref, ckg_ref, cos_c_ref, sin_c_ref,
                     dqg_ref, dkg_ref, cos_d_ref, sin_d_ref,
                     z_ref, yb_ref, dq_ref, dk_ref, dv_ref, *c_refs_and_scratch, bw, tm, dils):
    c_refs, stage_ref = c_refs_and_scratch[:-1], c_refs_and_scratch[-1]
    h = _rms(x_ref[...], g_ref[...]).astype(BF16)
    heads = bw // HEAD_DIM

    def seg(c0, n):
        return _dot(h, w_ref[:, c0 * bw:(c0 + n) * bw])

    r = seg(2, 2)
    u = _gelu(r[:, :bw])
    v = _layernorm(_gelu(r[:, bw:]), sg_ref[...], sb_ref[...]).astype(BF16)
    row = lax.broadcasted_iota(jnp.int32, (SGU_CHUNK, SGU_CHUNK), 0)
    col = lax.broadcasted_iota(jnp.int32, (SGU_CHUNK, SGU_CHUNK), 1)
    for g in range(heads):
        lanes = slice(g * HEAD_DIM, (g + 1) * HEAD_DIM)
        w = jnp.where(row >= col, ws_ref[g], 0.0).astype(BF16)
        for c in range(tm // SGU_CHUNK):
            rws = slice(c * SGU_CHUNK, (c + 1) * SGU_CHUNK)
            mixed = _dot(w, v[rws, lanes]) + bias_ref[:, lanes]
            yb_ref[rws, lanes] = (u[rws, lanes] * mixed).astype(BF16)

    r = seg(4, 3)
    cos, sin = cos_c_ref[...], sin_c_ref[...]
    half = HEAD_DIM // ROPE_FRACTION // 2
    for part in range(3):
        outs = c_refs[part * len(dils):(part + 1) * len(dils)]
        for hd in range(heads):
            lanes = slice(hd * HEAD_DIM, (hd + 1) * HEAD_DIM)
            t = r[:, part * bw + hd * HEAD_DIM:part * bw + (hd + 1) * HEAD_DIM]
            if part < 2:
                t = _rope(_rms(t, (cqg_ref, ckg_ref)[part][...]), cos, sin, half)
            stage_ref[hd] = t
            for dil, o_ref in zip(dils, outs):
                if dil == 1:
                    o_ref[:, lanes] = t.astype(BF16)
        for dil, o_ref in zip(dils, outs):
            if dil == 1:
                continue
            for res in range(dil):
                for hd in range(heads):
                    o_ref[:, res * bw + hd * HEAD_DIM:res * bw + (hd + 1) * HEAD_DIM] = (
                        stage_ref[hd, pl.ds(res, tm // dil, stride=dil), :].astype(BF16))

    r = seg(7, 3)
    cos, sin = cos_d_ref[...], sin_d_ref[...]
    half = DIFF_QK_DIM // ROPE_FRACTION // 2
    scale = np.float32(DIFF_QK_DIM ** -0.5)
    for hd in range(heads):
        lanes = slice(hd * HEAD_DIM, (hd + 1) * HEAD_DIM)
        klanes = slice(bw + hd * HEAD_DIM, bw + (hd + 1) * HEAD_DIM)
        q = _rope(_segment_rms(r[:, lanes], dqg_ref[...]), cos, sin, half)
        dq_ref[:, lanes] = (q * scale).astype(BF16)
        dk_ref[:, lanes] = _rope(_segment_rms(r[:, klanes], dkg_ref[...]), cos, sin, half).astype(BF16)
    dv_ref[...] = r[:, 2 * bw:].astype(BF16)

    r = seg(0, 2)
    z_ref[...] = r[:, :bw] * jax.nn.sigmoid(r[:, bw:])


def _mix_proj(x2, g, w_mix, sgu_g, sgu_b, w_s, b_s, cq_g, ck_g, cos_c, sin_c, dq_g, dk_g, cos_d, sin_d, *, bw, tm):
    m, d = x2.shape
    groups = w_s.shape[0]
    heads = bw // HEAD_DIM
    dils = tuple(dil for _, dil in DIL_PATTERNS)
    assert groups == heads and w_mix.shape[1] == 10 * bw and all(tm % (dil * 2 * SUBLANES) == 0 for dil in dils)
    bias = jnp.repeat(b_s.T, HEAD_DIM, axis=1)
    vec = pl.BlockSpec((1, bw), lambda i: (0, 0))
    lane_spec = pl.BlockSpec((1, LANES), lambda i: (0, 0))
    tab_spec = pl.BlockSpec((tm, LANES), lambda i: (i, 0))
    out_spec = pl.BlockSpec((tm, bw), lambda i: (i, 0))
    out_bf = jax.ShapeDtypeStruct((m, bw), BF16)
    view_specs = [pl.BlockSpec((tm // dil, dil * bw), lambda i: (i, 0)) for dil in dils] * 3
    view_shapes = [jax.ShapeDtypeStruct((m // dil, dil * bw), BF16) for dil in dils] * 3
    outs = pl.pallas_call(
        functools.partial(_mix_proj_kernel, bw=bw, tm=tm, dils=dils),
        grid=(m // tm,),
        in_specs=[
            pl.BlockSpec((tm, d), lambda i: (i, 0)),
            pl.BlockSpec((1, d), lambda i: (0, 0)),
            _resident((d, 10 * bw), lambda i: (0, 0)),
            vec, vec,
            pl.BlockSpec((groups, SGU_CHUNK, SGU_CHUNK), lambda i: (0, 0, 0)),
            pl.BlockSpec((SGU_CHUNK, bw), lambda i: (0, 0)),
            lane_spec, lane_spec, tab_spec, tab_spec,
            lane_spec, lane_spec, tab_spec, tab_spec,
        ],
        out_specs=[out_spec] * 5 + view_specs,
        out_shape=[jax.ShapeDtypeStruct((m, bw), F32)] + [out_bf] * 4 + view_shapes,
        scratch_shapes=[pltpu.VMEM((heads, tm, HEAD_DIM), F32)],
        compiler_params=_params("parallel"),
        name="mix_proj",
    )(x2, g, w_mix, sgu_g, sgu_b, w_s, bias, cq_g, ck_g, cos_c, sin_c,
      jnp.tile(dq_g, (1, 2)), jnp.tile(dk_g, (1, 2)), cos_d, sin_d)
    n = len(dils)
    z, yb, dq, dk, dv = outs[:5]
    cq, ck, cv = outs[5:5 + n], outs[5 + n:5 + 2 * n], outs[5 + 2 * n:]
    return z, yb, (cq, ck, cv), (dq, dk, dv)


def _gate_proj_kernel(x_ref, g_ref, w_ref, o_ref, h_ref):
    @pl.when(pl.program_id(1) == 0)
    def _():
        h_ref[...] = _rms(x_ref[...], g_ref[...]).astype(BF16)

    o_ref[...] = jax.nn.sigmoid(_dot(h_ref[...], w_ref[...].astype(BF16))).astype(BF16)


def _gate_proj(x2, g, w_in_layers, layer, *, n_mix, tm, tn):
    m, d = x2.shape
    n = w_in_layers.shape[2] - n_mix
    assert n_mix % tn == 0 and n % tn == 0
    first = n_mix // tn
    return pl.pallas_call(
        _gate_proj_kernel,
        grid=(m // tm, n // tn),
        in_specs=[pl.BlockSpec((tm, d), lambda i, j: (i, 0)), pl.BlockSpec((1, d), lambda i, j: (0, 0)),
                  pl.BlockSpec((None, d, tn), lambda i, j: (layer, 0, first + j))],
        out_specs=pl.BlockSpec((tm, tn), lambda i, j: (i, j)),
        out_shape=jax.ShapeDtypeStruct((m, n), BF16),
        scratch_shapes=[pltpu.VMEM((tm, d), BF16)],
        compiler_params=_params("parallel", "arbitrary"),
        name="gate_proj",
    )(x2, g, w_in_layers)


def _conv_kernel(zin_ref, w_ref, b_ref, lg_ref, lb_ref, o_ref, z_ref, *, ts, width, rows):
    ext = ts + CONV_HALO

    @pl.when(pl.program_id(1) == 0)
    def _():
        z_ref[0, 0:CONV_HALO, :] = jnp.zeros((CONV_HALO, z_ref.shape[2]), F32)

    z_ref[0, CONV_HALO:ext, :] = zin_ref[...]
    for s in range(1, SUBLANES):
        z_ref[s, 0:ext - SUBLANES, :] = z_ref[0, s:s + ext - SUBLANES, :]
    ch = z_ref.shape[2]
    for c in range(ts // rows):
        r0 = c * rows
        acc = jnp.broadcast_to(b_ref[...], (rows, ch))
        for k in range(width):
            off = CONV_HALO - (width - 1) + k
            base = r0 + off // SUBLANES * SUBLANES
            acc = acc + w_ref[k:k + 1, :] * z_ref[off % SUBLANES, base:base + rows, :]
        y = _layernorm(acc, lg_ref[...], lb_ref[...])
        o_ref[r0:r0 + rows, :] = (y * jax.nn.sigmoid(y)).astype(o_ref.dtype)
    z_ref[0, 0:CONV_HALO, :] = z_ref[0, ts:ext, :]


def _conv_mixer(z, conv_w, conv_b, ln_g, ln_b, *, batch, seq, ts):
    m, bw = z.shape
    width = conv_w.shape[0]
    assert width - 1 <= CONV_HALO
    ns = seq // ts
    vec = pl.BlockSpec((1, bw), lambda b, s: (0, 0))
    tile = pl.BlockSpec((ts, bw), lambda b, s: (b * ns + s, 0))
    return pl.pallas_call(
        functools.partial(_conv_kernel, ts=ts, width=width, rows=64),
        grid=(batch, ns),
        in_specs=[tile, pl.BlockSpec((width, bw), lambda b, s: (0, 0)), vec, vec, vec],
        out_specs=tile,
        out_shape=jax.ShapeDtypeStruct((m, bw), BF16),
        scratch_shapes=[pltpu.VMEM((SUBLANES, ts + CONV_HALO, bw), F32)],
        compiler_params=_params("arbitrary", "arbitrary"),
        name="conv_mixer",
    )(z, conv_w, conv_b, ln_g, ln_b)


def _dil_band_kernel(*refs, lt, heads, others):
    bw = heads * HEAD_DIM
    if others:
        q_ref, k_ref, v_ref = refs[:3]
        other_refs = refs[3:3 + 2 * len(others)]
        y_ref = refs[3 + 2 * len(others)]
        stage_refs = refs[4 + 2 * len(others):]
        for dil, o_ref, lse_ref, o_st, lse_st in zip(others, other_refs[0::2], other_refs[1::2],
                                                     stage_refs[0::2], stage_refs[1::2]):
            for res in range(dil):
                for hd in range(heads):
                    cols = slice(res * bw + hd * HEAD_DIM, res * bw + (hd + 1) * HEAD_DIM)
                    o_st[hd, pl.ds(res, lt // dil, stride=dil), :] = o_ref[:, cols]
                    lse_st[hd, pl.ds(res, lt // dil, stride=dil), :] = lse_ref[:, cols]
    else:
        q_ref, k_ref, v_ref, o_ref, lse_ref = refs
    i = pl.program_id(2)
    blk = DIL_BLOCK
    scale = np.float32(HEAD_DIM ** -0.5)
    ones = jnp.ones((2 * blk, HEAD_DIM), BF16)
    row = lax.broadcasted_iota(jnp.int32, (blk, 2 * blk), 0)
    col = lax.broadcasted_iota(jnp.int32, (blk, 2 * blk), 1)
    for jb in range(lt // blk):
        rws = slice(jb * blk, (jb + 1) * blk)
        l0 = i * lt + jb * blk
        ks = pl.multiple_of(jnp.maximum(l0 - blk, 0), blk)
        dist = (l0 - ks) + row - col
        valid = (dist >= 0) & (dist <= DIL_SPAN)
        for hd in range(heads):
            lanes = slice(hd * HEAD_DIM, (hd + 1) * HEAD_DIM)
            s = _dot_nt(q_ref[rws, lanes], k_ref[pl.ds(ks, 2 * blk), lanes]) * scale
            s = jnp.where(valid, s, NEG_INF)
            mx = jnp.max(s, axis=-1, keepdims=True)
            p = jnp.exp(s - mx).astype(BF16)
            r = _dot(p, jnp.concatenate([v_ref[pl.ds(ks, 2 * blk), lanes], ones], axis=1))
            den = r[:, HEAD_DIM:]
            o = r[:, :HEAD_DIM] / den
            lse = mx + jnp.log(den)
            if others:
                o_all = [o] + [st[hd, rws, :] for st in stage_refs[0::2]]
                lse_all = [lse] + [st[hd, rws, :] for st in stage_refs[1::2]]
                top = functools.reduce(jnp.maximum, lse_all)
                wts = [jnp.exp(t - top) for t in lse_all]
                num = wts[0] * o_all[0]
                tot = wts[0]
                for wg, og in zip(wts[1:], o_all[1:]):
                    num = num + wg * og
                    tot = tot + wg
                y_ref[rws, lanes] = (num / tot).astype(BF16)
            else:
                o_ref[rws, lanes] = o
                lse_ref[rws, lanes] = lse


def _dil_band(q, k, v, others, *, batch, seq, dil, lt):
    rows, cols = q.shape
    bw = cols // dil
    heads = bw // HEAD_DIM
    ln = seq // dil
    lt = min(lt, ln)
    nl = ln // lt
    assert ln % lt == 0 and lt % DIL_BLOCK == 0 and ln >= 2 * DIL_BLOCK
    tile = pl.BlockSpec((lt, bw), lambda b, r, i: (b * nl + i, r))
    whole = pl.BlockSpec((ln, bw), lambda b, r, i: (b, r))
    if others:
        assert dil == 1 and all(lt % (od * SUBLANES) == 0 for od, _, _ in others)
        extra = [a for _, o, lse in others for a in (o, lse)]
        extra_specs = [pl.BlockSpec((lt // od, od * bw), lambda b, r, i: (b * nl + i, 0))
                       for od, _, _ in others for _ in range(2)]
        out_specs, out_shape = tile, jax.ShapeDtypeStruct((rows, cols), BF16)
        scratch = [pltpu.VMEM((heads, lt, HEAD_DIM), F32)] * len(extra)
    else:
        extra, extra_specs, scratch = [], [], []
        out_specs = [tile, tile]
        out_shape = [jax.ShapeDtypeStruct((rows, cols), F32)] * 2
    return pl.pallas_call(
        functools.partial(_dil_band_kernel, lt=lt, heads=heads, others=tuple(od for od, _, _ in others)),
        grid=(batch, dil, nl),
        in_specs=[tile, whole, whole] + extra_specs,
        out_specs=out_specs,
        out_shape=out_shape,
        scratch_shapes=scratch,
        compiler_params=_params("parallel", "parallel", "arbitrary"),
        name=f"dil_band_{dil}",
    )(q, k, v, *extra)


def _dil_attn(cq, ck, cv, *, batch, seq):
    dils = [d for _, d in DIL_PATTERNS]
    assert all(w // d == DIL_SPAN for w, d in DIL_PATTERNS) and dils[0] == 1
    others = []
    for idx in range(1, len(dils)):
        o, lse = _dil_band(cq[idx], ck[idx], cv[idx], [], batch=batch, seq=seq, dil=dils[idx], lt=512)
        others.append((dils[idx], o, lse))
    return _dil_band(cq[0], ck[0], cv[0], others, batch=batch, seq=seq, dil=1, lt=512)


def _diff_attn_kernel(q_ref, k_ref, v_ref, lq1_ref, lk1_ref, lq2_ref, lk2_ref, sub_ref, o_ref,
                      s_ref, m_ref, l_ref, acc_ref, *, tq, lam_init):
    i = pl.program_id(2)
    q = q_ref[...]
    lane = lax.broadcasted_iota(jnp.int32, q.shape, 1)
    zero = jnp.zeros_like(q)
    qq = jnp.concatenate([jnp.where(lane < DIFF_QK_DIM, q, zero), jnp.where(lane < DIFF_QK_DIM, zero, q)], axis=0)
    tiles = tq // LANES

    def lane_fold(x, op):
        r = x[:, 0:LANES]
        for t in range(1, tiles):
            r = op(r, x[:, t * LANES:(t + 1) * LANES])
        return r

    m_ref[...] = jnp.full(m_ref.shape, NEG_INF, F32)

    def scores(j, masked):
        k0 = pl.multiple_of(j * tq, tq)
        s = _dot_nt(qq, k_ref[pl.ds(k0, tq), :]) * np.float32(math.log2(math.e))
        if masked:
            row = lax.broadcasted_iota(jnp.int32, s.shape, 0)
            col = lax.broadcasted_iota(jnp.int32, s.shape, 1)
            row = jnp.where(row >= tq, row - tq, row)
            s = jnp.where(col <= row, s, NEG_INF)
        s_ref[j] = s
        m_ref[...] = jnp.maximum(m_ref[...], lane_fold(s, jnp.maximum))

    def accumulate(j):
        k0 = pl.multiple_of(j * tq, tq)
        s = s_ref[j]
        mx = m_ref[...]
        ps = [jnp.exp2(s[:, t * LANES:(t + 1) * LANES] - mx) for t in range(tiles)]
        part = ps[0]
        for t in range(1, tiles):
            part = part + ps[t]
        l_ref[...] += part
        p = jnp.concatenate(ps, axis=1).astype(BF16)
        acc_ref[...] += _dot(p, v_ref[pl.ds(k0, tq), :])

    def in_pairs(fn, count):
        def body(jp, carry):
            fn(2 * jp)
            fn(2 * jp + 1)
            return carry

        lax.fori_loop(0, count // 2, body, 0)

        @pl.when(count % 2 == 1)
        def _():
            fn(count - 1)

    in_pairs(lambda j: scores(j, False), i)
    scores(i, True)
    m_ref[...] = jnp.broadcast_to(jnp.max(m_ref[...], axis=-1, keepdims=True), m_ref.shape)

    l_ref[...] = jnp.zeros(l_ref.shape, F32)
    acc_ref[...] = jnp.zeros(acc_ref.shape, F32)
    in_pairs(accumulate, i + 1)

    o = acc_ref[...] / jnp.sum(l_ref[...], axis=-1, keepdims=True)
    lam = (jnp.exp(jnp.sum(lq1_ref[...] * lk1_ref[...], axis=-1, keepdims=True))
           - jnp.exp(jnp.sum(lq2_ref[...] * lk2_ref[...], axis=-1, keepdims=True)) + np.float32(lam_init))
    od = o[0:tq, :] - lam * o[tq:2 * tq, :]
    o_ref[...] = (_rms(od, sub_ref[...]) * np.float32(1.0 - lam_init)).astype(o_ref.dtype)


def _diff_attn(q, k, v, lq1, lk1, lq2, lk2, subln, *, batch, seq, tq, lam_init):
    m, bw = q.shape
    heads = bw // HEAD_DIM
    nq = seq // tq
    q_spec = pl.BlockSpec((tq, HEAD_DIM), lambda b, h, i: (b * nq + i, h))
    kv_spec = pl.BlockSpec((seq, HEAD_DIM), lambda b, h, i: (b, h))
    lam_spec = pl.BlockSpec((1, DIFF_QK_DIM), lambda b, h, i: (0, 0))
    stat = pltpu.VMEM((2 * tq, LANES), F32)
    return pl.pallas_call(
        functools.partial(_diff_attn_kernel, tq=tq, lam_init=lam_init),
        grid=(batch, heads, nq),
        in_specs=[q_spec, kv_spec, kv_spec, lam_spec, lam_spec, lam_spec, lam_spec,
                  pl.BlockSpec((1, HEAD_DIM), lambda b, h, i: (0, 0))],
        out_specs=q_spec,
        out_shape=jax.ShapeDtypeStruct((m, bw), BF16),
        scratch_shapes=[pltpu.VMEM((nq, 2 * tq, tq), F32), stat, stat, stat],
        compiler_params=_params("parallel", "parallel", "arbitrary"),
        name="diff_attn",
    )(q, k, v, lq1, lk1, lq2, lk2, subln)


def _merge_kernel(ya_ref, yb_ref, yc_ref, yd_ref, ga_ref, gb_ref, gc_ref, gd_ref, wb_ref, o_ref, *, tn):
    branches = ((ya_ref, ga_ref), (yb_ref, gb_ref), (yc_ref, gc_ref), (yd_ref, gd_ref))
    for c in range(o_ref.shape[1] // tn):
        cols = slice(c * tn, (c + 1) * tn)
        acc = None
        for b, (y_ref, gate_ref) in enumerate(branches):
            term = gate_ref[:, cols].astype(F32) * _dot(y_ref[...], wb_ref[b, :, cols])
            acc = term if acc is None else acc + term
        o_ref[:, cols] = acc.astype(BF16)


def _merge(ys, gates, w_branch, *, tm, tn):
    m, bw = ys[0].shape
    nb, _, d = w_branch.shape
    assert nb == len(ys) == 4
    y_spec = pl.BlockSpec((tm, bw), lambda r: (r, 0))
    gate_specs = [pl.BlockSpec((tm, d), lambda r, b=b: (r, b)) for b in range(nb)]
    return pl.pallas_call(
        functools.partial(_merge_kernel, tn=tn),
        grid=(m // tm,),
        in_specs=[y_spec] * nb + gate_specs + [_resident((nb, bw, d), lambda r: (0, 0, 0))],
        out_specs=pl.BlockSpec((tm, d), lambda r: (r, 0)),
        out_shape=jax.ShapeDtypeStruct((m, d), BF16),
        compiler_params=_params("parallel"),
        name="merge_branches",
    )(*ys, gates, gates, gates, gates, w_branch)


def kernel(x, positions, ffn1_norm, ffn1_w13, ffn1_w2, mix_norm, w_in, conv_w, conv_b, conv_ln_g, conv_ln_b, sgu_ln_g, sgu_ln_b, sgu_w, sgu_b, dil_q_norm, dil_k_norm, diff_q_norm, diff_k_norm, diff_lq1, diff_lk1, diff_lq2, diff_lk2, diff_subln, w_branch, w_out, ffn2_norm, ffn2_w13, ffn2_w2):
    batch, seq, d = x.shape
    depth = w_in.shape[0]
    m = batch * seq
    bw = d // 4
    n_mix = 10 * bw
    assert bw % HEAD_DIM == 0 and w_in.shape[2] == n_mix + 4 * d

    tm = min(1024, m)
    dff = ffn1_w2.shape[1]
    tf = 512 if dff % 512 == 0 else 256
    tn = min(512, d)

    def row(v, l):
        return v[l].reshape(1, -1)

    def ffn(x2, g, w13_layers, w2, l):
        act = _ffn_up(x2, g, w13_layers, l, tm=tm, tf=tf)
        return _residual_matmul(act, w2.astype(BF16), x2, tm=tm, tn=tn, name="ffn_down")

    x2 = x.reshape(m, d)
    cos_c, sin_c, cos_d, sin_d = _rope_tables(positions, ts=min(512, m))

    for l in range(depth):
        lam_init = 0.8 - 0.6 * math.exp(-0.3 * l)
        x2 = ffn(x2, row(ffn1_norm, l), ffn1_w13, ffn1_w2[l], l)

        z, yb, (cq, ck, cv), (dq, dk, dv) = _mix_proj(
            x2, row(mix_norm, l), w_in[l, :, :n_mix].astype(BF16), row(sgu_ln_g, l), row(sgu_ln_b, l),
            sgu_w[l], sgu_b[l], row(dil_q_norm, l), row(dil_k_norm, l), cos_c, sin_c,
            row(diff_q_norm, l), row(diff_k_norm, l), cos_d, sin_d, bw=bw, tm=min(512, m))
        gates = _gate_proj(x2, row(mix_norm, l), w_in, l, n_mix=n_mix, tm=tm, tn=math.gcd(n_mix, 1024))

        ya = _conv_mixer(z, conv_w[l], row(conv_b, l), row(conv_ln_g, l), row(conv_ln_b, l),
                         batch=batch, seq=seq, ts=min(512, seq))
        yc = _dil_attn(cq, ck, cv, batch=batch, seq=seq)
        yd = _diff_attn(dq, dk, dv, row(diff_lq1, l), row(diff_lk1, l), row(diff_lq2, l), row(diff_lk2, l),
                        row(diff_subln, l), batch=batch, seq=seq, tq=min(512, seq), lam_init=lam_init)

        merged = _merge((ya, yb, yc, yd), gates, w_branch[l].astype(BF16), tm=min(512, m), tn=tn)
        x2 = _residual_matmul(merged, w_out[l].astype(BF16), x2, tm=min(512, m), tn=d, name="out_proj")

        x2 = ffn(x2, row(ffn2_norm, l), ffn2_w13, ffn2_w2[l], l)
    return x2.reshape(batch, seq, d)
```

```python
import functools
import math

import jax
import jax.numpy as jnp
import numpy as np
from jax import lax
from jax.experimental import pallas as pl
from jax.experimental.pallas import tpu as pltpu

HEAD_DIM = 128
LANES = 128
SUBLANES = 8
DIFF_QK_DIM = HEAD_DIM // 2
SGU_CHUNK = 128
DIL_PATTERNS = ((128, 1), (512, 4), (2048, 16))
DIL_SPAN = 128
DIL_BLOCK = 128
ROPE_THETA = 500000.0
ROPE_FRACTION = 4
EPS = 1e-6
NEG_INF = -1e30
CONV_HALO = 32

VMEM_LIMIT_BYTES = 56 * 1024 * 1024

BF16 = jnp.bfloat16
F32 = jnp.float32


def _params(*sem):
    return pltpu.CompilerParams(dimension_semantics=sem, vmem_limit_bytes=VMEM_LIMIT_BYTES)


def _resident(shape, index_map):
    return pl.BlockSpec(shape, index_map, pipeline_mode=pl.Buffered(1))


def _dot(a, b):
    return jnp.dot(a, b, preferred_element_type=F32)


def _dot_nt(a, b):
    return lax.dot_general(a, b, (((1,), (1,)), ((), ())), preferred_element_type=F32)


def _rms(x, g):
    return x * lax.rsqrt(jnp.mean(x * x, axis=-1, keepdims=True) + EPS) * g


def _layernorm(x, g, b):
    mu = jnp.mean(x, axis=-1, keepdims=True)
    xc = x - mu
    return xc * lax.rsqrt(jnp.mean(xc * xc, axis=-1, keepdims=True) + EPS) * g + b


def _gelu(x):
    return 0.5 * x * (1.0 + lax.erf(x * np.float32(math.sqrt(0.5))))


def _cast_kernel(w_ref, o_ref):
    o_ref[...] = w_ref[...].astype(BF16)


def _cast_layers(w_layers, ncols, *, tr):
    depth, rows, _ = w_layers.shape
    return pl.pallas_call(
        _cast_kernel,
        grid=(depth, rows // tr),
        in_specs=[pl.BlockSpec((None, tr, ncols), lambda l, i: (l, i, 0))],
        out_specs=pl.BlockSpec((None, tr, ncols), lambda l, i: (l, i, 0)),
        out_shape=jax.ShapeDtypeStruct((depth, rows, ncols), BF16),
        compiler_params=_params("parallel", "parallel"),
        name="cast_weights",
    )(w_layers)


def _ffn_up_kernel(x_ref, g_ref, wg_ref, wu_ref, a_ref, h_ref):
    @pl.when(pl.program_id(1) == 0)
    def _():
        h_ref[...] = _rms(x_ref[...], g_ref[...]).astype(BF16)

    h = h_ref[...]
    gate = _dot(h, wg_ref[...].astype(BF16))
    up = _dot(h, wu_ref[...].astype(BF16))
    a_ref[...] = (0.5 * gate * jax.nn.sigmoid(gate) * up).astype(BF16)


def _ffn_up(x2, g, w13_layers, layer, *, tm, tf):
    m, d = x2.shape
    dff = w13_layers.shape[2] // 2
    nf = dff // tf
    return pl.pallas_call(
        _ffn_up_kernel,
        grid=(m // tm, nf),
        in_specs=[
            pl.BlockSpec((tm, d), lambda i, j: (i, 0)),
            pl.BlockSpec((1, d), lambda i, j: (0, 0)),
            pl.BlockSpec((None, d, tf), lambda i, j: (layer, 0, j)),
            pl.BlockSpec((None, d, tf), lambda i, j: (layer, 0, j + nf)),
        ],
        out_specs=pl.BlockSpec((tm, tf), lambda i, j: (i, j)),
        out_shape=jax.ShapeDtypeStruct((m, dff), BF16),
        scratch_shapes=[pltpu.VMEM((tm, d), BF16)],
        compiler_params=_params("parallel", "arbitrary"),
        name="ffn_up",
    )(x2, g, w13_layers, w13_layers)


def _residual_matmul_kernel(a_ref, w_ref, x_ref, o_ref):
    o_ref[...] = x_ref[...] + _dot(a_ref[...], w_ref[...])


def _residual_matmul(a, w_layers, layer, x2, *, tm, tn, name):
    m, kdim = a.shape
    d = w_layers.shape[2]
    tile = pl.BlockSpec((tm, tn), lambda i, j: (i, j))
    w_spec = _resident if tn == d else pl.BlockSpec
    return pl.pallas_call(
        _residual_matmul_kernel,
        grid=(m // tm, d // tn),
        in_specs=[pl.BlockSpec((tm, kdim), lambda i, j: (i, 0)),
                  w_spec((None, kdim, tn), lambda i, j: (layer, 0, j)), tile],
        out_specs=tile,
        out_shape=jax.ShapeDtypeStruct((m, d), F32),
        compiler_params=_params("parallel", "arbitrary"),
        name=name,
    )(a, w_layers, x2)


def _rope_table_kernel(pos_ref, inv_c_ref, sgn_c_ref, inv_d_ref, sgn_d_ref,
                       cos_c_ref, sin_c_ref, cos_d_ref, sin_d_ref):
    p = pos_ref[...].astype(F32)
    ang_c = p * inv_c_ref[...]
    cos_c_ref[...] = jnp.cos(ang_c)
    sin_c_ref[...] = jnp.sin(ang_c) * sgn_c_ref[...]
    ang_d = p * inv_d_ref[...]
    cos_d_ref[...] = jnp.cos(ang_d)
    sin_d_ref[...] = jnp.sin(ang_d) * sgn_d_ref[...]


def _rope_lane_consts(width):
    rot = width // ROPE_FRACTION
    half = rot // 2
    inv_freq = 1.0 / (ROPE_THETA ** (jnp.arange(half, dtype=F32) * 2.0 / rot))
    lane = np.arange(LANES) % width
    in_rot = lane < rot
    inv = jnp.where(in_rot, inv_freq[lane % half], 0.0).astype(F32)
    sgn = np.where(lane < half, -1.0, np.where(in_rot, 1.0, 0.0)).astype(np.float32)
    return inv.reshape(1, LANES), jnp.asarray(sgn).reshape(1, LANES)


def _rope_tables(positions, *, ts):
    m = positions.size
    inv_c, sgn_c = _rope_lane_consts(HEAD_DIM)
    inv_d, sgn_d = _rope_lane_consts(DIFF_QK_DIM)
    lane_spec = pl.BlockSpec((1, LANES), lambda i: (0, 0))
    tab_spec = pl.BlockSpec((ts, LANES), lambda i: (i, 0))
    tab = jax.ShapeDtypeStruct((m, LANES), F32)
    return pl.pallas_call(
        _rope_table_kernel,
        grid=(m // ts,),
        in_specs=[pl.BlockSpec((ts, 1), lambda i: (i, 0)), lane_spec, lane_spec, lane_spec, lane_spec],
        out_specs=[tab_spec] * 4,
        out_shape=[tab] * 4,
        compiler_params=_params("parallel"),
        name="rope_tables",
    )(positions.reshape(m, 1), inv_c, sgn_c, inv_d, sgn_d)


def _rope(x, cos, sin, half):
    lane = lax.broadcasted_iota(jnp.int32, x.shape, 1)
    lower = (lane % (2 * half)) < half
    partner = jnp.where(lower, pltpu.roll(x, LANES - half, 1), pltpu.roll(x, half, 1))
    return x * cos + partner * sin


def _segment_rms(x, g):
    lane = lax.broadcasted_iota(jnp.int32, x.shape, 1)
    lower = lane < DIFF_QK_DIM
    sq = x * x
    lo = jnp.sum(jnp.where(lower, sq, 0.0), axis=-1, keepdims=True)
    hi = jnp.sum(jnp.where(lower, 0.0, sq), axis=-1, keepdims=True)
    ms = jnp.where(lower, lo, hi) * np.float32(1.0 / DIFF_QK_DIM)
    return x * lax.rsqrt(ms + EPS) * g


def _mix_proj_kernel(x_ref, g_ref, w_ref, sg_ref, sb_ref, ws_ref, bias_ref, cqg_ref, ckg_ref, cos_c_ref, sin_c_ref,
                     dqg_ref, dkg_ref, cos_d_ref, sin_d_ref,
                     z_ref, yb_ref, dq_ref, dk_ref, dv_ref, *c_refs_and_scratch, bw, tm, dils):
    c_refs, stage_ref = c_refs_and_scratch[:-1], c_refs_and_scratch[-1]
    h = _rms(x_ref[...], g_ref[...]).astype(BF16)
    heads = bw // HEAD_DIM

    def seg(c0, n):
        return _dot(h, w_ref[:, c0 * bw:(c0 + n) * bw])

    r = seg(2, 2)
    u = _gelu(r[:, :bw])
    v = _layernorm(_gelu(r[:, bw:]), sg_ref[...], sb_ref[...]).astype(BF16)
    row = lax.broadcasted_iota(jnp.int32, (SGU_CHUNK, SGU_CHUNK), 0)
    col = lax.broadcasted_iota(jnp.int32, (SGU_CHUNK, SGU_CHUNK), 1)
    for g in range(heads):
        lanes = slice(g * HEAD_DIM, (g + 1) * HEAD_DIM)
        w = jnp.where(row >= col, ws_ref[g], 0.0).astype(BF16)
        for c in range(tm // SGU_CHUNK):
            rws = slice(c * SGU_CHUNK, (c + 1) * SGU_CHUNK)
            mixed = _dot(w, v[rws, lanes]) + bias_ref[:, lanes]
            yb_ref[rws, lanes] = (u[rws, lanes] * mixed).astype(BF16)

    r = seg(4, 3)
    cos, sin = cos_c_ref[...], sin_c_ref[...]
    half = HEAD_DIM // ROPE_FRACTION // 2
    for part in range(3):
        outs = c_refs[part * len(dils):(part + 1) * len(dils)]
        for hd in range(heads):
            lanes = slice(hd * HEAD_DIM, (hd + 1) * HEAD_DIM)
            t = r[:, part * bw + hd * HEAD_DIM:part * bw + (hd + 1) * HEAD_DIM]
            if part < 2:
                t = _rope(_rms(t, (cqg_ref, ckg_ref)[part][...]), cos, sin, half)
            stage_ref[hd] = t
            for dil, o_ref in zip(dils, outs):
                if dil == 1:
                    o_ref[:, lanes] = t.astype(BF16)
        for dil, o_ref in zip(dils, outs):
            if dil == 1:
                continue
            for res in range(dil):
                for hd in range(heads):
                    o_ref[:, res * bw + hd * HEAD_DIM:res * bw + (hd + 1) * HEAD_DIM] = (
                        stage_ref[hd, pl.ds(res, tm // dil, stride=dil), :].astype(BF16))

    r = seg(7, 3)
    cos, sin = cos_d_ref[...], sin_d_ref[...]
    half = DIFF_QK_DIM // ROPE_FRACTION // 2
    scale = np.float32(DIFF_QK_DIM ** -0.5)
    for hd in range(heads):
        lanes = slice(hd * HEAD_DIM, (hd + 1) * HEAD_DIM)
        klanes = slice(bw + hd * HEAD_DIM, bw + (hd + 1) * HEAD_DIM)
        q = _rope(_segment_rms(r[:, lanes], dqg_ref[...]), cos, sin, half)
        dq_ref[:, lanes] = (q * scale).astype(BF16)
        dk_ref[:, lanes] = _rope(_segment_rms(r[:, klanes], dkg_ref[...]), cos, sin, half).astype(BF16)
    dv_ref[...] = r[:, 2 * bw:].astype(BF16)

    r = seg(0, 2)
    z_ref[...] = r[:, :bw] * jax.nn.sigmoid(r[:, bw:])


def _mix_proj(x2, g, w_mix_layers, layer, sgu_g, sgu_b, w_s, b_s, cq_g, ck_g, cos_c, sin_c, dq_g, dk_g, cos_d, sin_d,
              *, bw, tm):
    m, d = x2.shape
    groups = w_s.shape[0]
    heads = bw // HEAD_DIM
    dils = tuple(dil for _, dil in DIL_PATTERNS)
    assert groups == heads and w_mix_layers.shape[2] == 10 * bw
    assert all(tm % (dil * 2 * SUBLANES) == 0 for dil in dils)
    bias = jnp.repeat(b_s.T, HEAD_DIM, axis=1)
    vec = pl.BlockSpec((1, bw), lambda i: (0, 0))
    lane_spec = pl.BlockSpec((1, LANES), lambda i: (0, 0))
    tab_spec = pl.BlockSpec((tm, LANES), lambda i: (i, 0))
    out_spec = pl.BlockSpec((tm, bw), lambda i: (i, 0))
    out_bf = jax.ShapeDtypeStruct((m, bw), BF16)
    view_specs = [pl.BlockSpec((tm // dil, dil * bw), lambda i: (i, 0)) for dil in dils] * 3
    view_shapes = [jax.ShapeDtypeStruct((m // dil, dil * bw), BF16) for dil in dils] * 3
    outs = pl.pallas_call(
        functools.partial(_mix_proj_kernel, bw=bw, tm=tm, dils=dils),
        grid=(m // tm,),
        in_specs=[
            pl.BlockSpec((tm, d), lambda i: (i, 0)),
            pl.BlockSpec((1, d), lambda i: (0, 0)),
            _resident((None, d, 10 * bw), lambda i: (layer, 0, 0)),
            vec, vec,
            pl.BlockSpec((groups, SGU_CHUNK, SGU_CHUNK), lambda i: (0, 0, 0)),
            pl.BlockSpec((SGU_CHUNK, bw), lambda i: (0, 0)),
            lane_spec, lane_spec, tab_spec, tab_spec,
            lane_spec, lane_spec, tab_spec, tab_spec,
        ],
        out_specs=[out_spec] * 5 + view_specs,
        out_shape=[jax.ShapeDtypeStruct((m, bw), F32)] + [out_bf] * 4 + view_shapes,
        scratch_shapes=[pltpu.VMEM((heads, tm, HEAD_DIM), F32)],
        compiler_params=_params("parallel"),
        name="mix_proj",
    )(x2, g, w_mix_layers, sgu_g, sgu_b, w_s, bias, cq_g, ck_g, cos_c, sin_c,
      jnp.tile(dq_g, (1, 2)), jnp.tile(dk_g, (1, 2)), cos_d, sin_d)
    n = len(dils)
    z, yb, dq, dk, dv = outs[:5]
    cq, ck, cv = outs[5:5 + n], outs[5 + n:5 + 2 * n], outs[5 + 2 * n:]
    return z, yb, (cq, ck, cv), (dq, dk, dv)


def _gate_proj_kernel(x_ref, g_ref, w_ref, o_ref, h_ref):
    @pl.when(pl.program_id(1) == 0)
    def _():
        h_ref[...] = _rms(x_ref[...], g_ref[...]).astype(BF16)

    o_ref[...] = jax.nn.sigmoid(_dot(h_ref[...], w_ref[...].astype(BF16))).astype(BF16)


def _gate_proj(x2, g, w_in_layers, layer, *, n_mix, tm, tn):
    m, d = x2.shape
    n = w_in_layers.shape[2] - n_mix
    assert n_mix % tn == 0 and n % tn == 0
    first = n_mix // tn
    return pl.pallas_call(
        _gate_proj_kernel,
        grid=(m // tm, n // tn),
        in_specs=[pl.BlockSpec((tm, d), lambda i, j: (i, 0)), pl.BlockSpec((1, d), lambda i, j: (0, 0)),
                  pl.BlockSpec((None, d, tn), lambda i, j: (layer, 0, first + j))],
        out_specs=pl.BlockSpec((tm, tn), lambda i, j: (i, j)),
        out_shape=jax.ShapeDtypeStruct((m, n), BF16),
        scratch_shapes=[pltpu.VMEM((tm, d), BF16)],
        compiler_params=_params("parallel", "arbitrary"),
        name="gate_proj",
    )(x2, g, w_in_layers)


def _conv_kernel(zin_ref, w_ref, b_ref, lg_ref, lb_ref, o_ref, z_ref, *, ts, width, rows):
    ext = ts + CONV_HALO

    @pl.when(pl.program_id(1) == 0)
    def _():
        z_ref[0, 0:CONV_HALO, :] = jnp.zeros((CONV_HALO, z_ref.shape[2]), F32)

    z_ref[0, CONV_HALO:ext, :] = zin_ref[...]
    for s in range(1, SUBLANES):
        z_ref[s, 0:ext - SUBLANES, :] = z_ref[0, s:s + ext - SUBLANES, :]
    ch = z_ref.shape[2]
    for c in range(ts // rows):
        r0 = c * rows
        acc = jnp.broadcast_to(b_ref[...], (rows, ch))
        for k in range(width):
            off = CONV_HALO - (width - 1) + k
            base = r0 + off // SUBLANES * SUBLANES
            acc = acc + w_ref[k:k + 1, :] * z_ref[off % SUBLANES, base:base + rows, :]
        y = _layernorm(acc, lg_ref[...], lb_ref[...])
        o_ref[r0:r0 + rows, :] = (y * jax.nn.sigmoid(y)).astype(o_ref.dtype)
    z_ref[0, 0:CONV_HALO, :] = z_ref[0, ts:ext, :]


def _conv_mixer(z, conv_w, conv_b, ln_g, ln_b, *, batch, seq, ts):
    m, bw = z.shape
    width = conv_w.shape[0]
    assert width - 1 <= CONV_HALO
    ns = seq // ts
    vec = pl.BlockSpec((1, bw), lambda b, s: (0, 0))
    tile = pl.BlockSpec((ts, bw), lambda b, s: (b * ns + s, 0))
    return pl.pallas_call(
        functools.partial(_conv_kernel, ts=ts, width=width, rows=64),
        grid=(batch, ns),
        in_specs=[tile, pl.BlockSpec((width, bw), lambda b, s: (0, 0)), vec, vec, vec],
        out_specs=tile,
        out_shape=jax.ShapeDtypeStruct((m, bw), BF16),
        scratch_shapes=[pltpu.VMEM((SUBLANES, ts + CONV_HALO, bw), F32)],
        compiler_params=_params("arbitrary", "arbitrary"),
        name="conv_mixer",
    )(z, conv_w, conv_b, ln_g, ln_b)


def _dil_band_kernel(*refs, lt, heads, others):
    bw = heads * HEAD_DIM
    if others:
        q_ref, k_ref, v_ref = refs[:3]
        other_refs = refs[3:3 + 2 * len(others)]
        y_ref = refs[3 + 2 * len(others)]
        stage_refs = refs[4 + 2 * len(others):]
        for dil, o_ref, lse_ref, o_st, lse_st in zip(others, other_refs[0::2], other_refs[1::2],
                                                     stage_refs[0::2], stage_refs[1::2]):
            for res in range(dil):
                for hd in range(heads):
                    cols = slice(res * bw + hd * HEAD_DIM, res * bw + (hd + 1) * HEAD_DIM)
                    o_st[hd, pl.ds(res, lt // dil, stride=dil), :] = o_ref[:, cols]
                    lse_st[hd, pl.ds(res, lt // dil, stride=dil), :] = lse_ref[:, cols]
    else:
        q_ref, k_ref, v_ref, o_ref, lse_ref = refs
    i = pl.program_id(2)
    blk = DIL_BLOCK
    scale = np.float32(HEAD_DIM ** -0.5)
    ones = jnp.ones((2 * blk, HEAD_DIM), BF16)
    row = lax.broadcasted_iota(jnp.int32, (blk, 2 * blk), 0)
    col = lax.broadcasted_iota(jnp.int32, (blk, 2 * blk), 1)
    for jb in range(lt // blk):
        rws = slice(jb * blk, (jb + 1) * blk)
        l0 = i * lt + jb * blk
        ks = pl.multiple_of(jnp.maximum(l0 - blk, 0), blk)
        dist = (l0 - ks) + row - col
        valid = (dist >= 0) & (dist <= DIL_SPAN)
        for hd in range(heads):
            lanes = slice(hd * HEAD_DIM, (hd + 1) * HEAD_DIM)
            s = _dot_nt(q_ref[rws, lanes], k_ref[pl.ds(ks, 2 * blk), lanes]) * scale
            s = jnp.where(valid, s, NEG_INF)
            mx = jnp.max(s, axis=-1, keepdims=True)
            p = jnp.exp(s - mx).astype(BF16)
            r = _dot(p, jnp.concatenate([v_ref[pl.ds(ks, 2 * blk), lanes], ones], axis=1))
            den = r[:, HEAD_DIM:]
            o = r[:, :HEAD_DIM] / den
            lse = mx + jnp.log(den)
            if others:
                o_all = [o] + [st[hd, rws, :] for st in stage_refs[0::2]]
                lse_all = [lse] + [st[hd, rws, :] for st in stage_refs[1::2]]
                top = functools.reduce(jnp.maximum, lse_all)
                wts = [jnp.exp(t - top) for t in lse_all]
                num = wts[0] * o_all[0]
                tot = wts[0]
                for wg, og in zip(wts[1:], o_all[1:]):
                    num = num + wg * og
                    tot = tot + wg
                y_ref[rws, lanes] = (num / tot).astype(BF16)
            else:
                o_ref[rws, lanes] = o
                lse_ref[rws, lanes] = lse


def _dil_band(q, k, v, others, *, batch, seq, dil, lt, classes=1):
    rows, cols = q.shape
    bw = cols // dil * classes
    heads = bw // HEAD_DIM
    ln = seq // dil
    lt = min(lt, ln)
    nl = ln // lt
    assert ln % lt == 0 and lt % DIL_BLOCK == 0 and ln >= 2 * DIL_BLOCK and dil % classes == 0
    tile = pl.BlockSpec((lt, bw), lambda b, r, i: (b * nl + i, r))
    whole = pl.BlockSpec((ln, bw), lambda b, r, i: (b, r))
    if others:
        assert dil == 1 and all(lt % (od * SUBLANES) == 0 for od, _, _ in others)
        extra = [a for _, o, lse in others for a in (o, lse)]
        extra_specs = [pl.BlockSpec((lt // od, od * bw), lambda b, r, i: (b * nl + i, 0))
                       for od, _, _ in others for _ in range(2)]
        out_specs, out_shape = tile, jax.ShapeDtypeStruct((rows, cols), BF16)
        scratch = [pltpu.VMEM((heads, lt, HEAD_DIM), F32)] * len(extra)
    else:
        extra, extra_specs, scratch = [], [], []
        out_specs = [tile, tile]
        out_shape = [jax.ShapeDtypeStruct((rows, cols), F32)] * 2
    return pl.pallas_call(
        functools.partial(_dil_band_kernel, lt=lt, heads=heads, others=tuple(od for od, _, _ in others)),
        grid=(batch, dil // classes, nl),
        in_specs=[tile, whole, whole] + extra_specs,
        out_specs=out_specs,
        out_shape=out_shape,
        scratch_shapes=scratch,
        compiler_params=_params("parallel", "parallel", "arbitrary"),
        name=f"dil_band_{dil}",
    )(q, k, v, *extra)


def _dil_attn(cq, ck, cv, *, batch, seq):
    dils = [d for _, d in DIL_PATTERNS]
    assert all(w // d == DIL_SPAN for w, d in DIL_PATTERNS) and dils[0] == 1
    others = []
    heads = cq[0].shape[1] // HEAD_DIM
    for idx in range(1, len(dils)):
        dil = dils[idx]
        lt = min(seq // dil, 8 * DIL_BLOCK)
        classes = math.gcd(dil, max(1, 32 // (lt // DIL_BLOCK * heads)))
        o, lse = _dil_band(cq[idx], ck[idx], cv[idx], [], batch=batch, seq=seq, dil=dil, lt=lt, classes=classes)
        others.append((dil, o, lse))
    return _dil_band(cq[0], ck[0], cv[0], others, batch=batch, seq=seq, dil=1, lt=512)


def _diff_attn_kernel(q_ref, k_ref, v_ref, lq1_ref, lk1_ref, lq2_ref, lk2_ref, sub_ref, o_ref,
                      s_ref, m_ref, l_ref, acc_ref, *, tq, lam_init):
    i = pl.program_id(2)
    q = q_ref[...]
    lane = lax.broadcasted_iota(jnp.int32, q.shape, 1)
    zero = jnp.zeros_like(q)
    qq = jnp.concatenate([jnp.where(lane < DIFF_QK_DIM, q, zero), jnp.where(lane < DIFF_QK_DIM, zero, q)], axis=0)
    tiles = tq // LANES

    def lane_fold(x, op):
        r = x[:, 0:LANES]
        for t in range(1, tiles):
            r = op(r, x[:, t * LANES:(t + 1) * LANES])
        return r

    m_ref[...] = jnp.full(m_ref.shape, NEG_INF, F32)

    def scores(j, masked):
        k0 = pl.multiple_of(j * tq, tq)
        s = _dot_nt(qq, k_ref[pl.ds(k0, tq), :]) * np.float32(math.log2(math.e))
        if masked:
            row = lax.broadcasted_iota(jnp.int32, s.shape, 0)
            col = lax.broadcasted_iota(jnp.int32, s.shape, 1)
            row = jnp.where(row >= tq, row - tq, row)
            s = jnp.where(col <= row, s, NEG_INF)
        s_ref[j] = s
        m_ref[...] = jnp.maximum(m_ref[...], lane_fold(s, jnp.maximum))

    def accumulate(j):
        k0 = pl.multiple_of(j * tq, tq)
        s = s_ref[j]
        mx = m_ref[...]
        ps = [jnp.exp2(s[:, t * LANES:(t + 1) * LANES] - mx) for t in range(tiles)]
        part = ps[0]
        for t in range(1, tiles):
            part = part + ps[t]
        l_ref[...] += part
        p = jnp.concatenate(ps, axis=1).astype(BF16)
        acc_ref[...] += _dot(p, v_ref[pl.ds(k0, tq), :])

    def in_pairs(fn, count):
        def body(jp, carry):
            fn(2 * jp)
            fn(2 * jp + 1)
            return carry

        lax.fori_loop(0, count // 2, body, 0)

        @pl.when(count % 2 == 1)
        def _():
            fn(count - 1)

    in_pairs(lambda j: scores(j, False), i)
    scores(i, True)
    m_ref[...] = jnp.broadcast_to(jnp.max(m_ref[...], axis=-1, keepdims=True), m_ref.shape)

    l_ref[...] = jnp.zeros(l_ref.shape, F32)
    acc_ref[...] = jnp.zeros(acc_ref.shape, F32)
    in_pairs(accumulate, i + 1)

    o = acc_ref[...] / jnp.sum(l_ref[...], axis=-1, keepdims=True)
    lam = (jnp.exp(jnp.sum(lq1_ref[...] * lk1_ref[...], axis=-1, keepdims=True))
           - jnp.exp(jnp.sum(lq2_ref[...] * lk2_ref[...], axis=-1, keepdims=True)) + np.float32(lam_init))
    od = o[0:tq, :] - lam * o[tq:2 * tq, :]
    o_ref[...] = (_rms(od, sub_ref[...]) * np.float32(1.0 - lam_init)).astype(o_ref.dtype)


def _diff_attn(q, k, v, lq1, lk1, lq2, lk2, subln, *, batch, seq, tq, lam_init):
    m, bw = q.shape
    heads = bw // HEAD_DIM
    nq = seq // tq
    q_spec = pl.BlockSpec((tq, HEAD_DIM), lambda b, h, i: (b * nq + i, h))
    kv_spec = pl.BlockSpec((seq, HEAD_DIM), lambda b, h, i: (b, h))
    lam_spec = pl.BlockSpec((1, DIFF_QK_DIM), lambda b, h, i: (0, 0))
    stat = pltpu.VMEM((2 * tq, LANES), F32)
    return pl.pallas_call(
        functools.partial(_diff_attn_kernel, tq=tq, lam_init=lam_init),
        grid=(batch, heads, nq),
        in_specs=[q_spec, kv_spec, kv_spec, lam_spec, lam_spec, lam_spec, lam_spec,
                  pl.BlockSpec((1, HEAD_DIM), lambda b, h, i: (0, 0))],
        out_specs=q_spec,
        out_shape=jax.ShapeDtypeStruct((m, bw), BF16),
        scratch_shapes=[pltpu.VMEM((nq, 2 * tq, tq), F32), stat, stat, stat],
        compiler_params=_params("parallel", "parallel", "arbitrary"),
        name="diff_attn",
    )(q, k, v, lq1, lk1, lq2, lk2, subln)


def _merge_kernel(ya_ref, yb_ref, yc_ref, yd_ref, ga_ref, gb_ref, gc_ref, gd_ref, wb_ref, o_ref, *, tn):
    branches = ((ya_ref, ga_ref), (yb_ref, gb_ref), (yc_ref, gc_ref), (yd_ref, gd_ref))
    for c in range(o_ref.shape[1] // tn):
        cols = slice(c * tn, (c + 1) * tn)
        acc = None
        for b, (y_ref, gate_ref) in enumerate(branches):
            term = gate_ref[:, cols].astype(F32) * _dot(y_ref[...], wb_ref[b, :, cols])
            acc = term if acc is None else acc + term
        o_ref[:, cols] = acc.astype(BF16)


def _merge(ys, gates, w_branch_layers, layer, *, tm, tn):
    m, bw = ys[0].shape
    _, nb, _, d = w_branch_layers.shape
    assert nb == len(ys) == 4
    y_spec = pl.BlockSpec((tm, bw), lambda r: (r, 0))
    gate_specs = [pl.BlockSpec((tm, d), lambda r, b=b: (r, b)) for b in range(nb)]
    return pl.pallas_call(
        functools.partial(_merge_kernel, tn=tn),
        grid=(m // tm,),
        in_specs=[y_spec] * nb + gate_specs + [_resident((None, nb, bw, d), lambda r: (layer, 0, 0, 0))],
        out_specs=pl.BlockSpec((tm, d), lambda r: (r, 0)),
        out_shape=jax.ShapeDtypeStruct((m, d), BF16),
        compiler_params=_params("parallel"),
        name="merge_branches",
    )(*ys, gates, gates, gates, gates, w_branch_layers)


def kernel(x, positions, ffn1_norm, ffn1_w13, ffn1_w2, mix_norm, w_in, conv_w, conv_b, conv_ln_g, conv_ln_b, sgu_ln_g, sgu_ln_b, sgu_w, sgu_b, dil_q_norm, dil_k_norm, diff_q_norm, diff_k_norm, diff_lq1, diff_lk1, diff_lq2, diff_lk2, diff_subln, w_branch, w_out, ffn2_norm, ffn2_w13, ffn2_w2):
    batch, seq, d = x.shape
    depth = w_in.shape[0]
    m = batch * seq
    bw = d // 4
    n_mix = 10 * bw
    assert bw % HEAD_DIM == 0 and w_in.shape[2] == n_mix + 4 * d

    tm = min(1024, m)
    dff = ffn1_w2.shape[1]
    tf = 512 if dff % 512 == 0 else 256
    tn = min(512, d)

    def row(v, l):
        return v[l].reshape(1, -1)

    def ffn(x2, g, w13_layers, w2_layers, l):
        act = _ffn_up(x2, g, w13_layers, l, tm=tm, tf=tf)
        return _residual_matmul(act, w2_layers, l, x2, tm=tm, tn=tn, name="ffn_down")

    x2 = x.reshape(m, d)
    cos_c, sin_c, cos_d, sin_d = _rope_tables(positions, ts=min(512, m))
    ffn1_w2_bf, ffn2_w2_bf = ffn1_w2.astype(BF16), ffn2_w2.astype(BF16)
    w_mix_bf = _cast_layers(w_in, n_mix, tr=min(512, d))
    w_out_bf = _cast_layers(w_out, d, tr=min(1024, d))
    w_branch_bf = _cast_layers(w_branch.reshape(depth, 4 * bw, d), d, tr=min(1024, d)).reshape(w_branch.shape)

    for l in range(depth):
        lam_init = 0.8 - 0.6 * math.exp(-0.3 * l)
        x2 = ffn(x2, row(ffn1_norm, l), ffn1_w13, ffn1_w2_bf, l)

        z, yb, (cq, ck, cv), (dq, dk, dv) = _mix_proj(
            x2, row(mix_norm, l), w_mix_bf, l, row(sgu_ln_g, l), row(sgu_ln_b, l),
            sgu_w[l], sgu_b[l], row(dil_q_norm, l), row(dil_k_norm, l), cos_c, sin_c,
            row(diff_q_norm, l), row(diff_k_norm, l), cos_d, sin_d, bw=bw, tm=min(512, m))
        gates = _gate_proj(x2, row(mix_norm, l), w_in, l, n_mix=n_mix, tm=tm, tn=math.gcd(n_mix, 1024))

        ya = _conv_mixer(z, conv_w[l], row(conv_b, l), row(conv_ln_g, l), row(conv_ln_b, l),
                         batch=batch, seq=seq, ts=min(512, seq))
        yc = _dil_attn(cq, ck, cv, batch=batch, seq=seq)
        yd = _diff_attn(dq, dk, dv, row(diff_lq1, l), row(diff_lk1, l), row(diff_lq2, l), row(diff_lk2, l),
                        row(diff_subln, l), batch=batch, seq=seq, tq=min(512, seq), lam_init=lam_init)

        merged = _merge((ya, yb, yc, yd), gates, w_branch_bf, l, tm=min(512, m), tn=tn)
        x2 = _residual_matmul(merged, w_out_bf, l, x2, tm=min(512, m), tn=d, name="out_proj")

        x2 = ffn(x2, row(ffn2_norm, l), ffn2_w13, ffn2_w2_bf, l)
    return x2.reshape(batch, seq, d)
```

```python
import functools
import math

import jax
import jax.numpy as jnp
import numpy as np
from jax import lax
from jax.experimental import pallas as pl
from jax.experimental.pallas import tpu as pltpu

HEAD_DIM = 128
LANES = 128
SUBLANES = 8
DIFF_QK_DIM = HEAD_DIM // 2
SGU_CHUNK = 128
DIL_PATTERNS = ((128, 1), (512, 4), (2048, 16))
DIL_SPAN = 128
DIL_BLOCK = 128
ROPE_THETA = 500000.0
ROPE_FRACTION = 4
EPS = 1e-6
NEG_INF = -1e30
ROW_SLABS = 4
CONV_HALO = 32

VMEM_LIMIT_BYTES = 56 * 1024 * 1024

BF16 = jnp.bfloat16
F32 = jnp.float32


def _params(*sem):
    return pltpu.CompilerParams(dimension_semantics=sem, vmem_limit_bytes=VMEM_LIMIT_BYTES)


def _resident(shape, index_map):
    return pl.BlockSpec(shape, index_map, pipeline_mode=pl.Buffered(1))


def _dot(a, b):
    return jnp.dot(a, b, preferred_element_type=F32)


def _dot_nt(a, b):
    return lax.dot_general(a, b, (((1,), (1,)), ((), ())), preferred_element_type=F32)


def _rms(x, g):
    return x * lax.rsqrt(jnp.mean(x * x, axis=-1, keepdims=True) + EPS) * g


def _layernorm(x, g, b):
    mu = jnp.mean(x, axis=-1, keepdims=True)
    xc = x - mu
    return xc * lax.rsqrt(jnp.mean(xc * xc, axis=-1, keepdims=True) + EPS) * g + b


def _staggered_row_specs(tm, d, n_rows, n_inner):
    slabs = min(ROW_SLABS, n_inner)
    width = d // slabs

    def spec(q):
        first = n_inner - (slabs - 1 - q)
        return pl.BlockSpec(
            (tm, width), lambda i, j: (jnp.minimum(i + jnp.where(j >= first, 1, 0), n_rows - 1), q))

    return [spec(q) for q in range(slabs)]


def _rms_slabs_to(h_ref, x_refs, g_ref):
    d = h_ref.shape[1]
    width = d // len(x_refs)
    ssq = None
    for x_ref in x_refs:
        x = x_ref[...]
        part = jnp.sum(x * x, axis=-1, keepdims=True)
        ssq = part if ssq is None else ssq + part
    inv = lax.rsqrt(ssq * np.float32(1.0 / d) + EPS)
    for q, x_ref in enumerate(x_refs):
        cols = slice(q * width, (q + 1) * width)
        h_ref[:, cols] = (x_ref[...] * inv * g_ref[:, cols]).astype(BF16)


def _gelu(x):
    return 0.5 * x * (1.0 + lax.erf(x * np.float32(math.sqrt(0.5))))


def _cast_kernel(w_ref, o_ref):
    o_ref[...] = w_ref[...].astype(BF16)


def _cast_layers(w_layers, ncols, *, tr):
    depth, rows, _ = w_layers.shape
    return pl.pallas_call(
        _cast_kernel,
        grid=(depth, rows // tr),
        in_specs=[pl.BlockSpec((None, tr, ncols), lambda l, i: (l, i, 0))],
        out_specs=pl.BlockSpec((None, tr, ncols), lambda l, i: (l, i, 0)),
        out_shape=jax.ShapeDtypeStruct((depth, rows, ncols), BF16),
        compiler_params=_params("parallel", "parallel"),
        name="cast_weights",
    )(w_layers)


def _ffn_up_kernel(*refs):
    *x_refs, g_ref, wg_ref, wu_ref, a_ref, h_ref = refs

    @pl.when(pl.program_id(1) == 0)
    def _():
        _rms_slabs_to(h_ref, x_refs, g_ref)

    h = h_ref[...]
    gate = _dot(h, wg_ref[...].astype(BF16))
    up = _dot(h, wu_ref[...].astype(BF16))
    a_ref[...] = (0.5 * gate * jax.nn.sigmoid(gate) * up).astype(BF16)


def _ffn_up(x2, g, w13_layers, layer, *, tm, tf):
    m, d = x2.shape
    dff = w13_layers.shape[2] // 2
    nf = dff // tf
    x_specs = _staggered_row_specs(tm, d, m // tm, nf)
    return pl.pallas_call(
        _ffn_up_kernel,
        grid=(m // tm, nf),
        in_specs=x_specs + [
            pl.BlockSpec((1, d), lambda i, j: (0, 0)),
            pl.BlockSpec((None, d, tf), lambda i, j: (layer, 0, j)),
            pl.BlockSpec((None, d, tf), lambda i, j: (layer, 0, j + nf)),
        ],
        out_specs=pl.BlockSpec((tm, tf), lambda i, j: (i, j)),
        out_shape=jax.ShapeDtypeStruct((m, dff), BF16),
        scratch_shapes=[pltpu.VMEM((tm, d), BF16)],
        compiler_params=_params("parallel", "arbitrary"),
        name="ffn_up",
    )(*[x2] * len(x_specs), g, w13_layers, w13_layers)


def _residual_matmul_kernel(a_ref, w_ref, x_ref, o_ref):
    o_ref[...] = x_ref[...] + _dot(a_ref[...], w_ref[...])


def _residual_matmul(a, w_layers, layer, x2, *, tm, tn, name):
    m, kdim = a.shape
    d = w_layers.shape[2]
    tile = pl.BlockSpec((tm, tn), lambda i, j: (i, j))
    w_spec = _resident if tn == d else pl.BlockSpec
    return pl.pallas_call(
        _residual_matmul_kernel,
        grid=(m // tm, d // tn),
        in_specs=[pl.BlockSpec((tm, kdim), lambda i, j: (i, 0)),
                  w_spec((None, kdim, tn), lambda i, j: (layer, 0, j)), tile],
        out_specs=tile,
        out_shape=jax.ShapeDtypeStruct((m, d), F32),
        compiler_params=_params("parallel", "arbitrary"),
        name=name,
    )(a, w_layers, x2)


def _rope_table_kernel(pos_ref, inv_c_ref, sgn_c_ref, inv_d_ref, sgn_d_ref,
                       cos_c_ref, sin_c_ref, cos_d_ref, sin_d_ref):
    p = pos_ref[...].astype(F32)
    ang_c = p * inv_c_ref[...]
    cos_c_ref[...] = jnp.cos(ang_c)
    sin_c_ref[...] = jnp.sin(ang_c) * sgn_c_ref[...]
    ang_d = p * inv_d_ref[...]
    cos_d_ref[...] = jnp.cos(ang_d)
    sin_d_ref[...] = jnp.sin(ang_d) * sgn_d_ref[...]


def _rope_lane_consts(width):
    rot = width // ROPE_FRACTION
    half = rot // 2
    inv_freq = 1.0 / (ROPE_THETA ** (jnp.arange(half, dtype=F32) * 2.0 / rot))
    lane = np.arange(LANES) % width
    in_rot = lane < rot
    inv = jnp.where(in_rot, inv_freq[lane % half], 0.0).astype(F32)
    sgn = np.where(lane < half, -1.0, np.where(in_rot, 1.0, 0.0)).astype(np.float32)
    return inv.reshape(1, LANES), jnp.asarray(sgn).reshape(1, LANES)


def _rope_tables(positions, *, ts):
    m = positions.size
    inv_c, sgn_c = _rope_lane_consts(HEAD_DIM)
    inv_d, sgn_d = _rope_lane_consts(DIFF_QK_DIM)
    lane_spec = pl.BlockSpec((1, LANES), lambda i: (0, 0))
    tab_spec = pl.BlockSpec((ts, LANES), lambda i: (i, 0))
    tab = jax.ShapeDtypeStruct((m, LANES), F32)
    return pl.pallas_call(
        _rope_table_kernel,
        grid=(m // ts,),
        in_specs=[pl.BlockSpec((ts, 1), lambda i: (i, 0)), lane_spec, lane_spec, lane_spec, lane_spec],
        out_specs=[tab_spec] * 4,
        out_shape=[tab] * 4,
        compiler_params=_params("parallel"),
        name="rope_tables",
    )(positions.reshape(m, 1), inv_c, sgn_c, inv_d, sgn_d)


def _rope(x, cos, sin, half):
    lane = lax.broadcasted_iota(jnp.int32, x.shape, 1)
    lower = (lane % (2 * half)) < half
    partner = jnp.where(lower, pltpu.roll(x, LANES - half, 1), pltpu.roll(x, half, 1))
    return x * cos + partner * sin


def _segment_rms(x, g):
    lane = lax.broadcasted_iota(jnp.int32, x.shape, 1)
    lower = lane < DIFF_QK_DIM
    sq = x * x
    lo = jnp.sum(jnp.where(lower, sq, 0.0), axis=-1, keepdims=True)
    hi = jnp.sum(jnp.where(lower, 0.0, sq), axis=-1, keepdims=True)
    ms = jnp.where(lower, lo, hi) * np.float32(1.0 / DIFF_QK_DIM)
    return x * lax.rsqrt(ms + EPS) * g


def _mix_proj_kernel(x_ref, g_ref, w_ref, sg_ref, sb_ref, ws_ref, bias_ref, cqg_ref, ckg_ref, cos_c_ref, sin_c_ref,
                     dqg_ref, dkg_ref, cos_d_ref, sin_d_ref,
                     z_ref, yb_ref, dq_ref, dk_ref, dv_ref, *c_refs_and_scratch, bw, tm, dils):
    c_refs, stage_ref = c_refs_and_scratch[:-1], c_refs_and_scratch[-1]
    h = _rms(x_ref[...], g_ref[...]).astype(BF16)
    heads = bw // HEAD_DIM

    def seg(c0, n):
        return _dot(h, w_ref[:, c0 * bw:(c0 + n) * bw])

    r = seg(2, 2)
    u = _gelu(r[:, :bw])
    v = _layernorm(_gelu(r[:, bw:]), sg_ref[...], sb_ref[...]).astype(BF16)
    row = lax.broadcasted_iota(jnp.int32, (SGU_CHUNK, SGU_CHUNK), 0)
    col = lax.broadcasted_iota(jnp.int32, (SGU_CHUNK, SGU_CHUNK), 1)
    for g in range(heads):
        lanes = slice(g * HEAD_DIM, (g + 1) * HEAD_DIM)
        w = jnp.where(row >= col, ws_ref[g], 0.0).astype(BF16)
        for c in range(tm // SGU_CHUNK):
            rws = slice(c * SGU_CHUNK, (c + 1) * SGU_CHUNK)
            mixed = _dot(w, v[rws, lanes]) + bias_ref[:, lanes]
            yb_ref[rws, lanes] = (u[rws, lanes] * mixed).astype(BF16)

    r = seg(4, 3)
    cos, sin = cos_c_ref[...], sin_c_ref[...]
    half = HEAD_DIM // ROPE_FRACTION // 2
    for part in range(3):
        outs = c_refs[part * len(dils):(part + 1) * len(dils)]
        for hd in range(heads):
            lanes = slice(hd * HEAD_DIM, (hd + 1) * HEAD_DIM)
            t = r[:, part * bw + hd * HEAD_DIM:part * bw + (hd + 1) * HEAD_DIM]
            if part < 2:
                t = _rope(_rms(t, (cqg_ref, ckg_ref)[part][...]), cos, sin, half)
            stage_ref[hd] = t
            for dil, o_ref in zip(dils, outs):
                if dil == 1:
                    o_ref[:, lanes] = t.astype(BF16)
        for dil, o_ref in zip(dils, outs):
            if dil == 1:
                continue
            for res in range(dil):
                for hd in range(heads):
                    o_ref[:, res * bw + hd * HEAD_DIM:res * bw + (hd + 1) * HEAD_DIM] = (
                        stage_ref[hd, pl.ds(res, tm // dil, stride=dil), :].astype(BF16))

    r = seg(7, 3)
    cos, sin = cos_d_ref[...], sin_d_ref[...]
    half = DIFF_QK_DIM // ROPE_FRACTION // 2
    scale = np.float32(DIFF_QK_DIM ** -0.5)
    for hd in range(heads):
        lanes = slice(hd * HEAD_DIM, (hd + 1) * HEAD_DIM)
        klanes = slice(bw + hd * HEAD_DIM, bw + (hd + 1) * HEAD_DIM)
        q = _rope(_segment_rms(r[:, lanes], dqg_ref[...]), cos, sin, half)
        dq_ref[:, lanes] = (q * scale).astype(BF16)
        dk_ref[:, lanes] = _rope(_segment_rms(r[:, klanes], dkg_ref[...]), cos, sin, half).astype(BF16)
    dv_ref[...] = r[:, 2 * bw:].astype(BF16)

    r = seg(0, 2)
    z_ref[...] = r[:, :bw] * jax.nn.sigmoid(r[:, bw:])


def _mix_proj(x2, g, w_mix_layers, layer, sgu_g, sgu_b, w_s, b_s, cq_g, ck_g, cos_c, sin_c, dq_g, dk_g, cos_d, sin_d,
              *, bw, tm):
    m, d = x2.shape
    groups = w_s.shape[0]
    heads = bw // HEAD_DIM
    dils = tuple(dil for _, dil in DIL_PATTERNS)
    assert groups == heads and w_mix_layers.shape[2] == 10 * bw
    assert all(tm % (dil * 2 * SUBLANES) == 0 for dil in dils)
    bias = jnp.repeat(b_s.T, HEAD_DIM, axis=1)
    vec = pl.BlockSpec((1, bw), lambda i: (0, 0))
    lane_spec = pl.BlockSpec((1, LANES), lambda i: (0, 0))
    tab_spec = pl.BlockSpec((tm, LANES), lambda i: (i, 0))
    out_spec = pl.BlockSpec((tm, bw), lambda i: (i, 0))
    out_bf = jax.ShapeDtypeStruct((m, bw), BF16)
    view_specs = [pl.BlockSpec((tm // dil, dil * bw), lambda i: (i, 0)) for dil in dils] * 3
    view_shapes = [jax.ShapeDtypeStruct((m // dil, dil * bw), BF16) for dil in dils] * 3
    outs = pl.pallas_call(
        functools.partial(_mix_proj_kernel, bw=bw, tm=tm, dils=dils),
        grid=(m // tm,),
        in_specs=[
            pl.BlockSpec((tm, d), lambda i: (i, 0)),
            pl.BlockSpec((1, d), lambda i: (0, 0)),
            _resident((None, d, 10 * bw), lambda i: (layer, 0, 0)),
            vec, vec,
            pl.BlockSpec((groups, SGU_CHUNK, SGU_CHUNK), lambda i: (0, 0, 0)),
            pl.BlockSpec((SGU_CHUNK, bw), lambda i: (0, 0)),
            lane_spec, lane_spec, tab_spec, tab_spec,
            lane_spec, lane_spec, tab_spec, tab_spec,
        ],
        out_specs=[out_spec] * 5 + view_specs,
        out_shape=[jax.ShapeDtypeStruct((m, bw), F32)] + [out_bf] * 4 + view_shapes,
        scratch_shapes=[pltpu.VMEM((heads, tm, HEAD_DIM), F32)],
        compiler_params=_params("parallel"),
        name="mix_proj",
    )(x2, g, w_mix_layers, sgu_g, sgu_b, w_s, bias, cq_g, ck_g, cos_c, sin_c,
      jnp.tile(dq_g, (1, 2)), jnp.tile(dk_g, (1, 2)), cos_d, sin_d)
    n = len(dils)
    z, yb, dq, dk, dv = outs[:5]
    cq, ck, cv = outs[5:5 + n], outs[5 + n:5 + 2 * n], outs[5 + 2 * n:]
    return z, yb, (cq, ck, cv), (dq, dk, dv)


def _gate_proj_kernel(*refs):
    *x_refs, g_ref, w_ref, o_ref, h_ref = refs

    @pl.when(pl.program_id(1) == 0)
    def _():
        _rms_slabs_to(h_ref, x_refs, g_ref)

    o_ref[...] = jax.nn.sigmoid(_dot(h_ref[...], w_ref[...].astype(BF16))).astype(BF16)


def _gate_proj(x2, g, w_in_layers, layer, *, n_mix, tm, tn):
    m, d = x2.shape
    n = w_in_layers.shape[2] - n_mix
    assert n_mix % tn == 0 and n % tn == 0
    first = n_mix // tn
    x_specs = _staggered_row_specs(tm, d, m // tm, n // tn)
    return pl.pallas_call(
        _gate_proj_kernel,
        grid=(m // tm, n // tn),
        in_specs=x_specs + [pl.BlockSpec((1, d), lambda i, j: (0, 0)),
                            pl.BlockSpec((None, d, tn), lambda i, j: (layer, 0, first + j))],
        out_specs=pl.BlockSpec((tm, tn), lambda i, j: (i, j)),
        out_shape=jax.ShapeDtypeStruct((m, n), BF16),
        scratch_shapes=[pltpu.VMEM((tm, d), BF16)],
        compiler_params=_params("parallel", "arbitrary"),
        name="gate_proj",
    )(*[x2] * len(x_specs), g, w_in_layers)


def _conv_kernel(zin_ref, w_ref, b_ref, lg_ref, lb_ref, o_ref, z_ref, *, ts, width, rows):
    ext = ts + CONV_HALO

    @pl.when(pl.program_id(1) == 0)
    def _():
        z_ref[0, 0:CONV_HALO, :] = jnp.zeros((CONV_HALO, z_ref.shape[2]), F32)

    z_ref[0, CONV_HALO:ext, :] = zin_ref[...]
    for s in range(1, SUBLANES):
        z_ref[s, 0:ext - SUBLANES, :] = z_ref[0, s:s + ext - SUBLANES, :]
    ch = z_ref.shape[2]
    for c in range(ts // rows):
        r0 = c * rows
        acc = jnp.broadcast_to(b_ref[...], (rows, ch))
        for k in range(width):
            off = CONV_HALO - (width - 1) + k
            base = r0 + off // SUBLANES * SUBLANES
            acc = acc + w_ref[k:k + 1, :] * z_ref[off % SUBLANES, base:base + rows, :]
        y = _layernorm(acc, lg_ref[...], lb_ref[...])
        o_ref[r0:r0 + rows, :] = (y * jax.nn.sigmoid(y)).astype(o_ref.dtype)
    z_ref[0, 0:CONV_HALO, :] = z_ref[0, ts:ext, :]


def _conv_mixer(z, conv_w, conv_b, ln_g, ln_b, *, batch, seq, ts):
    m, bw = z.shape
    width = conv_w.shape[0]
    assert width - 1 <= CONV_HALO
    ns = seq // ts
    vec = pl.BlockSpec((1, bw), lambda b, s: (0, 0))
    tile = pl.BlockSpec((ts, bw), lambda b, s: (b * ns + s, 0))
    return pl.pallas_call(
        functools.partial(_conv_kernel, ts=ts, width=width, rows=64),
        grid=(batch, ns),
        in_specs=[tile, pl.BlockSpec((width, bw), lambda b, s: (0, 0)), vec, vec, vec],
        out_specs=tile,
        out_shape=jax.ShapeDtypeStruct((m, bw), BF16),
        scratch_shapes=[pltpu.VMEM((SUBLANES, ts + CONV_HALO, bw), F32)],
        compiler_params=_params("arbitrary", "arbitrary"),
        name="conv_mixer",
    )(z, conv_w, conv_b, ln_g, ln_b)


def _dil_band_kernel(*refs, lt, heads, others):
    bw = heads * HEAD_DIM
    if others:
        q_ref, k_ref, v_ref = refs[:3]
        other_refs = refs[3:3 + 2 * len(others)]
        y_ref = refs[3 + 2 * len(others)]
        stage_refs = refs[4 + 2 * len(others):]
        for dil, o_ref, lse_ref, o_st, lse_st in zip(others, other_refs[0::2], other_refs[1::2],
                                                     stage_refs[0::2], stage_refs[1::2]):
            for res in range(dil):
                for hd in range(heads):
                    cols = slice(res * bw + hd * HEAD_DIM, res * bw + (hd + 1) * HEAD_DIM)
                    o_st[hd, pl.ds(res, lt // dil, stride=dil), :] = o_ref[:, cols]
                    lse_st[hd, pl.ds(res, lt // dil, stride=dil), :] = lse_ref[:, cols]
    else:
        q_ref, k_ref, v_ref, o_ref, lse_ref = refs
    i = pl.program_id(2)
    blk = DIL_BLOCK
    scale = np.float32(HEAD_DIM ** -0.5)
    ones = jnp.ones((2 * blk, HEAD_DIM), BF16)
    row = lax.broadcasted_iota(jnp.int32, (blk, 2 * blk), 0)
    col = lax.broadcasted_iota(jnp.int32, (blk, 2 * blk), 1)
    for jb in range(lt // blk):
        rws = slice(jb * blk, (jb + 1) * blk)
        l0 = i * lt + jb * blk
        ks = pl.multiple_of(jnp.maximum(l0 - blk, 0), blk)
        dist = (l0 - ks) + row - col
        valid = (dist >= 0) & (dist <= DIL_SPAN)
        for hd in range(heads):
            lanes = slice(hd * HEAD_DIM, (hd + 1) * HEAD_DIM)
            s = _dot_nt(q_ref[rws, lanes], k_ref[pl.ds(ks, 2 * blk), lanes]) * scale
            s = jnp.where(valid, s, NEG_INF)
            mx = jnp.max(s, axis=-1, keepdims=True)
            p = jnp.exp(s - mx).astype(BF16)
            r = _dot(p, jnp.concatenate([v_ref[pl.ds(ks, 2 * blk), lanes], ones], axis=1))
            den = r[:, HEAD_DIM:]
            o = r[:, :HEAD_DIM] / den
            lse = mx + jnp.log(den)
            if others:
                o_all = [o] + [st[hd, rws, :] for st in stage_refs[0::2]]
                lse_all = [lse] + [st[hd, rws, :] for st in stage_refs[1::2]]
                top = functools.reduce(jnp.maximum, lse_all)
                wts = [jnp.exp(t - top) for t in lse_all]
                num = wts[0] * o_all[0]
                tot = wts[0]
                for wg, og in zip(wts[1:], o_all[1:]):
                    num = num + wg * og
                    tot = tot + wg
                y_ref[rws, lanes] = (num / tot).astype(BF16)
            else:
                o_ref[rws, lanes] = o
                lse_ref[rws, lanes] = lse


def _dil_band(q, k, v, others, *, batch, seq, dil, lt, classes=1):
    rows, cols = q.shape
    bw = cols // dil * classes
    heads = bw // HEAD_DIM
    ln = seq // dil
    lt = min(lt, ln)
    nl = ln // lt
    assert ln % lt == 0 and lt % DIL_BLOCK == 0 and ln >= 2 * DIL_BLOCK and dil % classes == 0
    tile = pl.BlockSpec((lt, bw), lambda b, r, i: (b * nl + i, r))
    whole = pl.BlockSpec((ln, bw), lambda b, r, i: (b, r))
    if others:
        assert dil == 1 and all(lt % (od * SUBLANES) == 0 for od, _, _ in others)
        extra = [a for _, o, lse in others for a in (o, lse)]
        extra_specs = [pl.BlockSpec((lt // od, od * bw), lambda b, r, i: (b * nl + i, 0))
                       for od, _, _ in others for _ in range(2)]
        out_specs, out_shape = tile, jax.ShapeDtypeStruct((rows, cols), BF16)
        scratch = [pltpu.VMEM((heads, lt, HEAD_DIM), F32)] * len(extra)
    else:
        extra, extra_specs, scratch = [], [], []
        out_specs = [tile, tile]
        out_shape = [jax.ShapeDtypeStruct((rows, cols), F32)] * 2
    return pl.pallas_call(
        functools.partial(_dil_band_kernel, lt=lt, heads=heads, others=tuple(od for od, _, _ in others)),
        grid=(batch, dil // classes, nl),
        in_specs=[tile, whole, whole] + extra_specs,
        out_specs=out_specs,
        out_shape=out_shape,
        scratch_shapes=scratch,
        compiler_params=_params("parallel", "parallel", "arbitrary"),
        name=f"dil_band_{dil}",
    )(q, k, v, *extra)


def _dil_attn(cq, ck, cv, *, batch, seq):
    dils = [d for _, d in DIL_PATTERNS]
    assert all(w // d == DIL_SPAN for w, d in DIL_PATTERNS) and dils[0] == 1
    others = []
    heads = cq[0].shape[1] // HEAD_DIM
    for idx in range(1, len(dils)):
        dil = dils[idx]
        lt = min(seq // dil, 8 * DIL_BLOCK)
        classes = math.gcd(dil, max(1, 32 // (lt // DIL_BLOCK * heads)))
        o, lse = _dil_band(cq[idx], ck[idx], cv[idx], [], batch=batch, seq=seq, dil=dil, lt=lt, classes=classes)
        others.append((dil, o, lse))
    return _dil_band(cq[0], ck[0], cv[0], others, batch=batch, seq=seq, dil=1, lt=512)


def _diff_attn_kernel(q_ref, k_ref, v_ref, lq1_ref, lk1_ref, lq2_ref, lk2_ref, sub_ref, o_ref,
                      s_ref, m_ref, l_ref, acc_ref, *, tq, lam_init):
    i = pl.program_id(2)
    q = q_ref[...]
    lane = lax.broadcasted_iota(jnp.int32, q.shape, 1)
    zero = jnp.zeros_like(q)
    qq = jnp.concatenate([jnp.where(lane < DIFF_QK_DIM, q, zero), jnp.where(lane < DIFF_QK_DIM, zero, q)], axis=0)
    tiles = tq // LANES

    def lane_fold(x, op):
        r = x[:, 0:LANES]
        for t in range(1, tiles):
            r = op(r, x[:, t * LANES:(t + 1) * LANES])
        return r

    m_ref[...] = jnp.full(m_ref.shape, NEG_INF, F32)

    def scores(j, masked):
        k0 = pl.multiple_of(j * tq, tq)
        s = _dot_nt(qq, k_ref[pl.ds(k0, tq), :]) * np.float32(math.log2(math.e))
        if masked:
            row = lax.broadcasted_iota(jnp.int32, s.shape, 0)
            col = lax.broadcasted_iota(jnp.int32, s.shape, 1)
            row = jnp.where(row >= tq, row - tq, row)
            s = jnp.where(col <= row, s, NEG_INF)
        s_ref[j] = s
        m_ref[...] = jnp.maximum(m_ref[...], lane_fold(s, jnp.maximum))

    def accumulate(j):
        k0 = pl.multiple_of(j * tq, tq)
        s = s_ref[j]
        mx = m_ref[...]
        ps = [jnp.exp2(s[:, t * LANES:(t + 1) * LANES] - mx) for t in range(tiles)]
        part = ps[0]
        for t in range(1, tiles):
            part = part + ps[t]
        l_ref[...] += part
        p = jnp.concatenate(ps, axis=1).astype(BF16)
        acc_ref[...] += _dot(p, v_ref[pl.ds(k0, tq), :])

    def in_pairs(fn, count):
        def body(jp, carry):
            fn(2 * jp)
            fn(2 * jp + 1)
            return carry

        lax.fori_loop(0, count // 2, body, 0)

        @pl.when(count % 2 == 1)
        def _():
            fn(count - 1)

    in_pairs(lambda j: scores(j, False), i)
    scores(i, True)
    m_ref[...] = jnp.broadcast_to(jnp.max(m_ref[...], axis=-1, keepdims=True), m_ref.shape)

    l_ref[...] = jnp.zeros(l_ref.shape, F32)
    acc_ref[...] = jnp.zeros(acc_ref.shape, F32)
    in_pairs(accumulate, i + 1)

    o = acc_ref[...] / jnp.sum(l_ref[...], axis=-1, keepdims=True)
    lam = (jnp.exp(jnp.sum(lq1_ref[...] * lk1_ref[...], axis=-1, keepdims=True))
           - jnp.exp(jnp.sum(lq2_ref[...] * lk2_ref[...], axis=-1, keepdims=True)) + np.float32(lam_init))
    od = o[0:tq, :] - lam * o[tq:2 * tq, :]
    o_ref[...] = (_rms(od, sub_ref[...]) * np.float32(1.0 - lam_init)).astype(o_ref.dtype)


def _diff_attn(q, k, v, lq1, lk1, lq2, lk2, subln, *, batch, seq, tq, lam_init):
    m, bw = q.shape
    heads = bw // HEAD_DIM
    nq = seq // tq
    q_spec = pl.BlockSpec((tq, HEAD_DIM), lambda b, h, i: (b * nq + i, h))
    kv_spec = pl.BlockSpec((seq, HEAD_DIM), lambda b, h, i: (b, h))
    lam_spec = pl.BlockSpec((1, DIFF_QK_DIM), lambda b, h, i: (0, 0))
    stat = pltpu.VMEM((2 * tq, LANES), F32)
    return pl.pallas_call(
        functools.partial(_diff_attn_kernel, tq=tq, lam_init=lam_init),
        grid=(batch, heads, nq),
        in_specs=[q_spec, kv_spec, kv_spec, lam_spec, lam_spec, lam_spec, lam_spec,
                  pl.BlockSpec((1, HEAD_DIM), lambda b, h, i: (0, 0))],
        out_specs=q_spec,
        out_shape=jax.ShapeDtypeStruct((m, bw), BF16),
        scratch_shapes=[pltpu.VMEM((nq, 2 * tq, tq), F32), stat, stat, stat],
        compiler_params=_params("parallel", "parallel", "arbitrary"),
        name="diff_attn",
    )(q, k, v, lq1, lk1, lq2, lk2, subln)


def _merge_kernel(ya_ref, yb_ref, yc_ref, yd_ref, ga_ref, gb_ref, gc_ref, gd_ref, wb_ref, o_ref, *, tn):
    branches = ((ya_ref, ga_ref), (yb_ref, gb_ref), (yc_ref, gc_ref), (yd_ref, gd_ref))
    for c in range(o_ref.shape[1] // tn):
        cols = slice(c * tn, (c + 1) * tn)
        acc = None
        for b, (y_ref, gate_ref) in enumerate(branches):
            term = gate_ref[:, cols].astype(F32) * _dot(y_ref[...], wb_ref[b, :, cols])
            acc = term if acc is None else acc + term
        o_ref[:, cols] = acc.astype(BF16)


def _merge(ys, gates, w_branch_layers, layer, *, tm, tn):
    m, bw = ys[0].shape
    _, nb, _, d = w_branch_layers.shape
    assert nb == len(ys) == 4
    y_spec = pl.BlockSpec((tm, bw), lambda r: (r, 0))
    gate_specs = [pl.BlockSpec((tm, d), lambda r, b=b: (r, b)) for b in range(nb)]
    return pl.pallas_call(
        functools.partial(_merge_kernel, tn=tn),
        grid=(m // tm,),
        in_specs=[y_spec] * nb + gate_specs + [_resident((None, nb, bw, d), lambda r: (layer, 0, 0, 0))],
        out_specs=pl.BlockSpec((tm, d), lambda r: (r, 0)),
        out_shape=jax.ShapeDtypeStruct((m, d), BF16),
        compiler_params=_params("parallel"),
        name="merge_branches",
    )(*ys, gates, gates, gates, gates, w_branch_layers)


def kernel(x, positions, ffn1_norm, ffn1_w13, ffn1_w2, mix_norm, w_in, conv_w, conv_b, conv_ln_g, conv_ln_b, sgu_ln_g, sgu_ln_b, sgu_w, sgu_b, dil_q_norm, dil_k_norm, diff_q_norm, diff_k_norm, diff_lq1, diff_lk1, diff_lq2, diff_lk2, diff_subln, w_branch, w_out, ffn2_norm, ffn2_w13, ffn2_w2):
    batch, seq, d = x.shape
    depth = w_in.shape[0]
    m = batch * seq
    bw = d // 4
    n_mix = 10 * bw
    assert bw % HEAD_DIM == 0 and w_in.shape[2] == n_mix + 4 * d

    tm = min(1024, m)
    dff = ffn1_w2.shape[1]
    tf = 512 if dff % 512 == 0 else 256
    tn = min(512, d)

    def row(v, l):
        return v[l].reshape(1, -1)

    def ffn(x2, g, w13_layers, w2_layers, l):
        act = _ffn_up(x2, g, w13_layers, l, tm=tm, tf=tf)
        return _residual_matmul(act, w2_layers, l, x2, tm=tm, tn=tn, name="ffn_down")

    x2 = x.reshape(m, d)
    cos_c, sin_c, cos_d, sin_d = _rope_tables(positions, ts=min(512, m))
    ffn1_w2_bf, ffn2_w2_bf = ffn1_w2.astype(BF16), ffn2_w2.astype(BF16)
    w_mix_bf = _cast_layers(w_in, n_mix, tr=min(512, d))
    w_out_bf = _cast_layers(w_out, d, tr=min(1024, d))
    w_branch_bf = _cast_layers(w_branch.reshape(depth, 4 * bw, d), d, tr=min(1024, d)).reshape(w_branch.shape)

    for l in range(depth):
        lam_init = 0.8 - 0.6 * math.exp(-0.3 * l)
        x2 = ffn(x2, row(ffn1_norm, l), ffn1_w13, ffn1_w2_bf, l)

        z, yb, (cq, ck, cv), (dq, dk, dv) = _mix_proj(
            x2, row(mix_norm, l), w_mix_bf, l, row(sgu_ln_g, l), row(sgu_ln_b, l),
            sgu_w[l], sgu_b[l], row(dil_q_norm, l), row(dil_k_norm, l), cos_c, sin_c,
            row(diff_q_norm, l), row(diff_k_norm, l), cos_d, sin_d, bw=bw, tm=min(512, m))
        gates = _gate_proj(x2, row(mix_norm, l), w_in, l, n_mix=n_mix, tm=tm, tn=math.gcd(n_mix, 1024))

        ya = _conv_mixer(z, conv_w[l], row(conv_b, l), row(conv_ln_g, l), row(conv_ln_b, l),
                         batch=batch, seq=seq, ts=min(512, seq))
        yc = _dil_attn(cq, ck, cv, batch=batch, seq=seq)
        yd = _diff_attn(dq, dk, dv, row(diff_lq1, l), row(diff_lk1, l), row(diff_lq2, l), row(diff_lk2, l),
                        row(diff_subln, l), batch=batch, seq=seq, tq=min(512, seq), lam_init=lam_init)

        merged = _merge((ya, yb, yc, yd), gates, w_branch_bf, l, tm=min(512, m), tn=tn)
        x2 = _residual_matmul(merged, w_out_bf, l, x2, tm=min(512, m), tn=d, name="out_proj")

        x2 = ffn(x2, row(ffn2_norm, l), ffn2_w13, ffn2_w2_bf, l)
    return x2.reshape(batch, seq, d)
```

```python
import functools
import math

import jax
import jax.numpy as jnp
import numpy as np
from jax import lax
from jax.experimental import pallas as pl
from jax.experimental.pallas import tpu as pltpu

HEAD_DIM = 128
LANES = 128
SUBLANES = 8
DIFF_QK_DIM = HEAD_DIM // 2
SGU_CHUNK = 128
DIL_PATTERNS = ((128, 1), (512, 4), (2048, 16))
DIL_SPAN = 128
DIL_BLOCK = 128
ROPE_THETA = 500000.0
ROPE_FRACTION = 4
EPS = 1e-6
NEG_INF = -1e30
ROW_SLABS = 4
CONV_HALO = 32

VMEM_LIMIT_BYTES = 56 * 1024 * 1024

BF16 = jnp.bfloat16
F32 = jnp.float32


def _params(*sem):
    return pltpu.CompilerParams(dimension_semantics=sem, vmem_limit_bytes=VMEM_LIMIT_BYTES)


def _resident(shape, index_map):
    return pl.BlockSpec(shape, index_map, pipeline_mode=pl.Buffered(1))


def _dot(a, b):
    return jnp.dot(a, b, preferred_element_type=F32)


def _dot_nt(a, b):
    return lax.dot_general(a, b, (((1,), (1,)), ((), ())), preferred_element_type=F32)


def _rms(x, g):
    return x * lax.rsqrt(jnp.mean(x * x, axis=-1, keepdims=True) + EPS) * g


def _layernorm(x, g, b):
    mu = jnp.mean(x, axis=-1, keepdims=True)
    xc = x - mu
    return xc * lax.rsqrt(jnp.mean(xc * xc, axis=-1, keepdims=True) + EPS) * g + b


def _staggered_row_specs(tm, d, n_rows, n_inner):
    slabs = min(ROW_SLABS, n_inner)
    width = d // slabs

    def spec(q):
        first = n_inner - (slabs - 1 - q)
        return pl.BlockSpec(
            (tm, width), lambda i, j: (jnp.minimum(i + jnp.where(j >= first, 1, 0), n_rows - 1), q))

    return [spec(q) for q in range(slabs)]


def _rms_slabs_to(h_ref, x_refs, g_ref):
    d = h_ref.shape[1]
    width = d // len(x_refs)
    ssq = None
    for x_ref in x_refs:
        x = x_ref[...]
        part = jnp.sum(x * x, axis=-1, keepdims=True)
        ssq = part if ssq is None else ssq + part
    inv = lax.rsqrt(ssq * np.float32(1.0 / d) + EPS)
    for q, x_ref in enumerate(x_refs):
        cols = slice(q * width, (q + 1) * width)
        h_ref[:, cols] = (x_ref[...] * inv * g_ref[:, cols]).astype(BF16)


def _gelu(x):
    return 0.5 * x * (1.0 + lax.erf(x * np.float32(math.sqrt(0.5))))


def _cast_kernel(w_ref, o_ref):
    o_ref[...] = w_ref[...].astype(BF16)


def _cast_layers(w_layers, ncols, *, tr):
    depth, rows, _ = w_layers.shape
    return pl.pallas_call(
        _cast_kernel,
        grid=(depth, rows // tr),
        in_specs=[pl.BlockSpec((None, tr, ncols), lambda l, i: (l, i, 0))],
        out_specs=pl.BlockSpec((None, tr, ncols), lambda l, i: (l, i, 0)),
        out_shape=jax.ShapeDtypeStruct((depth, rows, ncols), BF16),
        compiler_params=_params("parallel", "parallel"),
        name="cast_weights",
    )(w_layers)


def _ffn_up_kernel(*refs):
    *x_refs, g_ref, wg_ref, wu_ref, a_ref, h_ref = refs

    @pl.when(pl.program_id(1) == 0)
    def _():
        _rms_slabs_to(h_ref, x_refs, g_ref)

    h = h_ref[...]
    gate = _dot(h, wg_ref[...].astype(BF16))
    up = _dot(h, wu_ref[...].astype(BF16))
    a_ref[...] = (0.5 * gate * jax.nn.sigmoid(gate) * up).astype(BF16)


def _ffn_up(x2, g, w13_layers, layer, *, tm, tf):
    m, d = x2.shape
    dff = w13_layers.shape[2] // 2
    nf = dff // tf
    x_specs = _staggered_row_specs(tm, d, m // tm, nf)
    return pl.pallas_call(
        _ffn_up_kernel,
        grid=(m // tm, nf),
        in_specs=x_specs + [
            pl.BlockSpec((1, d), lambda i, j: (0, 0)),
            pl.BlockSpec((None, d, tf), lambda i, j: (layer, 0, j)),
            pl.BlockSpec((None, d, tf), lambda i, j: (layer, 0, j + nf)),
        ],
        out_specs=pl.BlockSpec((tm, tf), lambda i, j: (i, j)),
        out_shape=jax.ShapeDtypeStruct((m, dff), BF16),
        scratch_shapes=[pltpu.VMEM((tm, d), BF16)],
        compiler_params=_params("parallel", "arbitrary"),
        name="ffn_up",
    )(*[x2] * len(x_specs), g, w13_layers, w13_layers)


def _residual_matmul_kernel(a_ref, w_ref, x_ref, o_ref):
    o_ref[...] = x_ref[...] + _dot(a_ref[...], w_ref[...])


def _residual_matmul(a, w_layers, layer, x2, *, tm, tn, name):
    m, kdim = a.shape
    d = w_layers.shape[2]
    tile = pl.BlockSpec((tm, tn), lambda i, j: (i, j))
    w_spec = _resident if tn == d else pl.BlockSpec
    return pl.pallas_call(
        _residual_matmul_kernel,
        grid=(m // tm, d // tn),
        in_specs=[pl.BlockSpec((tm, kdim), lambda i, j: (i, 0)),
                  w_spec((None, kdim, tn), lambda i, j: (layer, 0, j)), tile],
        out_specs=tile,
        out_shape=jax.ShapeDtypeStruct((m, d), F32),
        compiler_params=_params("parallel", "arbitrary"),
        name=name,
    )(a, w_layers, x2)


def _rope_table_kernel(pos_ref, inv_c_ref, sgn_c_ref, inv_d_ref, sgn_d_ref,
                       cos_c_ref, sin_c_ref, cos_d_ref, sin_d_ref):
    p = pos_ref[...].astype(F32)
    ang_c = p * inv_c_ref[...]
    cos_c_ref[...] = jnp.cos(ang_c)
    sin_c_ref[...] = jnp.sin(ang_c) * sgn_c_ref[...]
    ang_d = p * inv_d_ref[...]
    cos_d_ref[...] = jnp.cos(ang_d)
    sin_d_ref[...] = jnp.sin(ang_d) * sgn_d_ref[...]


def _rope_lane_consts(width):
    rot = width // ROPE_FRACTION
    half = rot // 2
    inv_freq = 1.0 / (ROPE_THETA ** (jnp.arange(half, dtype=F32) * 2.0 / rot))
    lane = np.arange(LANES) % width
    in_rot = lane < rot
    inv = jnp.where(in_rot, inv_freq[lane % half], 0.0).astype(F32)
    sgn = np.where(lane < half, -1.0, np.where(in_rot, 1.0, 0.0)).astype(np.float32)
    return inv.reshape(1, LANES), jnp.asarray(sgn).reshape(1, LANES)


def _rope_tables(positions, *, ts):
    m = positions.size
    inv_c, sgn_c = _rope_lane_consts(HEAD_DIM)
    inv_d, sgn_d = _rope_lane_consts(DIFF_QK_DIM)
    lane_spec = pl.BlockSpec((1, LANES), lambda i: (0, 0))
    tab_spec = pl.BlockSpec((ts, LANES), lambda i: (i, 0))
    tab = jax.ShapeDtypeStruct((m, LANES), F32)
    return pl.pallas_call(
        _rope_table_kernel,
        grid=(m // ts,),
        in_specs=[pl.BlockSpec((ts, 1), lambda i: (i, 0)), lane_spec, lane_spec, lane_spec, lane_spec],
        out_specs=[tab_spec] * 4,
        out_shape=[tab] * 4,
        compiler_params=_params("parallel"),
        name="rope_tables",
    )(positions.reshape(m, 1), inv_c, sgn_c, inv_d, sgn_d)


def _rope(x, cos, sin, half):
    lane = lax.broadcasted_iota(jnp.int32, x.shape, 1)
    lower = (lane % (2 * half)) < half
    partner = jnp.where(lower, pltpu.roll(x, LANES - half, 1), pltpu.roll(x, half, 1))
    return x * cos + partner * sin


def _segment_rms(x, g):
    lane = lax.broadcasted_iota(jnp.int32, x.shape, 1)
    lower = lane < DIFF_QK_DIM
    sq = x * x
    lo = jnp.sum(jnp.where(lower, sq, 0.0), axis=-1, keepdims=True)
    hi = jnp.sum(jnp.where(lower, 0.0, sq), axis=-1, keepdims=True)
    ms = jnp.where(lower, lo, hi) * np.float32(1.0 / DIFF_QK_DIM)
    return x * lax.rsqrt(ms + EPS) * g


def _mix_proj_kernel(x_ref, g_ref, w_ref, sg_ref, sb_ref, ws_ref, bias_ref, cqg_ref, ckg_ref, cos_c_ref, sin_c_ref,
                     dqg_ref, dkg_ref, cos_d_ref, sin_d_ref,
                     z_ref, yb_ref, dq_ref, dk_ref, dv_ref, *c_refs_and_scratch, bw, tm, dils):
    c_refs, stage_ref = c_refs_and_scratch[:-1], c_refs_and_scratch[-1]
    h = _rms(x_ref[...], g_ref[...]).astype(BF16)
    heads = bw // HEAD_DIM

    def seg(c0, n):
        return _dot(h, w_ref[:, c0 * bw:(c0 + n) * bw])

    r = seg(2, 2)
    u = _gelu(r[:, :bw])
    v = _layernorm(_gelu(r[:, bw:]), sg_ref[...], sb_ref[...]).astype(BF16)
    row = lax.broadcasted_iota(jnp.int32, (SGU_CHUNK, SGU_CHUNK), 0)
    col = lax.broadcasted_iota(jnp.int32, (SGU_CHUNK, SGU_CHUNK), 1)
    for g in range(heads):
        lanes = slice(g * HEAD_DIM, (g + 1) * HEAD_DIM)
        w = jnp.where(row >= col, ws_ref[g], 0.0).astype(BF16)
        for c in range(tm // SGU_CHUNK):
            rws = slice(c * SGU_CHUNK, (c + 1) * SGU_CHUNK)
            mixed = _dot(w, v[rws, lanes]) + bias_ref[:, lanes]
            yb_ref[rws, lanes] = (u[rws, lanes] * mixed).astype(BF16)

    r = seg(4, 3)
    cos, sin = cos_c_ref[...], sin_c_ref[...]
    half = HEAD_DIM // ROPE_FRACTION // 2
    for part in range(3):
        outs = c_refs[part * len(dils):(part + 1) * len(dils)]
        for hd in range(heads):
            lanes = slice(hd * HEAD_DIM, (hd + 1) * HEAD_DIM)
            t = r[:, part * bw + hd * HEAD_DIM:part * bw + (hd + 1) * HEAD_DIM]
            if part < 2:
                t = _rope(_rms(t, (cqg_ref, ckg_ref)[part][...]), cos, sin, half)
            stage_ref[hd] = t
            for dil, o_ref in zip(dils, outs):
                if dil == 1:
                    o_ref[:, lanes] = t.astype(BF16)
        for dil, o_ref in zip(dils, outs):
            if dil == 1:
                continue
            for res in range(dil):
                for hd in range(heads):
                    o_ref[:, res * bw + hd * HEAD_DIM:res * bw + (hd + 1) * HEAD_DIM] = (
                        stage_ref[hd, pl.ds(res, tm // dil, stride=dil), :].astype(BF16))

    r = seg(7, 3)
    cos, sin = cos_d_ref[...], sin_d_ref[...]
    half = DIFF_QK_DIM // ROPE_FRACTION // 2
    scale = np.float32(DIFF_QK_DIM ** -0.5)
    for hd in range(heads):
        lanes = slice(hd * HEAD_DIM, (hd + 1) * HEAD_DIM)
        klanes = slice(bw + hd * HEAD_DIM, bw + (hd + 1) * HEAD_DIM)
        q = _rope(_segment_rms(r[:, lanes], dqg_ref[...]), cos, sin, half)
        dq_ref[:, lanes] = (q * scale).astype(BF16)
        dk_ref[:, lanes] = _rope(_segment_rms(r[:, klanes], dkg_ref[...]), cos, sin, half).astype(BF16)
    dv_ref[...] = r[:, 2 * bw:].astype(BF16)

    r = seg(0, 2)
    z_ref[...] = r[:, :bw] * jax.nn.sigmoid(r[:, bw:])


def _mix_proj(x2, g, w_mix_layers, layer, sgu_g, sgu_b, w_s, b_s, cq_g, ck_g, cos_c, sin_c, dq_g, dk_g, cos_d, sin_d,
              *, bw, tm):
    m, d = x2.shape
    groups = w_s.shape[0]
    heads = bw // HEAD_DIM
    dils = tuple(dil for _, dil in DIL_PATTERNS)
    assert groups == heads and w_mix_layers.shape[2] == 10 * bw
    assert all(tm % (dil * 2 * SUBLANES) == 0 for dil in dils)
    bias = jnp.repeat(b_s.T, HEAD_DIM, axis=1)
    vec = pl.BlockSpec((1, bw), lambda i: (0, 0))
    lane_spec = pl.BlockSpec((1, LANES), lambda i: (0, 0))
    tab_spec = pl.BlockSpec((tm, LANES), lambda i: (i, 0))
    out_spec = pl.BlockSpec((tm, bw), lambda i: (i, 0))
    out_bf = jax.ShapeDtypeStruct((m, bw), BF16)
    view_specs = [pl.BlockSpec((tm // dil, dil * bw), lambda i: (i, 0)) for dil in dils] * 3
    view_shapes = [jax.ShapeDtypeStruct((m // dil, dil * bw), BF16) for dil in dils] * 3
    outs = pl.pallas_call(
        functools.partial(_mix_proj_kernel, bw=bw, tm=tm, dils=dils),
        grid=(m // tm,),
        in_specs=[
            pl.BlockSpec((tm, d), lambda i: (i, 0)),
            pl.BlockSpec((1, d), lambda i: (0, 0)),
            _resident((None, d, 10 * bw), lambda i: (layer, 0, 0)),
            vec, vec,
            pl.BlockSpec((groups, SGU_CHUNK, SGU_CHUNK), lambda i: (0, 0, 0)),
            pl.BlockSpec((SGU_CHUNK, bw), lambda i: (0, 0)),
            lane_spec, lane_spec, tab_spec, tab_spec,
            lane_spec, lane_spec, tab_spec, tab_spec,
        ],
        out_specs=[out_spec] * 5 + view_specs,
        out_shape=[jax.ShapeDtypeStruct((m, bw), F32)] + [out_bf] * 4 + view_shapes,
        scratch_shapes=[pltpu.VMEM((heads, tm, HEAD_DIM), F32)],
        compiler_params=_params("parallel"),
        name="mix_proj",
    )(x2, g, w_mix_layers, sgu_g, sgu_b, w_s, bias, cq_g, ck_g, cos_c, sin_c,
      jnp.tile(dq_g, (1, 2)), jnp.tile(dk_g, (1, 2)), cos_d, sin_d)
    n = len(dils)
    z, yb, dq, dk, dv = outs[:5]
    cq, ck, cv = outs[5:5 + n], outs[5 + n:5 + 2 * n], outs[5 + 2 * n:]
    return z, yb, (cq, ck, cv), (dq, dk, dv)


def _gate_proj_kernel(*refs):
    *x_refs, g_ref, w_ref, o_ref, h_ref = refs

    @pl.when(pl.program_id(1) == 0)
    def _():
        _rms_slabs_to(h_ref, x_refs, g_ref)

    o_ref[...] = jax.nn.sigmoid(_dot(h_ref[...], w_ref[...].astype(BF16))).astype(BF16)


def _gate_proj(x2, g, w_in_layers, layer, *, n_mix, tm, tn):
    m, d = x2.shape
    n = w_in_layers.shape[2] - n_mix
    assert n_mix % tn == 0 and n % tn == 0
    first = n_mix // tn
    x_specs = _staggered_row_specs(tm, d, m // tm, n // tn)
    return pl.pallas_call(
        _gate_proj_kernel,
        grid=(m // tm, n // tn),
        in_specs=x_specs + [pl.BlockSpec((1, d), lambda i, j: (0, 0)),
                            pl.BlockSpec((None, d, tn), lambda i, j: (layer, 0, first + j))],
        out_specs=pl.BlockSpec((tm, tn), lambda i, j: (i, j)),
        out_shape=jax.ShapeDtypeStruct((m, n), BF16),
        scratch_shapes=[pltpu.VMEM((tm, d), BF16)],
        compiler_params=_params("parallel", "arbitrary"),
        name="gate_proj",
    )(*[x2] * len(x_specs), g, w_in_layers)


def _conv_kernel(zin_ref, w_ref, b_ref, lg_ref, lb_ref, o_ref, z_ref, *, ts, width, rows):
    ext = ts + CONV_HALO

    @pl.when(pl.program_id(1) == 0)
    def _():
        z_ref[0, 0:CONV_HALO, :] = jnp.zeros((CONV_HALO, z_ref.shape[2]), F32)

    z_ref[0, CONV_HALO:ext, :] = zin_ref[...]
    for s in range(1, SUBLANES):
        z_ref[s, 0:ext - SUBLANES, :] = z_ref[0, s:s + ext - SUBLANES, :]
    ch = z_ref.shape[2]
    for c in range(ts // rows):
        r0 = c * rows
        acc = jnp.broadcast_to(b_ref[...], (rows, ch))
        for k in range(width):
            off = CONV_HALO - (width - 1) + k
            base = r0 + off // SUBLANES * SUBLANES
            acc = acc + w_ref[k:k + 1, :] * z_ref[off % SUBLANES, base:base + rows, :]
        y = _layernorm(acc, lg_ref[...], lb_ref[...])
        o_ref[r0:r0 + rows, :] = (y * jax.nn.sigmoid(y)).astype(o_ref.dtype)
    z_ref[0, 0:CONV_HALO, :] = z_ref[0, ts:ext, :]


def _conv_mixer(z, conv_w, conv_b, ln_g, ln_b, *, batch, seq, ts):
    m, bw = z.shape
    width = conv_w.shape[0]
    assert width - 1 <= CONV_HALO
    ns = seq // ts
    vec = pl.BlockSpec((1, bw), lambda b, s: (0, 0))
    tile = pl.BlockSpec((ts, bw), lambda b, s: (b * ns + s, 0))
    return pl.pallas_call(
        functools.partial(_conv_kernel, ts=ts, width=width, rows=64),
        grid=(batch, ns),
        in_specs=[tile, pl.BlockSpec((width, bw), lambda b, s: (0, 0)), vec, vec, vec],
        out_specs=tile,
        out_shape=jax.ShapeDtypeStruct((m, bw), BF16),
        scratch_shapes=[pltpu.VMEM((SUBLANES, ts + CONV_HALO, bw), F32)],
        compiler_params=_params("arbitrary", "arbitrary"),
        name="conv_mixer",
    )(z, conv_w, conv_b, ln_g, ln_b)


def _dil_band_kernel(*refs, lt, heads, others):
    bw = heads * HEAD_DIM
    if others:
        q_ref, k_ref, v_ref = refs[:3]
        other_refs = refs[3:3 + 2 * len(others)]
        y_ref = refs[3 + 2 * len(others)]
        stage_refs = refs[4 + 2 * len(others):]
        for dil, o_ref, lse_ref, o_st, lse_st in zip(others, other_refs[0::2], other_refs[1::2],
                                                     stage_refs[0::2], stage_refs[1::2]):
            for res in range(dil):
                for hd in range(heads):
                    cols = slice(res * bw + hd * HEAD_DIM, res * bw + (hd + 1) * HEAD_DIM)
                    o_st[hd, pl.ds(res, lt // dil, stride=dil), :] = o_ref[:, cols]
                    lse_st[hd, pl.ds(res, lt // dil, stride=dil), :] = lse_ref[:, cols]
    else:
        q_ref, k_ref, v_ref, o_ref, lse_ref = refs
    i = pl.program_id(2)
    blk = DIL_BLOCK
    scale = np.float32(HEAD_DIM ** -0.5)
    ones = jnp.ones((2 * blk, HEAD_DIM), BF16)
    row = lax.broadcasted_iota(jnp.int32, (blk, 2 * blk), 0)
    col = lax.broadcasted_iota(jnp.int32, (blk, 2 * blk), 1)
    for jb in range(lt // blk):
        rws = slice(jb * blk, (jb + 1) * blk)
        l0 = i * lt + jb * blk
        ks = pl.multiple_of(jnp.maximum(l0 - blk, 0), blk)
        dist = (l0 - ks) + row - col
        valid = (dist >= 0) & (dist <= DIL_SPAN)
        for hd in range(heads):
            lanes = slice(hd * HEAD_DIM, (hd + 1) * HEAD_DIM)
            s = _dot_nt(q_ref[rws, lanes], k_ref[pl.ds(ks, 2 * blk), lanes]) * scale
            s = jnp.where(valid, s, NEG_INF)
            mx = jnp.max(s, axis=-1, keepdims=True)
            p = jnp.exp(s - mx).astype(BF16)
            r = _dot(p, jnp.concatenate([v_ref[pl.ds(ks, 2 * blk), lanes], ones], axis=1))
            den = r[:, HEAD_DIM:]
            o = r[:, :HEAD_DIM] / den
            lse = mx + jnp.log(den)
            if others:
                o_all = [o] + [st[hd, rws, :] for st in stage_refs[0::2]]
                lse_all = [lse] + [st[hd, rws, :] for st in stage_refs[1::2]]
                top = functools.reduce(jnp.maximum, lse_all)
                wts = [jnp.exp(t - top) for t in lse_all]
                num = wts[0] * o_all[0]
                tot = wts[0]
                for wg, og in zip(wts[1:], o_all[1:]):
                    num = num + wg * og
                    tot = tot + wg
                y_ref[rws, lanes] = (num / tot).astype(BF16)
            else:
                o_ref[rws, lanes] = o
                lse_ref[rws, lanes] = lse


def _dil_band(q, k, v, others, *, batch, seq, dil, lt, classes=1):
    rows, cols = q.shape
    bw = cols // dil * classes
    heads = bw // HEAD_DIM
    ln = seq // dil
    lt = min(lt, ln)
    nl = ln // lt
    assert ln % lt == 0 and lt % DIL_BLOCK == 0 and ln >= 2 * DIL_BLOCK and dil % classes == 0
    tile = pl.BlockSpec((lt, bw), lambda b, r, i: (b * nl + i, r))
    whole = pl.BlockSpec((ln, bw), lambda b, r, i: (b, r))
    if others:
        assert dil == 1 and all(lt % (od * SUBLANES) == 0 for od, _, _ in others)
        extra = [a for _, o, lse in others for a in (o, lse)]
        extra_specs = [pl.BlockSpec((lt // od, od * bw), lambda b, r, i: (b * nl + i, 0))
                       for od, _, _ in others for _ in range(2)]
        out_specs, out_shape = tile, jax.ShapeDtypeStruct((rows, cols), BF16)
        scratch = [pltpu.VMEM((heads, lt, HEAD_DIM), F32)] * len(extra)
    else:
        extra, extra_specs, scratch = [], [], []
        out_specs = [tile, tile]
        out_shape = [jax.ShapeDtypeStruct((rows, cols), F32)] * 2
    return pl.pallas_call(
        functools.partial(_dil_band_kernel, lt=lt, heads=heads, others=tuple(od for od, _, _ in others)),
        grid=(batch, dil // classes, nl),
        in_specs=[tile, whole, whole] + extra_specs,
        out_specs=out_specs,
        out_shape=out_shape,
        scratch_shapes=scratch,
        compiler_params=_params("parallel", "parallel", "arbitrary"),
        name=f"dil_band_{dil}",
    )(q, k, v, *extra)


def _dil_attn(cq, ck, cv, *, batch, seq):
    dils = [d for _, d in DIL_PATTERNS]
    assert all(w // d == DIL_SPAN for w, d in DIL_PATTERNS) and dils[0] == 1
    others = []
    heads = cq[0].shape[1] // HEAD_DIM
    for idx in range(1, len(dils)):
        dil = dils[idx]
        lt = min(seq // dil, 8 * DIL_BLOCK)
        classes = math.gcd(dil, max(1, 32 // (lt // DIL_BLOCK * heads)))
        o, lse = _dil_band(cq[idx], ck[idx], cv[idx], [], batch=batch, seq=seq, dil=dil, lt=lt, classes=classes)
        others.append((dil, o, lse))
    return _dil_band(cq[0], ck[0], cv[0], others, batch=batch, seq=seq, dil=1, lt=512)


def _diff_attn_kernel(q_ref, k_ref, v_ref, lq1_ref, lk1_ref, lq2_ref, lk2_ref, sub_ref, o_ref,
                      s_ref, m_ref, l_ref, acc_ref, *, tq, hps, lam_init):
    i = pl.program_id(2)
    tiles = tq // LANES
    head_lanes = [slice(hd * HEAD_DIM, (hd + 1) * HEAD_DIM) for hd in range(hps)]

    def stacked_q(lanes):
        q = q_ref[:, lanes]
        lane = lax.broadcasted_iota(jnp.int32, q.shape, 1)
        zero = jnp.zeros_like(q)
        return jnp.concatenate([jnp.where(lane < DIFF_QK_DIM, q, zero), jnp.where(lane < DIFF_QK_DIM, zero, q)],
                               axis=0)

    qq = [stacked_q(lanes) for lanes in head_lanes]

    def lane_fold(x, op):
        r = x[:, 0:LANES]
        for t in range(1, tiles):
            r = op(r, x[:, t * LANES:(t + 1) * LANES])
        return r

    m_ref[...] = jnp.full(m_ref.shape, NEG_INF, F32)

    def scores(j, masked):
        k0 = pl.multiple_of(j * tq, tq)
        for hd, lanes in enumerate(head_lanes):
            s = _dot_nt(qq[hd], k_ref[pl.ds(k0, tq), lanes]) * np.float32(math.log2(math.e))
            if masked:
                row = lax.broadcasted_iota(jnp.int32, s.shape, 0)
                col = lax.broadcasted_iota(jnp.int32, s.shape, 1)
                row = jnp.where(row >= tq, row - tq, row)
                s = jnp.where(col <= row, s, NEG_INF)
            s_ref[hd, j] = s
            m_ref[hd] = jnp.maximum(m_ref[hd], lane_fold(s, jnp.maximum))

    def accumulate(j):
        k0 = pl.multiple_of(j * tq, tq)
        for hd, lanes in enumerate(head_lanes):
            s = s_ref[hd, j]
            mx = m_ref[hd]
            ps = [jnp.exp2(s[:, t * LANES:(t + 1) * LANES] - mx) for t in range(tiles)]
            part = ps[0]
            for t in range(1, tiles):
                part = part + ps[t]
            l_ref[hd] += part
            p = jnp.concatenate(ps, axis=1).astype(BF16)
            acc_ref[hd] += _dot(p, v_ref[pl.ds(k0, tq), lanes])

    def in_pairs(fn, count):
        def body(jp, carry):
            fn(2 * jp)
            fn(2 * jp + 1)
            return carry

        lax.fori_loop(0, count // 2, body, 0)

        @pl.when(count % 2 == 1)
        def _():
            fn(count - 1)

    in_pairs(lambda j: scores(j, False), i)
    scores(i, True)
    for hd in range(hps):
        m_ref[hd] = jnp.broadcast_to(jnp.max(m_ref[hd], axis=-1, keepdims=True), m_ref.shape[1:])

    l_ref[...] = jnp.zeros(l_ref.shape, F32)
    acc_ref[...] = jnp.zeros(acc_ref.shape, F32)
    in_pairs(accumulate, i + 1)

    lam = (jnp.exp(jnp.sum(lq1_ref[...] * lk1_ref[...], axis=-1, keepdims=True))
           - jnp.exp(jnp.sum(lq2_ref[...] * lk2_ref[...], axis=-1, keepdims=True)) + np.float32(lam_init))
    for hd, lanes in enumerate(head_lanes):
        o = acc_ref[hd] / jnp.sum(l_ref[hd], axis=-1, keepdims=True)
        od = o[0:tq, :] - lam * o[tq:2 * tq, :]
        o_ref[:, lanes] = (_rms(od, sub_ref[...]) * np.float32(1.0 - lam_init)).astype(o_ref.dtype)


def _diff_attn(q, k, v, lq1, lk1, lq2, lk2, subln, *, batch, seq, tq, hps, lam_init):
    m, bw = q.shape
    heads = bw // HEAD_DIM
    nq = seq // tq
    assert heads % hps == 0
    q_spec = pl.BlockSpec((tq, hps * HEAD_DIM), lambda b, h, i: (b * nq + i, h))
    kv_spec = pl.BlockSpec((seq, hps * HEAD_DIM), lambda b, h, i: (b, h))
    lam_spec = pl.BlockSpec((1, DIFF_QK_DIM), lambda b, h, i: (0, 0))
    stat = pltpu.VMEM((hps, 2 * tq, LANES), F32)
    return pl.pallas_call(
        functools.partial(_diff_attn_kernel, tq=tq, hps=hps, lam_init=lam_init),
        grid=(batch, heads // hps, nq),
        in_specs=[q_spec, kv_spec, kv_spec, lam_spec, lam_spec, lam_spec, lam_spec,
                  pl.BlockSpec((1, HEAD_DIM), lambda b, h, i: (0, 0))],
        out_specs=q_spec,
        out_shape=jax.ShapeDtypeStruct((m, bw), BF16),
        scratch_shapes=[pltpu.VMEM((hps, nq, 2 * tq, tq), F32), stat, stat, stat],
        compiler_params=_params("parallel", "parallel", "arbitrary"),
        name="diff_attn",
    )(q, k, v, lq1, lk1, lq2, lk2, subln)


def _merge_kernel(ya_ref, yb_ref, yc_ref, yd_ref, ga_ref, gb_ref, gc_ref, gd_ref, wb_ref, o_ref, *, tn):
    branches = ((ya_ref, ga_ref), (yb_ref, gb_ref), (yc_ref, gc_ref), (yd_ref, gd_ref))
    for c in range(o_ref.shape[1] // tn):
        cols = slice(c * tn, (c + 1) * tn)
        acc = None
        for b, (y_ref, gate_ref) in enumerate(branches):
            term = gate_ref[:, cols].astype(F32) * _dot(y_ref[...], wb_ref[b, :, cols])
            acc = term if acc is None else acc + term
        o_ref[:, cols] = acc.astype(BF16)


def _merge(ys, gates, w_branch_layers, layer, *, tm, tn):
    m, bw = ys[0].shape
    _, nb, _, d = w_branch_layers.shape
    assert nb == len(ys) == 4
    y_spec = pl.BlockSpec((tm, bw), lambda r: (r, 0))
    gate_specs = [pl.BlockSpec((tm, d), lambda r, b=b: (r, b)) for b in range(nb)]
    return pl.pallas_call(
        functools.partial(_merge_kernel, tn=tn),
        grid=(m // tm,),
        in_specs=[y_spec] * nb + gate_specs + [_resident((None, nb, bw, d), lambda r: (layer, 0, 0, 0))],
        out_specs=pl.BlockSpec((tm, d), lambda r: (r, 0)),
        out_shape=jax.ShapeDtypeStruct((m, d), BF16),
        compiler_params=_params("parallel"),
        name="merge_branches",
    )(*ys, gates, gates, gates, gates, w_branch_layers)


def kernel(x, positions, ffn1_norm, ffn1_w13, ffn1_w2, mix_norm, w_in, conv_w, conv_b, conv_ln_g, conv_ln_b, sgu_ln_g, sgu_ln_b, sgu_w, sgu_b, dil_q_norm, dil_k_norm, diff_q_norm, diff_k_norm, diff_lq1, diff_lk1, diff_lq2, diff_lk2, diff_subln, w_branch, w_out, ffn2_norm, ffn2_w13, ffn2_w2):
    batch, seq, d = x.shape
    depth = w_in.shape[0]
    m = batch * seq
    bw = d // 4
    n_mix = 10 * bw
    assert bw % HEAD_DIM == 0 and w_in.shape[2] == n_mix + 4 * d

    tm = min(1024, m)
    dff = ffn1_w2.shape[1]
    tf = 512 if dff % 512 == 0 else 256
    tn = min(512, d)

    def row(v, l):
        return v[l].reshape(1, -1)

    def ffn(x2, g, w13_layers, w2_layers, l):
        act = _ffn_up(x2, g, w13_layers, l, tm=tm, tf=tf)
        return _residual_matmul(act, w2_layers, l, x2, tm=tm, tn=tn, name="ffn_down")

    x2 = x.reshape(m, d)
    cos_c, sin_c, cos_d, sin_d = _rope_tables(positions, ts=min(512, m))
    ffn1_w2_bf, ffn2_w2_bf = ffn1_w2.astype(BF16), ffn2_w2.astype(BF16)
    w_mix_bf = _cast_layers(w_in, n_mix, tr=min(512, d))
    w_out_bf = _cast_layers(w_out, d, tr=min(1024, d))
    w_branch_bf = _cast_layers(w_branch.reshape(depth, 4 * bw, d), d, tr=min(1024, d)).reshape(w_branch.shape)

    for l in range(depth):
        lam_init = 0.8 - 0.6 * math.exp(-0.3 * l)
        x2 = ffn(x2, row(ffn1_norm, l), ffn1_w13, ffn1_w2_bf, l)

        z, yb, (cq, ck, cv), (dq, dk, dv) = _mix_proj(
            x2, row(mix_norm, l), w_mix_bf, l, row(sgu_ln_g, l), row(sgu_ln_b, l),
            sgu_w[l], sgu_b[l], row(dil_q_norm, l), row(dil_k_norm, l), cos_c, sin_c,
            row(diff_q_norm, l), row(diff_k_norm, l), cos_d, sin_d, bw=bw, tm=min(512, m))
        gates = _gate_proj(x2, row(mix_norm, l), w_in, l, n_mix=n_mix, tm=tm, tn=math.gcd(n_mix, 1024))

        ya = _conv_mixer(z, conv_w[l], row(conv_b, l), row(conv_ln_g, l), row(conv_ln_b, l),
                         batch=batch, seq=seq, ts=min(512, seq))
        yc = _dil_attn(cq, ck, cv, batch=batch, seq=seq)
        yd = _diff_attn(dq, dk, dv, row(diff_lq1, l), row(diff_lk1, l), row(diff_lq2, l), row(diff_lk2, l),
                        row(diff_subln, l), batch=batch, seq=seq, tq=min(512, seq), hps=2, lam_init=lam_init)

        merged = _merge((ya, yb, yc, yd), gates, w_branch_bf, l, tm=min(512, m), tn=tn)
        x2 = _residual_matmul(merged, w_out_bf, l, x2, tm=min(512, m), tn=d, name="out_proj")

        x2 = ffn(x2, row(ffn2_norm, l), ffn2_w13, ffn2_w2_bf, l)
    return x2.reshape(batch, seq, d)
```

```python
import functools
import math

import jax
import jax.numpy as jnp
import numpy as np
from jax import lax
from jax.experimental import pallas as pl
from jax.experimental.pallas import tpu as pltpu

HEAD_DIM = 128
LANES = 128
SUBLANES = 8
DIFF_QK_DIM = HEAD_DIM // 2
SGU_CHUNK = 128
DIL_PATTERNS = ((128, 1), (512, 4), (2048, 16))
DIL_SPAN = 128
DIL_BLOCK = 128
ROPE_THETA = 500000.0
ROPE_FRACTION = 4
EPS = 1e-6
NEG_INF = -1e30
ROW_SLABS = 2
CONV_HALO = 32

VMEM_LIMIT_BYTES = 56 * 1024 * 1024

BF16 = jnp.bfloat16
F32 = jnp.float32


def _params(*sem):
    return pltpu.CompilerParams(dimension_semantics=sem, vmem_limit_bytes=VMEM_LIMIT_BYTES)


def _resident(shape, index_map):
    return pl.BlockSpec(shape, index_map, pipeline_mode=pl.Buffered(1))


def _dot(a, b):
    return jnp.dot(a, b, preferred_element_type=F32)


def _dot_nt(a, b):
    return lax.dot_general(a, b, (((1,), (1,)), ((), ())), preferred_element_type=F32)


def _rms(x, g):
    return x * lax.rsqrt(jnp.mean(x * x, axis=-1, keepdims=True) + EPS) * g


def _layernorm(x, g, b):
    mu = jnp.mean(x, axis=-1, keepdims=True)
    xc = x - mu
    return xc * lax.rsqrt(jnp.mean(xc * xc, axis=-1, keepdims=True) + EPS) * g + b


def _ahead_row_specs(tm, d, n_rows):
    width = d // ROW_SLABS

    def spec(q):
        return pl.BlockSpec((tm, width), lambda i, j: (jnp.minimum(i + jnp.where(j > q, 1, 0), n_rows - 1), q))

    return [spec(q) for q in range(ROW_SLABS)]


def _norm_ahead_plan(n_inner, tm):
    chunks = 1
    while chunks * 2 <= n_inner - ROW_SLABS and tm % (chunks * 2 * SUBLANES) == 0:
        chunks *= 2
    assert n_inner - chunks >= ROW_SLABS, "needs at least ROW_SLABS + 1 inner steps"
    return n_inner - chunks, tm // chunks


def _rms_rows_to(h_ref, rows, x_refs, g_ref):
    d = h_ref.shape[1]
    width = d // len(x_refs)
    ssq = None
    for x_ref in x_refs:
        x = x_ref[rows, :]
        part = jnp.sum(x * x, axis=-1, keepdims=True)
        ssq = part if ssq is None else ssq + part
    inv = lax.rsqrt(ssq * np.float32(1.0 / d) + EPS)
    for q, x_ref in enumerate(x_refs):
        cols = slice(q * width, (q + 1) * width)
        h_ref[rows, cols] = (x_ref[rows, :] * inv * g_ref[:, cols]).astype(BF16)


def _with_norm_ahead(step, x_refs, g_ref, h_refs, *, start, chunk_rows):
    i, j = pl.program_id(0), pl.program_id(1)

    @pl.when((i == 0) & (j == 0))
    def _():
        _rms_rows_to(h_refs[0], slice(None), x_refs, g_ref)

    r0 = pl.multiple_of(jnp.maximum(j - start, 0) * chunk_rows, chunk_rows)
    for parity in range(2):
        @pl.when(i % 2 == parity)
        def _():
            step(h_refs[parity])
            _rms_rows_to(h_refs[1 - parity], pl.ds(r0, chunk_rows), x_refs, g_ref)


def _gelu(x):
    return 0.5 * x * (1.0 + lax.erf(x * np.float32(math.sqrt(0.5))))


def _cast_kernel(w_ref, o_ref):
    o_ref[...] = w_ref[...].astype(BF16)


def _cast_layers(w_layers, ncols, *, tr):
    depth, rows, _ = w_layers.shape
    return pl.pallas_call(
        _cast_kernel,
        grid=(depth, rows // tr),
        in_specs=[pl.BlockSpec((None, tr, ncols), lambda l, i: (l, i, 0))],
        out_specs=pl.BlockSpec((None, tr, ncols), lambda l, i: (l, i, 0)),
        out_shape=jax.ShapeDtypeStruct((depth, rows, ncols), BF16),
        compiler_params=_params("parallel", "parallel"),
        name="cast_weights",
    )(w_layers)


def _ffn_up_kernel(*refs, start, chunk_rows):
    *x_refs, g_ref, wg_ref, wu_ref, a_ref, h0_ref, h1_ref = refs

    def step(h_ref):
        h = h_ref[...]
        gate = _dot(h, wg_ref[...].astype(BF16))
        up = _dot(h, wu_ref[...].astype(BF16))
        a_ref[...] = (0.5 * gate * jax.nn.sigmoid(gate) * up).astype(BF16)

    _with_norm_ahead(step, x_refs, g_ref, (h0_ref, h1_ref), start=start, chunk_rows=chunk_rows)


def _ffn_up(x2, g, w13_layers, layer, *, tm, tf):
    m, d = x2.shape
    dff = w13_layers.shape[2] // 2
    nf = dff // tf
    x_specs = _ahead_row_specs(tm, d, m // tm)
    start, chunk_rows = _norm_ahead_plan(nf, tm)
    return pl.pallas_call(
        functools.partial(_ffn_up_kernel, start=start, chunk_rows=chunk_rows),
        grid=(m // tm, nf),
        in_specs=x_specs + [
            pl.BlockSpec((1, d), lambda i, j: (0, 0)),
            pl.BlockSpec((None, d, tf), lambda i, j: (layer, 0, j)),
            pl.BlockSpec((None, d, tf), lambda i, j: (layer, 0, j + nf)),
        ],
        out_specs=pl.BlockSpec((tm, tf), lambda i, j: (i, j)),
        out_shape=jax.ShapeDtypeStruct((m, dff), BF16),
        scratch_shapes=[pltpu.VMEM((tm, d), BF16)] * 2,
        compiler_params=_params("arbitrary", "arbitrary"),
        name="ffn_up",
    )(*[x2] * len(x_specs), g, w13_layers, w13_layers)


def _residual_matmul_kernel(a_ref, w_ref, x_ref, o_ref):
    o_ref[...] = x_ref[...] + _dot(a_ref[...], w_ref[...])


def _residual_matmul(a, w_layers, layer, x2, *, tm, tn, name):
    m, kdim = a.shape
    d = w_layers.shape[2]
    tile = pl.BlockSpec((tm, tn), lambda i, j: (i, j))
    w_spec = _resident if tn == d else pl.BlockSpec
    return pl.pallas_call(
        _residual_matmul_kernel,
        grid=(m // tm, d // tn),
        in_specs=[pl.BlockSpec((tm, kdim), lambda i, j: (i, 0)),
                  w_spec((None, kdim, tn), lambda i, j: (layer, 0, j)), tile],
        out_specs=tile,
        out_shape=jax.ShapeDtypeStruct((m, d), F32),
        compiler_params=_params("parallel", "arbitrary"),
        name=name,
    )(a, w_layers, x2)


def _rope_table_kernel(pos_ref, inv_c_ref, sgn_c_ref, inv_d_ref, sgn_d_ref,
                       cos_c_ref, sin_c_ref, cos_d_ref, sin_d_ref):
    p = pos_ref[...].astype(F32)
    ang_c = p * inv_c_ref[...]
    cos_c_ref[...] = jnp.cos(ang_c)
    sin_c_ref[...] = jnp.sin(ang_c) * sgn_c_ref[...]
    ang_d = p * inv_d_ref[...]
    cos_d_ref[...] = jnp.cos(ang_d)
    sin_d_ref[...] = jnp.sin(ang_d) * sgn_d_ref[...]


def _rope_lane_consts(width):
    rot = width // ROPE_FRACTION
    half = rot // 2
    inv_freq = 1.0 / (ROPE_THETA ** (jnp.arange(half, dtype=F32) * 2.0 / rot))
    lane = np.arange(LANES) % width
    in_rot = lane < rot
    inv = jnp.where(in_rot, inv_freq[lane % half], 0.0).astype(F32)
    sgn = np.where(lane < half, -1.0, np.where(in_rot, 1.0, 0.0)).astype(np.float32)
    return inv.reshape(1, LANES), jnp.asarray(sgn).reshape(1, LANES)


def _rope_tables(positions, *, ts):
    m = positions.size
    inv_c, sgn_c = _rope_lane_consts(HEAD_DIM)
    inv_d, sgn_d = _rope_lane_consts(DIFF_QK_DIM)
    lane_spec = pl.BlockSpec((1, LANES), lambda i: (0, 0))
    tab_spec = pl.BlockSpec((ts, LANES), lambda i: (i, 0))
    tab = jax.ShapeDtypeStruct((m, LANES), F32)
    return pl.pallas_call(
        _rope_table_kernel,
        grid=(m // ts,),
        in_specs=[pl.BlockSpec((ts, 1), lambda i: (i, 0)), lane_spec, lane_spec, lane_spec, lane_spec],
        out_specs=[tab_spec] * 4,
        out_shape=[tab] * 4,
        compiler_params=_params("parallel"),
        name="rope_tables",
    )(positions.reshape(m, 1), inv_c, sgn_c, inv_d, sgn_d)


def _rope(x, cos, sin, half):
    lane = lax.broadcasted_iota(jnp.int32, x.shape, 1)
    lower = (lane % (2 * half)) < half
    partner = jnp.where(lower, pltpu.roll(x, LANES - half, 1), pltpu.roll(x, half, 1))
    return x * cos + partner * sin


def _segment_rms(x, g):
    lane = lax.broadcasted_iota(jnp.int32, x.shape, 1)
    lower = lane < DIFF_QK_DIM
    sq = x * x
    lo = jnp.sum(jnp.where(lower, sq, 0.0), axis=-1, keepdims=True)
    hi = jnp.sum(jnp.where(lower, 0.0, sq), axis=-1, keepdims=True)
    ms = jnp.where(lower, lo, hi) * np.float32(1.0 / DIFF_QK_DIM)
    return x * lax.rsqrt(ms + EPS) * g


def _mix_proj_kernel(x_ref, g_ref, w_ref, sg_ref, sb_ref, ws_ref, bias_ref, cqg_ref, ckg_ref, cos_c_ref, sin_c_ref,
                     dqg_ref, dkg_ref, cos_d_ref, sin_d_ref,
                     z_ref, yb_ref, dq_ref, dk_ref, dv_ref, *c_refs_and_scratch, bw, tm, dils):
    c_refs, stage_ref = c_refs_and_scratch[:-1], c_refs_and_scratch[-1]
    h = _rms(x_ref[...], g_ref[...]).astype(BF16)
    heads = bw // HEAD_DIM

    def seg(c0, n):
        return _dot(h, w_ref[:, c0 * bw:(c0 + n) * bw])

    r = seg(2, 2)
    u = _gelu(r[:, :bw])
    v = _layernorm(_gelu(r[:, bw:]), sg_ref[...], sb_ref[...]).astype(BF16)
    row = lax.broadcasted_iota(jnp.int32, (SGU_CHUNK, SGU_CHUNK), 0)
    col = lax.broadcasted_iota(jnp.int32, (SGU_CHUNK, SGU_CHUNK), 1)
    for g in range(heads):
        lanes = slice(g * HEAD_DIM, (g + 1) * HEAD_DIM)
        w = jnp.where(row >= col, ws_ref[g], 0.0).astype(BF16)
        for c in range(tm // SGU_CHUNK):
            rws = slice(c * SGU_CHUNK, (c + 1) * SGU_CHUNK)
            mixed = _dot(w, v[rws, lanes]) + bias_ref[:, lanes]
            yb_ref[rws, lanes] = (u[rws, lanes] * mixed).astype(BF16)

    r = seg(4, 3)
    cos, sin = cos_c_ref[...], sin_c_ref[...]
    half = HEAD_DIM // ROPE_FRACTION // 2
    for part in range(3):
        outs = c_refs[part * len(dils):(part + 1) * len(dils)]
        for hd in range(heads):
            lanes = slice(hd * HEAD_DIM, (hd + 1) * HEAD_DIM)
            t = r[:, part * bw + hd * HEAD_DIM:part * bw + (hd + 1) * HEAD_DIM]
            if part < 2:
                t = _rope(_rms(t, (cqg_ref, ckg_ref)[part][...]), cos, sin, half)
            stage_ref[hd] = t
            for dil, o_ref in zip(dils, outs):
                if dil == 1:
                    o_ref[:, lanes] = t.astype(BF16)
        for dil, o_ref in zip(dils, outs):
            if dil == 1:
                continue
            for res in range(dil):
                for hd in range(heads):
                    o_ref[:, res * bw + hd * HEAD_DIM:res * bw + (hd + 1) * HEAD_DIM] = (
                        stage_ref[hd, pl.ds(res, tm // dil, stride=dil), :].astype(BF16))

    r = seg(7, 3)
    cos, sin = cos_d_ref[...], sin_d_ref[...]
    half = DIFF_QK_DIM // ROPE_FRACTION // 2
    scale = np.float32(DIFF_QK_DIM ** -0.5)
    for hd in range(heads):
        lanes = slice(hd * HEAD_DIM, (hd + 1) * HEAD_DIM)
        klanes = slice(bw + hd * HEAD_DIM, bw + (hd + 1) * HEAD_DIM)
        q = _rope(_segment_rms(r[:, lanes], dqg_ref[...]), cos, sin, half)
        dq_ref[:, lanes] = (q * scale).astype(BF16)
        dk_ref[:, lanes] = _rope(_segment_rms(r[:, klanes], dkg_ref[...]), cos, sin, half).astype(BF16)
    dv_ref[...] = r[:, 2 * bw:].astype(BF16)

    r = seg(0, 2)
    z_ref[...] = r[:, :bw] * jax.nn.sigmoid(r[:, bw:])


def _mix_proj(x2, g, w_mix_layers, layer, sgu_g, sgu_b, w_s, b_s, cq_g, ck_g, cos_c, sin_c, dq_g, dk_g, cos_d, sin_d,
              *, bw, tm):
    m, d = x2.shape
    groups = w_s.shape[0]
    heads = bw // HEAD_DIM
    dils = tuple(dil for _, dil in DIL_PATTERNS)
    assert groups == heads and w_mix_layers.shape[2] == 10 * bw
    assert all(tm % (dil * 2 * SUBLANES) == 0 for dil in dils)
    bias = jnp.repeat(b_s.T, HEAD_DIM, axis=1)
    vec = pl.BlockSpec((1, bw), lambda i: (0, 0))
    lane_spec = pl.BlockSpec((1, LANES), lambda i: (0, 0))
    tab_spec = pl.BlockSpec((tm, LANES), lambda i: (i, 0))
    out_spec = pl.BlockSpec((tm, bw), lambda i: (i, 0))
    out_bf = jax.ShapeDtypeStruct((m, bw), BF16)
    view_specs = [pl.BlockSpec((tm // dil, dil * bw), lambda i: (i, 0)) for dil in dils] * 3
    view_shapes = [jax.ShapeDtypeStruct((m // dil, dil * bw), BF16) for dil in dils] * 3
    outs = pl.pallas_call(
        functools.partial(_mix_proj_kernel, bw=bw, tm=tm, dils=dils),
        grid=(m // tm,),
        in_specs=[
            pl.BlockSpec((tm, d), lambda i: (i, 0)),
            pl.BlockSpec((1, d), lambda i: (0, 0)),
            _resident((None, d, 10 * bw), lambda i: (layer, 0, 0)),
            vec, vec,
            pl.BlockSpec((groups, SGU_CHUNK, SGU_CHUNK), lambda i: (0, 0, 0)),
            pl.BlockSpec((SGU_CHUNK, bw), lambda i: (0, 0)),
            lane_spec, lane_spec, tab_spec, tab_spec,
            lane_spec, lane_spec, tab_spec, tab_spec,
        ],
        out_specs=[out_spec] * 5 + view_specs,
        out_shape=[jax.ShapeDtypeStruct((m, bw), F32)] + [out_bf] * 4 + view_shapes,
        scratch_shapes=[pltpu.VMEM((heads, tm, HEAD_DIM), F32)],
        compiler_params=_params("parallel"),
        name="mix_proj",
    )(x2, g, w_mix_layers, sgu_g, sgu_b, w_s, bias, cq_g, ck_g, cos_c, sin_c,
      jnp.tile(dq_g, (1, 2)), jnp.tile(dk_g, (1, 2)), cos_d, sin_d)
    n = len(dils)
    z, yb, dq, dk, dv = outs[:5]
    cq, ck, cv = outs[5:5 + n], outs[5 + n:5 + 2 * n], outs[5 + 2 * n:]
    return z, yb, (cq, ck, cv), (dq, dk, dv)


def _gate_proj_kernel(*refs, start, chunk_rows):
    *x_refs, g_ref, w_ref, o_ref, h0_ref, h1_ref = refs

    def step(h_ref):
        o_ref[...] = jax.nn.sigmoid(_dot(h_ref[...], w_ref[...].astype(BF16))).astype(BF16)

    _with_norm_ahead(step, x_refs, g_ref, (h0_ref, h1_ref), start=start, chunk_rows=chunk_rows)


def _gate_proj(x2, g, w_in_layers, layer, *, n_mix, tm, tn):
    m, d = x2.shape
    n = w_in_layers.shape[2] - n_mix
    assert n_mix % tn == 0 and n % tn == 0
    first = n_mix // tn
    x_specs = _ahead_row_specs(tm, d, m // tm)
    start, chunk_rows = _norm_ahead_plan(n // tn, tm)
    return pl.pallas_call(
        functools.partial(_gate_proj_kernel, start=start, chunk_rows=chunk_rows),
        grid=(m // tm, n // tn),
        in_specs=x_specs + [pl.BlockSpec((1, d), lambda i, j: (0, 0)),
                            pl.BlockSpec((None, d, tn), lambda i, j: (layer, 0, first + j))],
        out_specs=pl.BlockSpec((tm, tn), lambda i, j: (i, j)),
        out_shape=jax.ShapeDtypeStruct((m, n), BF16),
        scratch_shapes=[pltpu.VMEM((tm, d), BF16)] * 2,
        compiler_params=_params("arbitrary", "arbitrary"),
        name="gate_proj",
    )(*[x2] * len(x_specs), g, w_in_layers)


def _conv_kernel(zin_ref, w_ref, b_ref, lg_ref, lb_ref, o_ref, z_ref, *, ts, width, rows):
    ext = ts + CONV_HALO

    @pl.when(pl.program_id(1) == 0)
    def _():
        z_ref[0, 0:CONV_HALO, :] = jnp.zeros((CONV_HALO, z_ref.shape[2]), F32)

    z_ref[0, CONV_HALO:ext, :] = zin_ref[...]
    for s in range(1, SUBLANES):
        z_ref[s, 0:ext - SUBLANES, :] = z_ref[0, s:s + ext - SUBLANES, :]
    ch = z_ref.shape[2]
    for c in range(ts // rows):
        r0 = c * rows
        acc = jnp.broadcast_to(b_ref[...], (rows, ch))
        for k in range(width):
            off = CONV_HALO - (width - 1) + k
            base = r0 + off // SUBLANES * SUBLANES
            acc = acc + w_ref[k:k + 1, :] * z_ref[off % SUBLANES, base:base + rows, :]
        y = _layernorm(acc, lg_ref[...], lb_ref[...])
        o_ref[r0:r0 + rows, :] = (y * jax.nn.sigmoid(y)).astype(o_ref.dtype)
    z_ref[0, 0:CONV_HALO, :] = z_ref[0, ts:ext, :]


def _conv_mixer(z, conv_w, conv_b, ln_g, ln_b, *, batch, seq, ts):
    m, bw = z.shape
    width = conv_w.shape[0]
    assert width - 1 <= CONV_HALO
    ns = seq // ts
    vec = pl.BlockSpec((1, bw), lambda b, s: (0, 0))
    tile = pl.BlockSpec((ts, bw), lambda b, s: (b * ns + s, 0))
    return pl.pallas_call(
        functools.partial(_conv_kernel, ts=ts, width=width, rows=64),
        grid=(batch, ns),
        in_specs=[tile, pl.BlockSpec((width, bw), lambda b, s: (0, 0)), vec, vec, vec],
        out_specs=tile,
        out_shape=jax.ShapeDtypeStruct((m, bw), BF16),
        scratch_shapes=[pltpu.VMEM((SUBLANES, ts + CONV_HALO, bw), F32)],
        compiler_params=_params("arbitrary", "arbitrary"),
        name="conv_mixer",
    )(z, conv_w, conv_b, ln_g, ln_b)


def _dil_band_kernel(*refs, lt, heads, others):
    bw = heads * HEAD_DIM
    if others:
        q_ref, k_ref, v_ref = refs[:3]
        other_refs = refs[3:3 + 2 * len(others)]
        y_ref = refs[3 + 2 * len(others)]
        stage_refs = refs[4 + 2 * len(others):]
        for dil, o_ref, lse_ref, o_st, lse_st in zip(others, other_refs[0::2], other_refs[1::2],
                                                     stage_refs[0::2], stage_refs[1::2]):
            for res in range(dil):
                for hd in range(heads):
                    cols = slice(res * bw + hd * HEAD_DIM, res * bw + (hd + 1) * HEAD_DIM)
                    o_st[hd, pl.ds(res, lt // dil, stride=dil), :] = o_ref[:, cols]
                    lse_st[hd, pl.ds(res, lt // dil, stride=dil), :] = lse_ref[:, cols]
    else:
        q_ref, k_ref, v_ref, o_ref, lse_ref = refs
    i = pl.program_id(2)
    blk = DIL_BLOCK
    scale = np.float32(HEAD_DIM ** -0.5)
    ones = jnp.ones((2 * blk, HEAD_DIM), BF16)
    row = lax.broadcasted_iota(jnp.int32, (blk, 2 * blk), 0)
    col = lax.broadcasted_iota(jnp.int32, (blk, 2 * blk), 1)
    for jb in range(lt // blk):
        rws = slice(jb * blk, (jb + 1) * blk)
        l0 = i * lt + jb * blk
        ks = pl.multiple_of(jnp.maximum(l0 - blk, 0), blk)
        dist = (l0 - ks) + row - col
        valid = (dist >= 0) & (dist <= DIL_SPAN)
        for hd in range(heads):
            lanes = slice(hd * HEAD_DIM, (hd + 1) * HEAD_DIM)
            s = _dot_nt(q_ref[rws, lanes], k_ref[pl.ds(ks, 2 * blk), lanes]) * scale
            s = jnp.where(valid, s, NEG_INF)
            mx = jnp.max(s, axis=-1, keepdims=True)
            p = jnp.exp(s - mx).astype(BF16)
            r = _dot(p, jnp.concatenate([v_ref[pl.ds(ks, 2 * blk), lanes], ones], axis=1))
            den = r[:, HEAD_DIM:]
            o = r[:, :HEAD_DIM] / den
            lse = mx + jnp.log(den)
            if others:
                o_all = [o] + [st[hd, rws, :] for st in stage_refs[0::2]]
                lse_all = [lse] + [st[hd, rws, :] for st in stage_refs[1::2]]
                top = functools.reduce(jnp.maximum, lse_all)
                wts = [jnp.exp(t - top) for t in lse_all]
                num = wts[0] * o_all[0]
                tot = wts[0]
                for wg, og in zip(wts[1:], o_all[1:]):
                    num = num + wg * og
                    tot = tot + wg
                y_ref[rws, lanes] = (num / tot).astype(BF16)
            else:
                o_ref[rws, lanes] = o
                lse_ref[rws, lanes] = lse


def _dil_band(q, k, v, others, *, batch, seq, dil, lt, classes=1):
    rows, cols = q.shape
    bw = cols // dil * classes
    heads = bw // HEAD_DIM
    ln = seq // dil
    lt = min(lt, ln)
    nl = ln // lt
    assert ln % lt == 0 and lt % DIL_BLOCK == 0 and ln >= 2 * DIL_BLOCK and dil % classes == 0
    tile = pl.BlockSpec((lt, bw), lambda b, r, i: (b * nl + i, r))
    whole = pl.BlockSpec((ln, bw), lambda b, r, i: (b, r))
    if others:
        assert dil == 1 and all(lt % (od * SUBLANES) == 0 for od, _, _ in others)
        extra = [a for _, o, lse in others for a in (o, lse)]
        extra_specs = [pl.BlockSpec((lt // od, od * bw), lambda b, r, i: (b * nl + i, 0))
                       for od, _, _ in others for _ in range(2)]
        out_specs, out_shape = tile, jax.ShapeDtypeStruct((rows, cols), BF16)
        scratch = [pltpu.VMEM((heads, lt, HEAD_DIM), F32)] * len(extra)
    else:
        extra, extra_specs, scratch = [], [], []
        out_specs = [tile, tile]
        out_shape = [jax.ShapeDtypeStruct((rows, cols), F32)] * 2
    return pl.pallas_call(
        functools.partial(_dil_band_kernel, lt=lt, heads=heads, others=tuple(od for od, _, _ in others)),
        grid=(batch, dil // classes, nl),
        in_specs=[tile, whole, whole] + extra_specs,
        out_specs=out_specs,
        out_shape=out_shape,
        scratch_shapes=scratch,
        compiler_params=_params("parallel", "parallel", "arbitrary"),
        name=f"dil_band_{dil}",
    )(q, k, v, *extra)


def _dil_attn(cq, ck, cv, *, batch, seq):
    dils = [d for _, d in DIL_PATTERNS]
    assert all(w // d == DIL_SPAN for w, d in DIL_PATTERNS) and dils[0] == 1
    others = []
    heads = cq[0].shape[1] // HEAD_DIM
    for idx in range(1, len(dils)):
        dil = dils[idx]
        lt = min(seq // dil, 8 * DIL_BLOCK)
        classes = math.gcd(dil, max(1, 32 // (lt // DIL_BLOCK * heads)))
        o, lse = _dil_band(cq[idx], ck[idx], cv[idx], [], batch=batch, seq=seq, dil=dil, lt=lt, classes=classes)
        others.append((dil, o, lse))
    return _dil_band(cq[0], ck[0], cv[0], others, batch=batch, seq=seq, dil=1, lt=512)


def _diff_attn_kernel(q_ref, k_ref, v_ref, lq1_ref, lk1_ref, lq2_ref, lk2_ref, sub_ref, o_ref,
                      s_ref, m_ref, l_ref, acc_ref, *, tq, hps, lam_init):
    i = pl.program_id(2)
    tiles = tq // LANES
    head_lanes = [slice(hd * HEAD_DIM, (hd + 1) * HEAD_DIM) for hd in range(hps)]

    def stacked_q(lanes):
        q = q_ref[:, lanes]
        lane = lax.broadcasted_iota(jnp.int32, q.shape, 1)
        zero = jnp.zeros_like(q)
        return jnp.concatenate([jnp.where(lane < DIFF_QK_DIM, q, zero), jnp.where(lane < DIFF_QK_DIM, zero, q)],
                               axis=0)

    qq = [stacked_q(lanes) for lanes in head_lanes]

    def lane_fold(x, op):
        r = x[:, 0:LANES]
        for t in range(1, tiles):
            r = op(r, x[:, t * LANES:(t + 1) * LANES])
        return r

    m_ref[...] = jnp.full(m_ref.shape, NEG_INF, F32)

    def scores(j, masked):
        k0 = pl.multiple_of(j * tq, tq)
        for hd, lanes in enumerate(head_lanes):
            s = _dot_nt(qq[hd], k_ref[pl.ds(k0, tq), lanes]) * np.float32(math.log2(math.e))
            if masked:
                row = lax.broadcasted_iota(jnp.int32, s.shape, 0)
                col = lax.broadcasted_iota(jnp.int32, s.shape, 1)
                row = jnp.where(row >= tq, row - tq, row)
                s = jnp.where(col <= row, s, NEG_INF)
            s_ref[hd, j] = s
            m_ref[hd] = jnp.maximum(m_ref[hd], lane_fold(s, jnp.maximum))

    def accumulate(j):
        k0 = pl.multiple_of(j * tq, tq)
        for hd, lanes in enumerate(head_lanes):
            s = s_ref[hd, j]
            mx = m_ref[hd]
            ps = [jnp.exp2(s[:, t * LANES:(t + 1) * LANES] - mx) for t in range(tiles)]
            part = ps[0]
            for t in range(1, tiles):
                part = part + ps[t]
            l_ref[hd] += part
            p = jnp.concatenate(ps, axis=1).astype(BF16)
            acc_ref[hd] += _dot(p, v_ref[pl.ds(k0, tq), lanes])

    def in_pairs(fn, count):
        def body(jp, carry):
            fn(2 * jp)
            fn(2 * jp + 1)
            return carry

        lax.fori_loop(0, count // 2, body, 0)

        @pl.when(count % 2 == 1)
        def _():
            fn(count - 1)

    in_pairs(lambda j: scores(j, False), i)
    scores(i, True)
    for hd in range(hps):
        m_ref[hd] = jnp.broadcast_to(jnp.max(m_ref[hd], axis=-1, keepdims=True), m_ref.shape[1:])

    l_ref[...] = jnp.zeros(l_ref.shape, F32)
    acc_ref[...] = jnp.zeros(acc_ref.shape, F32)
    in_pairs(accumulate, i + 1)

    lam = (jnp.exp(jnp.sum(lq1_ref[...] * lk1_ref[...], axis=-1, keepdims=True))
           - jnp.exp(jnp.sum(lq2_ref[...] * lk2_ref[...], axis=-1, keepdims=True)) + np.float32(lam_init))
    for hd, lanes in enumerate(head_lanes):
        o = acc_ref[hd] / jnp.sum(l_ref[hd], axis=-1, keepdims=True)
        od = o[0:tq, :] - lam * o[tq:2 * tq, :]
        o_ref[:, lanes] = (_rms(od, sub_ref[...]) * np.float32(1.0 - lam_init)).astype(o_ref.dtype)


def _diff_attn(q, k, v, lq1, lk1, lq2, lk2, subln, *, batch, seq, tq, hps, lam_init):
    m, bw = q.shape
    heads = bw // HEAD_DIM
    nq = seq // tq
    assert heads % hps == 0
    q_spec = pl.BlockSpec((tq, hps * HEAD_DIM), lambda b, h, i: (b * nq + i, h))
    kv_spec = pl.BlockSpec((seq, hps * HEAD_DIM), lambda b, h, i: (b, h))
    lam_spec = pl.BlockSpec((1, DIFF_QK_DIM), lambda b, h, i: (0, 0))
    stat = pltpu.VMEM((hps, 2 * tq, LANES), F32)
    return pl.pallas_call(
        functools.partial(_diff_attn_kernel, tq=tq, hps=hps, lam_init=lam_init),
        grid=(batch, heads // hps, nq),
        in_specs=[q_spec, kv_spec, kv_spec, lam_spec, lam_spec, lam_spec, lam_spec,
                  pl.BlockSpec((1, HEAD_DIM), lambda b, h, i: (0, 0))],
        out_specs=q_spec,
        out_shape=jax.ShapeDtypeStruct((m, bw), BF16),
        scratch_shapes=[pltpu.VMEM((hps, nq, 2 * tq, tq), F32), stat, stat, stat],
        compiler_params=_params("parallel", "parallel", "arbitrary"),
        name="diff_attn",
    )(q, k, v, lq1, lk1, lq2, lk2, subln)


def _merge_kernel(ya_ref, yb_ref, yc_ref, yd_ref, ga_ref, gb_ref, gc_ref, gd_ref, wb_ref, o_ref, *, tn):
    branches = ((ya_ref, ga_ref), (yb_ref, gb_ref), (yc_ref, gc_ref), (yd_ref, gd_ref))
    for c in range(o_ref.shape[1] // tn):
        cols = slice(c * tn, (c + 1) * tn)
        acc = None
        for b, (y_ref, gate_ref) in enumerate(branches):
            term = gate_ref[:, cols].astype(F32) * _dot(y_ref[...], wb_ref[b, :, cols])
            acc = term if acc is None else acc + term
        o_ref[:, cols] = acc.astype(BF16)


def _merge(ys, gates, w_branch_layers, layer, *, tm, tn):
    m, bw = ys[0].shape
    _, nb, _, d = w_branch_layers.shape
    assert nb == len(ys) == 4
    y_spec = pl.BlockSpec((tm, bw), lambda r: (r, 0))
    gate_specs = [pl.BlockSpec((tm, d), lambda r, b=b: (r, b)) for b in range(nb)]
    return pl.pallas_call(
        functools.partial(_merge_kernel, tn=tn),
        grid=(m // tm,),
        in_specs=[y_spec] * nb + gate_specs + [_resident((None, nb, bw, d), lambda r: (layer, 0, 0, 0))],
        out_specs=pl.BlockSpec((tm, d), lambda r: (r, 0)),
        out_shape=jax.ShapeDtypeStruct((m, d), BF16),
        compiler_params=_params("parallel"),
        name="merge_branches",
    )(*ys, gates, gates, gates, gates, w_branch_layers)


def kernel(x, positions, ffn1_norm, ffn1_w13, ffn1_w2, mix_norm, w_in, conv_w, conv_b, conv_ln_g, conv_ln_b, sgu_ln_g, sgu_ln_b, sgu_w, sgu_b, dil_q_norm, dil_k_norm, diff_q_norm, diff_k_norm, diff_lq1, diff_lk1, diff_lq2, diff_lk2, diff_subln, w_branch, w_out, ffn2_norm, ffn2_w13, ffn2_w2):
    batch, seq, d = x.shape
    depth = w_in.shape[0]
    m = batch * seq
    bw = d // 4
    n_mix = 10 * bw
    assert bw % HEAD_DIM == 0 and w_in.shape[2] == n_mix + 4 * d

    tm = min(1024, m)
    dff = ffn1_w2.shape[1]
    tf = 512 if dff % 512 == 0 else 256
    tn = min(512, d)

    def row(v, l):
        return v[l].reshape(1, -1)

    def ffn(x2, g, w13_layers, w2_layers, l):
        act = _ffn_up(x2, g, w13_layers, l, tm=tm, tf=tf)
        return _residual_matmul(act, w2_layers, l, x2, tm=tm, tn=tn, name="ffn_down")

    x2 = x.reshape(m, d)
    cos_c, sin_c, cos_d, sin_d = _rope_tables(positions, ts=min(512, m))
    ffn1_w2_bf, ffn2_w2_bf = ffn1_w2.astype(BF16), ffn2_w2.astype(BF16)
    w_mix_bf = _cast_layers(w_in, n_mix, tr=min(512, d))
    w_out_bf = _cast_layers(w_out, d, tr=min(1024, d))
    w_branch_bf = _cast_layers(w_branch.reshape(depth, 4 * bw, d), d, tr=min(1024, d)).reshape(w_branch.shape)

    for l in range(depth):
        lam_init = 0.8 - 0.6 * math.exp(-0.3 * l)
        x2 = ffn(x2, row(ffn1_norm, l), ffn1_w13, ffn1_w2_bf, l)

        z, yb, (cq, ck, cv), (dq, dk, dv) = _mix_proj(
            x2, row(mix_norm, l), w_mix_bf, l, row(sgu_ln_g, l), row(sgu_ln_b, l),
            sgu_w[l], sgu_b[l], row(dil_q_norm, l), row(dil_k_norm, l), cos_c, sin_c,
            row(diff_q_norm, l), row(diff_k_norm, l), cos_d, sin_d, bw=bw, tm=min(512, m))
        gates = _gate_proj(x2, row(mix_norm, l), w_in, l, n_mix=n_mix, tm=tm, tn=math.gcd(n_mix, 1024))

        ya = _conv_mixer(z, conv_w[l], row(conv_b, l), row(conv_ln_g, l), row(conv_ln_b, l),
                         batch=batch, seq=seq, ts=min(512, seq))
        yc = _dil_attn(cq, ck, cv, batch=batch, seq=seq)
        yd = _diff_attn(dq, dk, dv, row(diff_lq1, l), row(diff_lk1, l), row(diff_lq2, l), row(diff_lk2, l),
                        row(diff_subln, l), batch=batch, seq=seq, tq=min(512, seq), hps=2, lam_init=lam_init)

        merged = _merge((ya, yb, yc, yd), gates, w_branch_bf, l, tm=min(512, m), tn=tn)
        x2 = _residual_matmul(merged, w_out_bf, l, x2, tm=min(512, m), tn=d, name="out_proj")

        x2 = ffn(x2, row(ffn2_norm, l), ffn2_w13, ffn2_w2_bf, l)
    return x2.reshape(batch, seq, d)
```

```python
import functools
import math

import jax
import jax.numpy as jnp
import numpy as np
from jax import lax
from jax.experimental import pallas as pl
from jax.experimental.pallas import tpu as pltpu

HEAD_DIM = 128
LANES = 128
SUBLANES = 8
DIFF_QK_DIM = HEAD_DIM // 2
SGU_CHUNK = 128
DIL_PATTERNS = ((128, 1), (512, 4), (2048, 16))
DIL_SPAN = 128
DIL_BLOCK = 128
ROPE_THETA = 500000.0
ROPE_FRACTION = 4
EPS = 1e-6
NEG_INF = -1e30
ROW_SLABS = 4
CONV_HALO = 32

VMEM_LIMIT_BYTES = 56 * 1024 * 1024

BF16 = jnp.bfloat16
F32 = jnp.float32


def _params(*sem):
    return pltpu.CompilerParams(dimension_semantics=sem, vmem_limit_bytes=VMEM_LIMIT_BYTES)


def _resident(shape, index_map):
    return pl.BlockSpec(shape, index_map, pipeline_mode=pl.Buffered(1))


def _dot(a, b):
    return jnp.dot(a, b, preferred_element_type=F32)


def _dot_nt(a, b):
    return lax.dot_general(a, b, (((1,), (1,)), ((), ())), preferred_element_type=F32)


def _rms(x, g):
    return x * lax.rsqrt(jnp.mean(x * x, axis=-1, keepdims=True) + EPS) * g


def _layernorm(x, g, b):
    mu = jnp.mean(x, axis=-1, keepdims=True)
    xc = x - mu
    return xc * lax.rsqrt(jnp.mean(xc * xc, axis=-1, keepdims=True) + EPS) * g + b


def _staggered_row_specs(tm, d, n_rows, n_inner):
    slabs = min(ROW_SLABS, n_inner)
    width = d // slabs

    def spec(q):
        first = n_inner - (slabs - 1 - q)
        return pl.BlockSpec(
            (tm, width), lambda i, j: (jnp.minimum(i + jnp.where(j >= first, 1, 0), n_rows - 1), q))

    return [spec(q) for q in range(slabs)]


def _rms_slabs_to(h_ref, x_refs, g_ref):
    d = h_ref.shape[1]
    width = d // len(x_refs)
    ssq = None
    for x_ref in x_refs:
        x = x_ref[...]
        part = jnp.sum(x * x, axis=-1, keepdims=True)
        ssq = part if ssq is None else ssq + part
    inv = lax.rsqrt(ssq * np.float32(1.0 / d) + EPS)
    for q, x_ref in enumerate(x_refs):
        cols = slice(q * width, (q + 1) * width)
        h_ref[:, cols] = (x_ref[...] * inv * g_ref[:, cols]).astype(BF16)


def _gelu(x):
    return 0.5 * x * (1.0 + lax.erf(x * np.float32(math.sqrt(0.5))))


def _cast_kernel(w_ref, o_ref):
    o_ref[...] = w_ref[...].astype(BF16)


def _cast_layers(w_layers, ncols, *, tr):
    depth, rows, _ = w_layers.shape
    return pl.pallas_call(
        _cast_kernel,
        grid=(depth, rows // tr),
        in_specs=[pl.BlockSpec((None, tr, ncols), lambda l, i: (l, i, 0))],
        out_specs=pl.BlockSpec((None, tr, ncols), lambda l, i: (l, i, 0)),
        out_shape=jax.ShapeDtypeStruct((depth, rows, ncols), BF16),
        compiler_params=_params("parallel", "parallel"),
        name="cast_weights",
    )(w_layers)


def _ffn_up_kernel(*refs):
    *x_refs, g_ref, wg_ref, wu_ref, a_ref, h_ref = refs

    @pl.when(pl.program_id(1) == 0)
    def _():
        _rms_slabs_to(h_ref, x_refs, g_ref)

    h = h_ref[...]
    gate = _dot(h, wg_ref[...].astype(BF16))
    up = _dot(h, wu_ref[...].astype(BF16))
    a_ref[...] = (0.5 * gate * jax.nn.sigmoid(gate) * up).astype(BF16)


def _ffn_up(x2, g, w13_layers, layer, *, tm, tf):
    m, d = x2.shape
    dff = w13_layers.shape[2] // 2
    nf = dff // tf
    x_specs = _staggered_row_specs(tm, d, m // tm, nf)
    return pl.pallas_call(
        _ffn_up_kernel,
        grid=(m // tm, nf),
        in_specs=x_specs + [
            pl.BlockSpec((1, d), lambda i, j: (0, 0)),
            pl.BlockSpec((None, d, tf), lambda i, j: (layer, 0, j)),
            pl.BlockSpec((None, d, tf), lambda i, j: (layer, 0, j + nf)),
        ],
        out_specs=pl.BlockSpec((tm, tf), lambda i, j: (i, j)),
        out_shape=jax.ShapeDtypeStruct((m, dff), BF16),
        scratch_shapes=[pltpu.VMEM((tm, d), BF16)],
        compiler_params=_params("parallel", "arbitrary"),
        name="ffn_up",
    )(*[x2] * len(x_specs), g, w13_layers, w13_layers)


def _residual_matmul_kernel(a_ref, w_ref, x_ref, o_ref):
    o_ref[...] = x_ref[...] + _dot(a_ref[...], w_ref[...])


def _residual_matmul(a, w_layers, layer, x2, *, tm, tn, name):
    m, kdim = a.shape
    d = w_layers.shape[2]
    tile = pl.BlockSpec((tm, tn), lambda i, j: (i, j))
    w_spec = _resident if tn == d else pl.BlockSpec
    return pl.pallas_call(
        _residual_matmul_kernel,
        grid=(m // tm, d // tn),
        in_specs=[pl.BlockSpec((tm, kdim), lambda i, j: (i, 0)),
                  w_spec((None, kdim, tn), lambda i, j: (layer, 0, j)), tile],
        out_specs=tile,
        out_shape=jax.ShapeDtypeStruct((m, d), F32),
        compiler_params=_params("parallel", "arbitrary"),
        name=name,
    )(a, w_layers, x2)


def _rope_table_kernel(pos_ref, inv_c_ref, sgn_c_ref, inv_d_ref, sgn_d_ref,
                       cos_c_ref, sin_c_ref, cos_d_ref, sin_d_ref):
    p = pos_ref[...].astype(F32)
    ang_c = p * inv_c_ref[...]
    cos_c_ref[...] = jnp.cos(ang_c)
    sin_c_ref[...] = jnp.sin(ang_c) * sgn_c_ref[...]
    ang_d = p * inv_d_ref[...]
    cos_d_ref[...] = jnp.cos(ang_d)
    sin_d_ref[...] = jnp.sin(ang_d) * sgn_d_ref[...]


def _rope_lane_consts(width):
    rot = width // ROPE_FRACTION
    half = rot // 2
    inv_freq = 1.0 / (ROPE_THETA ** (jnp.arange(half, dtype=F32) * 2.0 / rot))
    lane = np.arange(LANES) % width
    in_rot = lane < rot
    inv = jnp.where(in_rot, inv_freq[lane % half], 0.0).astype(F32)
    sgn = np.where(lane < half, -1.0, np.where(in_rot, 1.0, 0.0)).astype(np.float32)
    return inv.reshape(1, LANES), jnp.asarray(sgn).reshape(1, LANES)


def _rope_tables(positions, *, ts):
    m = positions.size
    inv_c, sgn_c = _rope_lane_consts(HEAD_DIM)
    inv_d, sgn_d = _rope_lane_consts(DIFF_QK_DIM)
    lane_spec = pl.BlockSpec((1, LANES), lambda i: (0, 0))
    tab_spec = pl.BlockSpec((ts, LANES), lambda i: (i, 0))
    tab = jax.ShapeDtypeStruct((m, LANES), F32)
    return pl.pallas_call(
        _rope_table_kernel,
        grid=(m // ts,),
        in_specs=[pl.BlockSpec((ts, 1), lambda i: (i, 0)), lane_spec, lane_spec, lane_spec, lane_spec],
        out_specs=[tab_spec] * 4,
        out_shape=[tab] * 4,
        compiler_params=_params("parallel"),
        name="rope_tables",
    )(positions.reshape(m, 1), inv_c, sgn_c, inv_d, sgn_d)


def _rope(x, cos, sin, half):
    lane = lax.broadcasted_iota(jnp.int32, x.shape, 1)
    lower = (lane % (2 * half)) < half
    partner = jnp.where(lower, pltpu.roll(x, LANES - half, 1), pltpu.roll(x, half, 1))
    return x * cos + partner * sin


def _segment_rms(x, g):
    lane = lax.broadcasted_iota(jnp.int32, x.shape, 1)
    lower = lane < DIFF_QK_DIM
    sq = x * x
    lo = jnp.sum(jnp.where(lower, sq, 0.0), axis=-1, keepdims=True)
    hi = jnp.sum(jnp.where(lower, 0.0, sq), axis=-1, keepdims=True)
    ms = jnp.where(lower, lo, hi) * np.float32(1.0 / DIFF_QK_DIM)
    return x * lax.rsqrt(ms + EPS) * g


def _mix_proj_kernel(x_ref, g_ref, w_ref, sg_ref, sb_ref, ws_ref, bias_ref, cqg_ref, ckg_ref, cos_c_ref, sin_c_ref,
                     dqg_ref, dkg_ref, cos_d_ref, sin_d_ref,
                     z_ref, yb_ref, dq_ref, dk_ref, dv_ref, *c_refs_and_scratch, bw, tm, dils):
    c_refs, stage_ref = c_refs_and_scratch[:-1], c_refs_and_scratch[-1]
    h = _rms(x_ref[...], g_ref[...]).astype(BF16)
    heads = bw // HEAD_DIM

    def seg(c0, n):
        return _dot(h, w_ref[:, c0 * bw:(c0 + n) * bw])

    r = seg(2, 2)
    u = _gelu(r[:, :bw])
    v = _layernorm(_gelu(r[:, bw:]), sg_ref[...], sb_ref[...]).astype(BF16)
    row = lax.broadcasted_iota(jnp.int32, (SGU_CHUNK, SGU_CHUNK), 0)
    col = lax.broadcasted_iota(jnp.int32, (SGU_CHUNK, SGU_CHUNK), 1)
    for g in range(heads):
        lanes = slice(g * HEAD_DIM, (g + 1) * HEAD_DIM)
        w = jnp.where(row >= col, ws_ref[g], 0.0).astype(BF16)
        for c in range(tm // SGU_CHUNK):
            rws = slice(c * SGU_CHUNK, (c + 1) * SGU_CHUNK)
            mixed = _dot(w, v[rws, lanes]) + bias_ref[:, lanes]
            yb_ref[rws, lanes] = (u[rws, lanes] * mixed).astype(BF16)

    r = seg(4, 3)
    cos, sin = cos_c_ref[...], sin_c_ref[...]
    half = HEAD_DIM // ROPE_FRACTION // 2
    for part in range(3):
        outs = c_refs[part * len(dils):(part + 1) * len(dils)]
        for hd in range(heads):
            lanes = slice(hd * HEAD_DIM, (hd + 1) * HEAD_DIM)
            t = r[:, part * bw + hd * HEAD_DIM:part * bw + (hd + 1) * HEAD_DIM]
            if part < 2:
                t = _rope(_rms(t, (cqg_ref, ckg_ref)[part][...]), cos, sin, half)
            stage_ref[hd] = t
            for dil, o_ref in zip(dils, outs):
                if dil == 1:
                    o_ref[:, lanes] = t.astype(BF16)
        for dil, o_ref in zip(dils, outs):
            if dil == 1:
                continue
            for res in range(dil):
                for hd in range(heads):
                    o_ref[:, res * bw + hd * HEAD_DIM:res * bw + (hd + 1) * HEAD_DIM] = (
                        stage_ref[hd, pl.ds(res, tm // dil, stride=dil), :].astype(BF16))

    r = seg(7, 3)
    cos, sin = cos_d_ref[...], sin_d_ref[...]
    half = DIFF_QK_DIM // ROPE_FRACTION // 2
    scale = np.float32(DIFF_QK_DIM ** -0.5)
    for hd in range(heads):
        lanes = slice(hd * HEAD_DIM, (hd + 1) * HEAD_DIM)
        klanes = slice(bw + hd * HEAD_DIM, bw + (hd + 1) * HEAD_DIM)
        q = _rope(_segment_rms(r[:, lanes], dqg_ref[...]), cos, sin, half)
        dq_ref[:, lanes] = (q * scale).astype(BF16)
        dk_ref[:, lanes] = _rope(_segment_rms(r[:, klanes], dkg_ref[...]), cos, sin, half).astype(BF16)
    dv_ref[...] = r[:, 2 * bw:].astype(BF16)

    r = seg(0, 2)
    z_ref[...] = r[:, :bw] * jax.nn.sigmoid(r[:, bw:])


def _mix_proj(x2, g, w_mix_layers, layer, sgu_g, sgu_b, w_s, b_s, cq_g, ck_g, cos_c, sin_c, dq_g, dk_g, cos_d, sin_d,
              *, bw, tm):
    m, d = x2.shape
    groups = w_s.shape[0]
    heads = bw // HEAD_DIM
    dils = tuple(dil for _, dil in DIL_PATTERNS)
    assert groups == heads and w_mix_layers.shape[2] == 10 * bw
    assert all(tm % (dil * 2 * SUBLANES) == 0 for dil in dils)
    bias = jnp.repeat(b_s.T, HEAD_DIM, axis=1)
    vec = pl.BlockSpec((1, bw), lambda i: (0, 0))
    lane_spec = pl.BlockSpec((1, LANES), lambda i: (0, 0))
    tab_spec = pl.BlockSpec((tm, LANES), lambda i: (i, 0))
    out_spec = pl.BlockSpec((tm, bw), lambda i: (i, 0))
    out_bf = jax.ShapeDtypeStruct((m, bw), BF16)
    view_specs = [pl.BlockSpec((tm // dil, dil * bw), lambda i: (i, 0)) for dil in dils] * 3
    view_shapes = [jax.ShapeDtypeStruct((m // dil, dil * bw), BF16) for dil in dils] * 3
    outs = pl.pallas_call(
        functools.partial(_mix_proj_kernel, bw=bw, tm=tm, dils=dils),
        grid=(m // tm,),
        in_specs=[
            pl.BlockSpec((tm, d), lambda i: (i, 0)),
            pl.BlockSpec((1, d), lambda i: (0, 0)),
            _resident((None, d, 10 * bw), lambda i: (layer, 0, 0)),
            vec, vec,
            pl.BlockSpec((groups, SGU_CHUNK, SGU_CHUNK), lambda i: (0, 0, 0)),
            pl.BlockSpec((SGU_CHUNK, bw), lambda i: (0, 0)),
            lane_spec, lane_spec, tab_spec, tab_spec,
            lane_spec, lane_spec, tab_spec, tab_spec,
        ],
        out_specs=[out_spec] * 5 + view_specs,
        out_shape=[jax.ShapeDtypeStruct((m, bw), F32)] + [out_bf] * 4 + view_shapes,
        scratch_shapes=[pltpu.VMEM((heads, tm, HEAD_DIM), F32)],
        compiler_params=_params("parallel"),
        name="mix_proj",
    )(x2, g, w_mix_layers, sgu_g, sgu_b, w_s, bias, cq_g, ck_g, cos_c, sin_c,
      jnp.tile(dq_g, (1, 2)), jnp.tile(dk_g, (1, 2)), cos_d, sin_d)
    n = len(dils)
    z, yb, dq, dk, dv = outs[:5]
    cq, ck, cv = outs[5:5 + n], outs[5 + n:5 + 2 * n], outs[5 + 2 * n:]
    return z, yb, (cq, ck, cv), (dq, dk, dv)


def _gate_proj_kernel(*refs):
    *x_refs, g_ref, w_ref, o_ref, h_ref = refs

    @pl.when(pl.program_id(1) == 0)
    def _():
        _rms_slabs_to(h_ref, x_refs, g_ref)

    o_ref[...] = jax.nn.sigmoid(_dot(h_ref[...], w_ref[...].astype(BF16))).astype(BF16)


def _gate_proj(x2, g, w_in_layers, layer, *, n_mix, tm, tn):
    m, d = x2.shape
    n = w_in_layers.shape[2] - n_mix
    assert n_mix % tn == 0 and n % tn == 0
    first = n_mix // tn
    x_specs = _staggered_row_specs(tm, d, m // tm, n // tn)
    return pl.pallas_call(
        _gate_proj_kernel,
        grid=(m // tm, n // tn),
        in_specs=x_specs + [pl.BlockSpec((1, d), lambda i, j: (0, 0)),
                            pl.BlockSpec((None, d, tn), lambda i, j: (layer, 0, first + j))],
        out_specs=pl.BlockSpec((tm, tn), lambda i, j: (i, j)),
        out_shape=jax.ShapeDtypeStruct((m, n), BF16),
        scratch_shapes=[pltpu.VMEM((tm, d), BF16)],
        compiler_params=_params("parallel", "arbitrary"),
        name="gate_proj",
    )(*[x2] * len(x_specs), g, w_in_layers)


def _conv_kernel(zin_ref, w_ref, b_ref, lg_ref, lb_ref, o_ref, z_ref, *, ts, width, rows):
    ext = ts + CONV_HALO

    @pl.when(pl.program_id(1) == 0)
    def _():
        z_ref[0, 0:CONV_HALO, :] = jnp.zeros((CONV_HALO, z_ref.shape[2]), F32)

    z_ref[0, CONV_HALO:ext, :] = zin_ref[...]
    for s in range(1, SUBLANES):
        z_ref[s, 0:ext - SUBLANES, :] = z_ref[0, s:s + ext - SUBLANES, :]
    ch = z_ref.shape[2]
    for c in range(ts // rows):
        r0 = c * rows
        acc = jnp.broadcast_to(b_ref[...], (rows, ch))
        for k in range(width):
            off = CONV_HALO - (width - 1) + k
            base = r0 + off // SUBLANES * SUBLANES
            acc = acc + w_ref[k:k + 1, :] * z_ref[off % SUBLANES, base:base + rows, :]
        y = _layernorm(acc, lg_ref[...], lb_ref[...])
        o_ref[r0:r0 + rows, :] = (y * jax.nn.sigmoid(y)).astype(o_ref.dtype)
    z_ref[0, 0:CONV_HALO, :] = z_ref[0, ts:ext, :]


def _conv_mixer(z, conv_w, conv_b, ln_g, ln_b, *, batch, seq, ts):
    m, bw = z.shape
    width = conv_w.shape[0]
    assert width - 1 <= CONV_HALO
    ns = seq // ts
    vec = pl.BlockSpec((1, bw), lambda b, s: (0, 0))
    tile = pl.BlockSpec((ts, bw), lambda b, s: (b * ns + s, 0))
    return pl.pallas_call(
        functools.partial(_conv_kernel, ts=ts, width=width, rows=64),
        grid=(batch, ns),
        in_specs=[tile, pl.BlockSpec((width, bw), lambda b, s: (0, 0)), vec, vec, vec],
        out_specs=tile,
        out_shape=jax.ShapeDtypeStruct((m, bw), BF16),
        scratch_shapes=[pltpu.VMEM((SUBLANES, ts + CONV_HALO, bw), F32)],
        compiler_params=_params("arbitrary", "arbitrary"),
        name="conv_mixer",
    )(z, conv_w, conv_b, ln_g, ln_b)


def _dil_band_kernel(*refs, lt, heads, others):
    bw = heads * HEAD_DIM
    if others:
        q_ref, k_ref, v_ref = refs[:3]
        other_refs = refs[3:3 + 2 * len(others)]
        y_ref = refs[3 + 2 * len(others)]
        stage_refs = refs[4 + 2 * len(others):]
        for dil, o_ref, lse_ref, o_st, lse_st in zip(others, other_refs[0::2], other_refs[1::2],
                                                     stage_refs[0::2], stage_refs[1::2]):
            for res in range(dil):
                for hd in range(heads):
                    cols = slice(res * bw + hd * HEAD_DIM, res * bw + (hd + 1) * HEAD_DIM)
                    o_st[hd, pl.ds(res, lt // dil, stride=dil), :] = o_ref[:, cols]
                    lse_st[hd, pl.ds(res, lt // dil, stride=dil), :] = lse_ref[:, cols]
    else:
        q_ref, k_ref, v_ref, o_ref, lse_ref = refs
    i = pl.program_id(2)
    blk = DIL_BLOCK
    scale = np.float32(HEAD_DIM ** -0.5)
    ones = jnp.ones((2 * blk, HEAD_DIM), BF16)
    row = lax.broadcasted_iota(jnp.int32, (blk, 2 * blk), 0)
    col = lax.broadcasted_iota(jnp.int32, (blk, 2 * blk), 1)
    for jb in range(lt // blk):
        rws = slice(jb * blk, (jb + 1) * blk)
        l0 = i * lt + jb * blk
        ks = pl.multiple_of(jnp.maximum(l0 - blk, 0), blk)
        dist = (l0 - ks) + row - col
        valid = (dist >= 0) & (dist <= DIL_SPAN)
        for hd in range(heads):
            lanes = slice(hd * HEAD_DIM, (hd + 1) * HEAD_DIM)
            s = _dot_nt(q_ref[rws, lanes], k_ref[pl.ds(ks, 2 * blk), lanes]) * scale
            s = jnp.where(valid, s, NEG_INF)
            mx = jnp.max(s, axis=-1, keepdims=True)
            p = jnp.exp(s - mx).astype(BF16)
            r = _dot(p, jnp.concatenate([v_ref[pl.ds(ks, 2 * blk), lanes], ones], axis=1))
            den = r[:, HEAD_DIM:]
            o = r[:, :HEAD_DIM] / den
            lse = mx + jnp.log(den)
            if others:
                o_all = [o] + [st[hd, rws, :] for st in stage_refs[0::2]]
                lse_all = [lse] + [st[hd, rws, :] for st in stage_refs[1::2]]
                top = functools.reduce(jnp.maximum, lse_all)
                wts = [jnp.exp(t - top) for t in lse_all]
                num = wts[0] * o_all[0]
                tot = wts[0]
                for wg, og in zip(wts[1:], o_all[1:]):
                    num = num + wg * og
                    tot = tot + wg
                y_ref[rws, lanes] = (num / tot).astype(BF16)
            else:
                o_ref[rws, lanes] = o
                lse_ref[rws, lanes] = lse


def _dil_band(q, k, v, others, *, batch, seq, dil, lt, classes=1):
    rows, cols = q.shape
    bw = cols // dil * classes
    heads = bw // HEAD_DIM
    ln = seq // dil
    lt = min(lt, ln)
    nl = ln // lt
    assert ln % lt == 0 and lt % DIL_BLOCK == 0 and ln >= 2 * DIL_BLOCK and dil % classes == 0
    tile = pl.BlockSpec((lt, bw), lambda b, r, i: (b * nl + i, r))
    whole = pl.BlockSpec((ln, bw), lambda b, r, i: (b, r))
    if others:
        assert dil == 1 and all(lt % (od * SUBLANES) == 0 for od, _, _ in others)
        extra = [a for _, o, lse in others for a in (o, lse)]
        extra_specs = [pl.BlockSpec((lt // od, od * bw), lambda b, r, i: (b * nl + i, 0))
                       for od, _, _ in others for _ in range(2)]
        out_specs, out_shape = tile, jax.ShapeDtypeStruct((rows, cols), BF16)
        scratch = [pltpu.VMEM((heads, lt, HEAD_DIM), F32)] * len(extra)
    else:
        extra, extra_specs, scratch = [], [], []
        out_specs = [tile, tile]
        out_shape = [jax.ShapeDtypeStruct((rows, cols), F32)] * 2
    return pl.pallas_call(
        functools.partial(_dil_band_kernel, lt=lt, heads=heads, others=tuple(od for od, _, _ in others)),
        grid=(batch, dil // classes, nl),
        in_specs=[tile, whole, whole] + extra_specs,
        out_specs=out_specs,
        out_shape=out_shape,
        scratch_shapes=scratch,
        compiler_params=_params("parallel", "parallel", "arbitrary"),
        name=f"dil_band_{dil}",
    )(q, k, v, *extra)


def _dil_attn(cq, ck, cv, *, batch, seq):
    dils = [d for _, d in DIL_PATTERNS]
    assert all(w // d == DIL_SPAN for w, d in DIL_PATTERNS) and dils[0] == 1
    others = []
    heads = cq[0].shape[1] // HEAD_DIM
    for idx in range(1, len(dils)):
        dil = dils[idx]
        lt = min(seq // dil, 8 * DIL_BLOCK)
        classes = math.gcd(dil, max(1, 32 // (lt // DIL_BLOCK * heads)))
        o, lse = _dil_band(cq[idx], ck[idx], cv[idx], [], batch=batch, seq=seq, dil=dil, lt=lt, classes=classes)
        others.append((dil, o, lse))
    return _dil_band(cq[0], ck[0], cv[0], others, batch=batch, seq=seq, dil=1, lt=512)


def _diff_attn_kernel(q_ref, k_ref, v_ref, lq1_ref, lk1_ref, lq2_ref, lk2_ref, sub_ref, o_ref,
                      s_ref, m_ref, l_ref, acc_ref, *, tq, hps, lam_init):
    i = pl.program_id(2)
    tiles = tq // LANES
    head_lanes = [slice(hd * HEAD_DIM, (hd + 1) * HEAD_DIM) for hd in range(hps)]

    def stacked_q(lanes):
        q = q_ref[:, lanes]
        lane = lax.broadcasted_iota(jnp.int32, q.shape, 1)
        zero = jnp.zeros_like(q)
        return jnp.concatenate([jnp.where(lane < DIFF_QK_DIM, q, zero), jnp.where(lane < DIFF_QK_DIM, zero, q)],
                               axis=0)

    qq = [stacked_q(lanes) for lanes in head_lanes]

    def lane_fold(x, op):
        r = x[:, 0:LANES]
        for t in range(1, tiles):
            r = op(r, x[:, t * LANES:(t + 1) * LANES])
        return r

    m_ref[...] = jnp.full(m_ref.shape, NEG_INF, F32)

    def scores(j, masked):
        k0 = pl.multiple_of(j * tq, tq)
        for hd, lanes in enumerate(head_lanes):
            s = _dot_nt(qq[hd], k_ref[pl.ds(k0, tq), lanes]) * np.float32(math.log2(math.e))
            if masked:
                row = lax.broadcasted_iota(jnp.int32, s.shape, 0)
                col = lax.broadcasted_iota(jnp.int32, s.shape, 1)
                row = jnp.where(row >= tq, row - tq, row)
                s = jnp.where(col <= row, s, NEG_INF)
            s_ref[hd, j] = s
            m_ref[hd] = jnp.maximum(m_ref[hd], lane_fold(s, jnp.maximum))

    def accumulate(j):
        k0 = pl.multiple_of(j * tq, tq)
        for hd, lanes in enumerate(head_lanes):
            s = s_ref[hd, j]
            mx = m_ref[hd]
            ps = [jnp.exp2(s[:, t * LANES:(t + 1) * LANES] - mx) for t in range(tiles)]
            part = ps[0]
            for t in range(1, tiles):
                part = part + ps[t]
            l_ref[hd] += part
            p = jnp.concatenate(ps, axis=1).astype(BF16)
            acc_ref[hd] += _dot(p, v_ref[pl.ds(k0, tq), lanes])

    def in_pairs(fn, count):
        def body(jp, carry):
            fn(2 * jp)
            fn(2 * jp + 1)
            return carry

        lax.fori_loop(0, count // 2, body, 0)

        @pl.when(count % 2 == 1)
        def _():
            fn(count - 1)

    in_pairs(lambda j: scores(j, False), i)
    scores(i, True)
    for hd in range(hps):
        m_ref[hd] = jnp.broadcast_to(jnp.max(m_ref[hd], axis=-1, keepdims=True), m_ref.shape[1:])

    l_ref[...] = jnp.zeros(l_ref.shape, F32)
    acc_ref[...] = jnp.zeros(acc_ref.shape, F32)
    in_pairs(accumulate, i + 1)

    lam = (jnp.exp(jnp.sum(lq1_ref[...] * lk1_ref[...], axis=-1, keepdims=True))
           - jnp.exp(jnp.sum(lq2_ref[...] * lk2_ref[...], axis=-1, keepdims=True)) + np.float32(lam_init))
    for hd, lanes in enumerate(head_lanes):
        o = acc_ref[hd] / jnp.sum(l_ref[hd], axis=-1, keepdims=True)
        od = o[0:tq, :] - lam * o[tq:2 * tq, :]
        o_ref[:, lanes] = (_rms(od, sub_ref[...]) * np.float32(1.0 - lam_init)).astype(o_ref.dtype)


def _diff_attn(q, k, v, lq1, lk1, lq2, lk2, subln, *, batch, seq, tq, hps, lam_init):
    m, bw = q.shape
    heads = bw // HEAD_DIM
    nq = seq // tq
    assert heads % hps == 0
    q_spec = pl.BlockSpec((tq, hps * HEAD_DIM), lambda b, h, i: (b * nq + i, h))
    kv_spec = pl.BlockSpec((seq, hps * HEAD_DIM), lambda b, h, i: (b, h))
    lam_spec = pl.BlockSpec((1, DIFF_QK_DIM), lambda b, h, i: (0, 0))
    stat = pltpu.VMEM((hps, 2 * tq, LANES), F32)
    return pl.pallas_call(
        functools.partial(_diff_attn_kernel, tq=tq, hps=hps, lam_init=lam_init),
        grid=(batch, heads // hps, nq),
        in_specs=[q_spec, kv_spec, kv_spec, lam_spec, lam_spec, lam_spec, lam_spec,
                  pl.BlockSpec((1, HEAD_DIM), lambda b, h, i: (0, 0))],
        out_specs=q_spec,
        out_shape=jax.ShapeDtypeStruct((m, bw), BF16),
        scratch_shapes=[pltpu.VMEM((hps, nq, 2 * tq, tq), F32), stat, stat, stat],
        compiler_params=_params("parallel", "parallel", "arbitrary"),
        name="diff_attn",
    )(q, k, v, lq1, lk1, lq2, lk2, subln)


def _merge_kernel(ya_ref, yb_ref, yc_ref, yd_ref, ga_ref, gb_ref, gc_ref, gd_ref, wb_ref, o_ref, *, tn):
    branches = ((ya_ref, ga_ref), (yb_ref, gb_ref), (yc_ref, gc_ref), (yd_ref, gd_ref))
    for c in range(o_ref.shape[1] // tn):
        cols = slice(c * tn, (c + 1) * tn)
        acc = None
        for b, (y_ref, gate_ref) in enumerate(branches):
            term = gate_ref[:, cols].astype(F32) * _dot(y_ref[...], wb_ref[b, :, cols])
            acc = term if acc is None else acc + term
        o_ref[:, cols] = acc.astype(BF16)


def _merge(ys, gates, w_branch_layers, layer, *, tm, tn):
    m, bw = ys[0].shape
    _, nb, _, d = w_branch_layers.shape
    assert nb == len(ys) == 4
    y_spec = pl.BlockSpec((tm, bw), lambda r: (r, 0))
    gate_specs = [pl.BlockSpec((tm, d), lambda r, b=b: (r, b)) for b in range(nb)]
    return pl.pallas_call(
        functools.partial(_merge_kernel, tn=tn),
        grid=(m // tm,),
        in_specs=[y_spec] * nb + gate_specs + [_resident((None, nb, bw, d), lambda r: (layer, 0, 0, 0))],
        out_specs=pl.BlockSpec((tm, d), lambda r: (r, 0)),
        out_shape=jax.ShapeDtypeStruct((m, d), BF16),
        compiler_params=_params("parallel"),
        name="merge_branches",
    )(*ys, gates, gates, gates, gates, w_branch_layers)


def kernel(x, positions, ffn1_norm, ffn1_w13, ffn1_w2, mix_norm, w_in, conv_w, conv_b, conv_ln_g, conv_ln_b, sgu_ln_g, sgu_ln_b, sgu_w, sgu_b, dil_q_norm, dil_k_norm, diff_q_norm, diff_k_norm, diff_lq1, diff_lk1, diff_lq2, diff_lk2, diff_subln, w_branch, w_out, ffn2_norm, ffn2_w13, ffn2_w2):
    batch, seq, d = x.shape
    depth = w_in.shape[0]
    m = batch * seq
    bw = d // 4
    n_mix = 10 * bw
    assert bw % HEAD_DIM == 0 and w_in.shape[2] == n_mix + 4 * d

    tm = min(1024, m)
    dff = ffn1_w2.shape[1]
    tf = 512 if dff % 512 == 0 else 256
    tn = min(512, d)

    def row(v, l):
        return v[l].reshape(1, -1)

    def ffn(x2, g, w13_layers, w2_layers, l):
        act = _ffn_up(x2, g, w13_layers, l, tm=tm, tf=tf)
        return _residual_matmul(act, w2_layers, l, x2, tm=min(512, m), tn=d, name="ffn_down")

    x2 = x.reshape(m, d)
    cos_c, sin_c, cos_d, sin_d = _rope_tables(positions, ts=min(512, m))
    ffn1_w2_bf, ffn2_w2_bf = ffn1_w2.astype(BF16), ffn2_w2.astype(BF16)
    w_mix_bf = _cast_layers(w_in, n_mix, tr=min(512, d))
    w_out_bf = _cast_layers(w_out, d, tr=min(1024, d))
    w_branch_bf = _cast_layers(w_branch.reshape(depth, 4 * bw, d), d, tr=min(1024, d)).reshape(w_branch.shape)

    for l in range(depth):
        lam_init = 0.8 - 0.6 * math.exp(-0.3 * l)
        x2 = ffn(x2, row(ffn1_norm, l), ffn1_w13, ffn1_w2_bf, l)

        z, yb, (cq, ck, cv), (dq, dk, dv) = _mix_proj(
            x2, row(mix_norm, l), w_mix_bf, l, row(sgu_ln_g, l), row(sgu_ln_b, l),
            sgu_w[l], sgu_b[l], row(dil_q_norm, l), row(dil_k_norm, l), cos_c, sin_c,
            row(diff_q_norm, l), row(diff_k_norm, l), cos_d, sin_d, bw=bw, tm=min(512, m))
        gates = _gate_proj(x2, row(mix_norm, l), w_in, l, n_mix=n_mix, tm=tm, tn=math.gcd(n_mix, 1024))

        ya = _conv_mixer(z, conv_w[l], row(conv_b, l), row(conv_ln_g, l), row(conv_ln_b, l),
                         batch=batch, seq=seq, ts=min(1024, seq))
        yc = _dil_attn(cq, ck, cv, batch=batch, seq=seq)
        yd = _diff_attn(dq, dk, dv, row(diff_lq1, l), row(diff_lk1, l), row(diff_lq2, l), row(diff_lk2, l),
                        row(diff_subln, l), batch=batch, seq=seq, tq=min(512, seq), hps=2, lam_init=lam_init)

        merged = _merge((ya, yb, yc, yd), gates, w_branch_bf, l, tm=min(512, m), tn=tn)
        x2 = _residual_matmul(merged, w_out_bf, l, x2, tm=min(512, m), tn=d, name="out_proj")

        x2 = ffn(x2, row(ffn2_norm, l), ffn2_w13, ffn2_w2_bf, l)
    return x2.reshape(batch, seq, d)
```

```python
import functools
import math

import jax
import jax.numpy as jnp
import numpy as np
from jax import lax
from jax.experimental import pallas as pl
from jax.experimental.pallas import tpu as pltpu

HEAD_DIM = 128
LANES = 128
SUBLANES = 8
DIFF_QK_DIM = HEAD_DIM // 2
SGU_CHUNK = 128
DIL_PATTERNS = ((128, 1), (512, 4), (2048, 16))
DIL_SPAN = 128
DIL_BLOCK = 128
ROPE_THETA = 500000.0
ROPE_FRACTION = 4
EPS = 1e-6
NEG_INF = -1e30
CONV_HALO = 32

VMEM_LIMIT_BYTES = 56 * 1024 * 1024

BF16 = jnp.bfloat16
F32 = jnp.float32


def _params(*sem):
    return pltpu.CompilerParams(dimension_semantics=sem, vmem_limit_bytes=VMEM_LIMIT_BYTES)


def _resident(shape, index_map):
    return pl.BlockSpec(shape, index_map, pipeline_mode=pl.Buffered(1))


def _dot(a, b):
    return jnp.dot(a, b, preferred_element_type=F32)


def _dot_nt(a, b):
    return lax.dot_general(a, b, (((1,), (1,)), ((), ())), preferred_element_type=F32)


def _rms(x, g):
    return x * lax.rsqrt(jnp.mean(x * x, axis=-1, keepdims=True) + EPS) * g


def _layernorm(x, g, b):
    mu = jnp.mean(x, axis=-1, keepdims=True)
    xc = x - mu
    return xc * lax.rsqrt(jnp.mean(xc * xc, axis=-1, keepdims=True) + EPS) * g + b


def _gelu(x):
    return 0.5 * x * (1.0 + lax.erf(x * np.float32(math.sqrt(0.5))))


def _cast_kernel(w_ref, o_ref):
    o_ref[...] = w_ref[...].astype(BF16)


def _cast_layers(w_layers, ncols, *, tr):
    depth, rows, _ = w_layers.shape
    return pl.pallas_call(
        _cast_kernel,
        grid=(depth, rows // tr),
        in_specs=[pl.BlockSpec((None, tr, ncols), lambda l, i: (l, i, 0))],
        out_specs=pl.BlockSpec((None, tr, ncols), lambda l, i: (l, i, 0)),
        out_shape=jax.ShapeDtypeStruct((depth, rows, ncols), BF16),
        compiler_params=_params("parallel", "parallel"),
        name="cast_weights",
    )(w_layers)


def _rmsnorm_kernel(x_ref, g_ref, o_ref):
    o_ref[...] = _rms(x_ref[...], g_ref[...]).astype(BF16)


def _rmsnorm(x2, g, *, tm):
    m, d = x2.shape
    return pl.pallas_call(
        _rmsnorm_kernel,
        grid=(m // tm,),
        in_specs=[pl.BlockSpec((tm, d), lambda i: (i, 0)), pl.BlockSpec((1, d), lambda i: (0, 0))],
        out_specs=pl.BlockSpec((tm, d), lambda i: (i, 0)),
        out_shape=jax.ShapeDtypeStruct((m, d), BF16),
        compiler_params=_params("parallel"),
        name="rmsnorm",
    )(x2, g)


def _ffn_up_kernel(h_ref, wg_ref, wu_ref, a_ref):
    h = h_ref[...]
    gate = _dot(h, wg_ref[...].astype(BF16))
    up = _dot(h, wu_ref[...].astype(BF16))
    a_ref[...] = (0.5 * gate * jax.nn.sigmoid(gate) * up).astype(BF16)


def _ffn_up(h, w13_layers, layer, *, tm, tf):
    m, d = h.shape
    dff = w13_layers.shape[2] // 2
    nf = dff // tf
    return pl.pallas_call(
        _ffn_up_kernel,
        grid=(m // tm, nf),
        in_specs=[
            pl.BlockSpec((tm, d), lambda i, j: (i, 0)),
            pl.BlockSpec((None, d, tf), lambda i, j: (layer, 0, j)),
            pl.BlockSpec((None, d, tf), lambda i, j: (layer, 0, j + nf)),
        ],
        out_specs=pl.BlockSpec((tm, tf), lambda i, j: (i, j)),
        out_shape=jax.ShapeDtypeStruct((m, dff), BF16),
        compiler_params=_params("parallel", "arbitrary"),
        name="ffn_up",
    )(h, w13_layers, w13_layers)


def _residual_matmul_kernel(a_ref, w_ref, x_ref, g_ref, o_ref, h_ref):
    out = x_ref[...] + _dot(a_ref[...], w_ref[...])
    o_ref[...] = out
    h_ref[...] = _rms(out, g_ref[...]).astype(BF16)


def _residual_matmul(a, w_layers, layer, x2, g_next, *, tm, name):
    m, kdim = a.shape
    d = w_layers.shape[2]
    tile = pl.BlockSpec((tm, d), lambda i: (i, 0))
    return pl.pallas_call(
        _residual_matmul_kernel,
        grid=(m // tm,),
        in_specs=[pl.BlockSpec((tm, kdim), lambda i: (i, 0)),
                  _resident((None, kdim, d), lambda i: (layer, 0, 0)), tile,
                  pl.BlockSpec((1, d), lambda i: (0, 0))],
        out_specs=[tile, tile],
        out_shape=[jax.ShapeDtypeStruct((m, d), F32), jax.ShapeDtypeStruct((m, d), BF16)],
        compiler_params=_params("parallel"),
        name=name,
    )(a, w_layers, x2, g_next)


def _rope_table_kernel(pos_ref, inv_c_ref, sgn_c_ref, inv_d_ref, sgn_d_ref,
                       cos_c_ref, sin_c_ref, cos_d_ref, sin_d_ref):
    p = pos_ref[...].astype(F32)
    ang_c = p * inv_c_ref[...]
    cos_c_ref[...] = jnp.cos(ang_c)
    sin_c_ref[...] = jnp.sin(ang_c) * sgn_c_ref[...]
    ang_d = p * inv_d_ref[...]
    cos_d_ref[...] = jnp.cos(ang_d)
    sin_d_ref[...] = jnp.sin(ang_d) * sgn_d_ref[...]


def _rope_lane_consts(width):
    rot = width // ROPE_FRACTION
    half = rot // 2
    inv_freq = 1.0 / (ROPE_THETA ** (jnp.arange(half, dtype=F32) * 2.0 / rot))
    lane = np.arange(LANES) % width
    in_rot = lane < rot
    inv = jnp.where(in_rot, inv_freq[lane % half], 0.0).astype(F32)
    sgn = np.where(lane < half, -1.0, np.where(in_rot, 1.0, 0.0)).astype(np.float32)
    return inv.reshape(1, LANES), jnp.asarray(sgn).reshape(1, LANES)


def _rope_tables(positions, *, ts):
    m = positions.size
    inv_c, sgn_c = _rope_lane_consts(HEAD_DIM)
    inv_d, sgn_d = _rope_lane_consts(DIFF_QK_DIM)
    lane_spec = pl.BlockSpec((1, LANES), lambda i: (0, 0))
    tab_spec = pl.BlockSpec((ts, LANES), lambda i: (i, 0))
    tab = jax.ShapeDtypeStruct((m, LANES), F32)
    return pl.pallas_call(
        _rope_table_kernel,
        grid=(m // ts,),
        in_specs=[pl.BlockSpec((ts, 1), lambda i: (i, 0)), lane_spec, lane_spec, lane_spec, lane_spec],
        out_specs=[tab_spec] * 4,
        out_shape=[tab] * 4,
        compiler_params=_params("parallel"),
        name="rope_tables",
    )(positions.reshape(m, 1), inv_c, sgn_c, inv_d, sgn_d)


def _rope(x, cos, sin, half):
    lane = lax.broadcasted_iota(jnp.int32, x.shape, 1)
    lower = (lane % (2 * half)) < half
    partner = jnp.where(lower, pltpu.roll(x, LANES - half, 1), pltpu.roll(x, half, 1))
    return x * cos + partner * sin


def _segment_rms(x, g):
    lane = lax.broadcasted_iota(jnp.int32, x.shape, 1)
    lower = lane < DIFF_QK_DIM
    sq = x * x
    lo = jnp.sum(jnp.where(lower, sq, 0.0), axis=-1, keepdims=True)
    hi = jnp.sum(jnp.where(lower, 0.0, sq), axis=-1, keepdims=True)
    ms = jnp.where(lower, lo, hi) * np.float32(1.0 / DIFF_QK_DIM)
    return x * lax.rsqrt(ms + EPS) * g


def _mix_proj_kernel(h_ref, w_ref, sg_ref, sb_ref, ws_ref, bias_ref, cqg_ref, ckg_ref, cos_c_ref, sin_c_ref,
                     dqg_ref, dkg_ref, cos_d_ref, sin_d_ref,
                     z_ref, yb_ref, dq_ref, dk_ref, dv_ref, *c_refs_and_scratch, bw, tm, dils):
    c_refs, stage_ref = c_refs_and_scratch[:-1], c_refs_and_scratch[-1]
    h = h_ref[...]
    heads = bw // HEAD_DIM

    def seg(c0, n):
        return _dot(h, w_ref[:, c0 * bw:(c0 + n) * bw])

    r = seg(2, 2)
    u = _gelu(r[:, :bw])
    v = _layernorm(_gelu(r[:, bw:]), sg_ref[...], sb_ref[...]).astype(BF16)
    row = lax.broadcasted_iota(jnp.int32, (SGU_CHUNK, SGU_CHUNK), 0)
    col = lax.broadcasted_iota(jnp.int32, (SGU_CHUNK, SGU_CHUNK), 1)
    for g in range(heads):
        lanes = slice(g * HEAD_DIM, (g + 1) * HEAD_DIM)
        w = jnp.where(row >= col, ws_ref[g], 0.0).astype(BF16)
        for c in range(tm // SGU_CHUNK):
            rws = slice(c * SGU_CHUNK, (c + 1) * SGU_CHUNK)
            mixed = _dot(w, v[rws, lanes]) + bias_ref[:, lanes]
            yb_ref[rws, lanes] = (u[rws, lanes] * mixed).astype(BF16)

    r = seg(4, 3)
    cos, sin = cos_c_ref[...], sin_c_ref[...]
    half = HEAD_DIM // ROPE_FRACTION // 2
    for part in range(3):
        outs = c_refs[part * len(dils):(part + 1) * len(dils)]
        for hd in range(heads):
            lanes = slice(hd * HEAD_DIM, (hd + 1) * HEAD_DIM)
            t = r[:, part * bw + hd * HEAD_DIM:part * bw + (hd + 1) * HEAD_DIM]
            if part < 2:
                t = _rope(_rms(t, (cqg_ref, ckg_ref)[part][...]), cos, sin, half)
            stage_ref[hd] = t
            for dil, o_ref in zip(dils, outs):
                if dil == 1:
                    o_ref[:, lanes] = t.astype(BF16)
        for dil, o_ref in zip(dils, outs):
            if dil == 1:
                continue
            for res in range(dil):
                for hd in range(heads):
                    o_ref[:, res * bw + hd * HEAD_DIM:res * bw + (hd + 1) * HEAD_DIM] = (
                        stage_ref[hd, pl.ds(res, tm // dil, stride=dil), :].astype(BF16))

    r = seg(7, 3)
    cos, sin = cos_d_ref[...], sin_d_ref[...]
    half = DIFF_QK_DIM // ROPE_FRACTION // 2
    scale = np.float32(DIFF_QK_DIM ** -0.5)
    for hd in range(heads):
        lanes = slice(hd * HEAD_DIM, (hd + 1) * HEAD_DIM)
        klanes = slice(bw + hd * HEAD_DIM, bw + (hd + 1) * HEAD_DIM)
        q = _rope(_segment_rms(r[:, lanes], dqg_ref[...]), cos, sin, half)
        dq_ref[:, lanes] = (q * scale).astype(BF16)
        dk_ref[:, lanes] = _rope(_segment_rms(r[:, klanes], dkg_ref[...]), cos, sin, half).astype(BF16)
    dv_ref[...] = r[:, 2 * bw:].astype(BF16)

    r = seg(0, 2)
    z_ref[...] = r[:, :bw] * jax.nn.sigmoid(r[:, bw:])


def _mix_proj(h, w_mix_layers, layer, sgu_g, sgu_b, w_s, b_s, cq_g, ck_g, cos_c, sin_c, dq_g, dk_g, cos_d, sin_d,
              *, bw, tm):
    m, d = h.shape
    groups = w_s.shape[0]
    heads = bw // HEAD_DIM
    dils = tuple(dil for _, dil in DIL_PATTERNS)
    assert groups == heads and w_mix_layers.shape[2] == 10 * bw
    assert all(tm % (dil * 2 * SUBLANES) == 0 for dil in dils)
    bias = jnp.repeat(b_s.T, HEAD_DIM, axis=1)
    vec = pl.BlockSpec((1, bw), lambda i: (0, 0))
    lane_spec = pl.BlockSpec((1, LANES), lambda i: (0, 0))
    tab_spec = pl.BlockSpec((tm, LANES), lambda i: (i, 0))
    out_spec = pl.BlockSpec((tm, bw), lambda i: (i, 0))
    out_bf = jax.ShapeDtypeStruct((m, bw), BF16)
    view_specs = [pl.BlockSpec((tm // dil, dil * bw), lambda i: (i, 0)) for dil in dils] * 3
    view_shapes = [jax.ShapeDtypeStruct((m // dil, dil * bw), BF16) for dil in dils] * 3
    outs = pl.pallas_call(
        functools.partial(_mix_proj_kernel, bw=bw, tm=tm, dils=dils),
        grid=(m // tm,),
        in_specs=[
            pl.BlockSpec((tm, d), lambda i: (i, 0)),
            _resident((None, d, 10 * bw), lambda i: (layer, 0, 0)),
            vec, vec,
            pl.BlockSpec((groups, SGU_CHUNK, SGU_CHUNK), lambda i: (0, 0, 0)),
            pl.BlockSpec((SGU_CHUNK, bw), lambda i: (0, 0)),
            lane_spec, lane_spec, tab_spec, tab_spec,
            lane_spec, lane_spec, tab_spec, tab_spec,
        ],
        out_specs=[out_spec] * 5 + view_specs,
        out_shape=[jax.ShapeDtypeStruct((m, bw), F32)] + [out_bf] * 4 + view_shapes,
        scratch_shapes=[pltpu.VMEM((heads, tm, HEAD_DIM), F32)],
        compiler_params=_params("parallel"),
        name="mix_proj",
    )(h, w_mix_layers, sgu_g, sgu_b, w_s, bias, cq_g, ck_g, cos_c, sin_c,
      jnp.tile(dq_g, (1, 2)), jnp.tile(dk_g, (1, 2)), cos_d, sin_d)
    n = len(dils)
    z, yb, dq, dk, dv = outs[:5]
    cq, ck, cv = outs[5:5 + n], outs[5 + n:5 + 2 * n], outs[5 + 2 * n:]
    return z, yb, (cq, ck, cv), (dq, dk, dv)


def _gate_proj_kernel(h_ref, w_ref, o_ref):
    o_ref[...] = jax.nn.sigmoid(_dot(h_ref[...], w_ref[...].astype(BF16))).astype(BF16)


def _gate_proj(h, w_in_layers, layer, *, n_mix, tm, tn):
    m, d = h.shape
    n = w_in_layers.shape[2] - n_mix
    assert n_mix % tn == 0 and n % tn == 0
    first = n_mix // tn
    return pl.pallas_call(
        _gate_proj_kernel,
        grid=(m // tm, n // tn),
        in_specs=[pl.BlockSpec((tm, d), lambda i, j: (i, 0)),
                  pl.BlockSpec((None, d, tn), lambda i, j: (layer, 0, first + j))],
        out_specs=pl.BlockSpec((tm, tn), lambda i, j: (i, j)),
        out_shape=jax.ShapeDtypeStruct((m, n), BF16),
        compiler_params=_params("parallel", "arbitrary"),
        name="gate_proj",
    )(h, w_in_layers)


def _conv_kernel(zin_ref, w_ref, b_ref, lg_ref, lb_ref, o_ref, z_ref, *, ts, width, rows):
    ext = ts + CONV_HALO

    @pl.when(pl.program_id(1) == 0)
    def _():
        z_ref[0, 0:CONV_HALO, :] = jnp.zeros((CONV_HALO, z_ref.shape[2]), F32)

    z_ref[0, CONV_HALO:ext, :] = zin_ref[...]
    for s in range(1, SUBLANES):
        z_ref[s, 0:ext - SUBLANES, :] = z_ref[0, s:s + ext - SUBLANES, :]
    ch = z_ref.shape[2]
    for c in range(ts // rows):
        r0 = c * rows
        acc = jnp.broadcast_to(b_ref[...], (rows, ch))
        for k in range(width):
            off = CONV_HALO - (width - 1) + k
            base = r0 + off // SUBLANES * SUBLANES
            acc = acc + w_ref[k:k + 1, :] * z_ref[off % SUBLANES, base:base + rows, :]
        y = _layernorm(acc, lg_ref[...], lb_ref[...])
        o_ref[r0:r0 + rows, :] = (y * jax.nn.sigmoid(y)).astype(o_ref.dtype)
    z_ref[0, 0:CONV_HALO, :] = z_ref[0, ts:ext, :]


def _conv_mixer(z, conv_w, conv_b, ln_g, ln_b, *, batch, seq, ts):
    m, bw = z.shape
    width = conv_w.shape[0]
    assert width - 1 <= CONV_HALO
    ns = seq // ts
    vec = pl.BlockSpec((1, bw), lambda b, s: (0, 0))
    tile = pl.BlockSpec((ts, bw), lambda b, s: (b * ns + s, 0))
    return pl.pallas_call(
        functools.partial(_conv_kernel, ts=ts, width=width, rows=64),
        grid=(batch, ns),
        in_specs=[tile, pl.BlockSpec((width, bw), lambda b, s: (0, 0)), vec, vec, vec],
        out_specs=tile,
        out_shape=jax.ShapeDtypeStruct((m, bw), BF16),
        scratch_shapes=[pltpu.VMEM((SUBLANES, ts + CONV_HALO, bw), F32)],
        compiler_params=_params("arbitrary", "arbitrary"),
        name="conv_mixer",
    )(z, conv_w, conv_b, ln_g, ln_b)


def _dil_band_kernel(*refs, lt, heads, others):
    bw = heads * HEAD_DIM
    if others:
        q_ref, k_ref, v_ref = refs[:3]
        other_refs = refs[3:3 + 2 * len(others)]
        y_ref = refs[3 + 2 * len(others)]
        stage_refs = refs[4 + 2 * len(others):]
        for dil, o_ref, lse_ref, o_st, lse_st in zip(others, other_refs[0::2], other_refs[1::2],
                                                     stage_refs[0::2], stage_refs[1::2]):
            for res in range(dil):
                for hd in range(heads):
                    cols = slice(res * bw + hd * HEAD_DIM, res * bw + (hd + 1) * HEAD_DIM)
                    o_st[hd, pl.ds(res, lt // dil, stride=dil), :] = o_ref[:, cols]
                    lse_st[hd, pl.ds(res, lt // dil, stride=dil), :] = lse_ref[:, cols]
    else:
        q_ref, k_ref, v_ref, o_ref, lse_ref = refs
    i = pl.program_id(2)
    blk = DIL_BLOCK
    scale = np.float32(HEAD_DIM ** -0.5)
    ones = jnp.ones((2 * blk, HEAD_DIM), BF16)
    row = lax.broadcasted_iota(jnp.int32, (blk, 2 * blk), 0)
    col = lax.broadcasted_iota(jnp.int32, (blk, 2 * blk), 1)
    for jb in range(lt // blk):
        rws = slice(jb * blk, (jb + 1) * blk)
        l0 = i * lt + jb * blk
        ks = pl.multiple_of(jnp.maximum(l0 - blk, 0), blk)
        dist = (l0 - ks) + row - col
        valid = (dist >= 0) & (dist <= DIL_SPAN)
        for hd in range(heads):
            lanes = slice(hd * HEAD_DIM, (hd + 1) * HEAD_DIM)
            s = _dot_nt(q_ref[rws, lanes], k_ref[pl.ds(ks, 2 * blk), lanes]) * scale
            s = jnp.where(valid, s, NEG_INF)
            mx = jnp.max(s, axis=-1, keepdims=True)
            p = jnp.exp(s - mx).astype(BF16)
            r = _dot(p, jnp.concatenate([v_ref[pl.ds(ks, 2 * blk), lanes], ones], axis=1))
            den = r[:, HEAD_DIM:]
            o = r[:, :HEAD_DIM] / den
            lse = mx + jnp.log(den)
            if others:
                o_all = [o] + [st[hd, rws, :] for st in stage_refs[0::2]]
                lse_all = [lse] + [st[hd, rws, :] for st in stage_refs[1::2]]
                top = functools.reduce(jnp.maximum, lse_all)
                wts = [jnp.exp(t - top) for t in lse_all]
                num = wts[0] * o_all[0]
                tot = wts[0]
                for wg, og in zip(wts[1:], o_all[1:]):
                    num = num + wg * og
                    tot = tot + wg
                y_ref[rws, lanes] = (num / tot).astype(BF16)
            else:
                o_ref[rws, lanes] = o
                lse_ref[rws, lanes] = lse


def _dil_band(q, k, v, others, *, batch, seq, dil, lt, classes=1):
    rows, cols = q.shape
    bw = cols // dil * classes
    heads = bw // HEAD_DIM
    ln = seq // dil
    lt = min(lt, ln)
    nl = ln // lt
    assert ln % lt == 0 and lt % DIL_BLOCK == 0 and ln >= 2 * DIL_BLOCK and dil % classes == 0
    tile = pl.BlockSpec((lt, bw), lambda b, r, i: (b * nl + i, r))
    whole = pl.BlockSpec((ln, bw), lambda b, r, i: (b, r))
    if others:
        assert dil == 1 and all(lt % (od * SUBLANES) == 0 for od, _, _ in others)
        extra = [a for _, o, lse in others for a in (o, lse)]
        extra_specs = [pl.BlockSpec((lt // od, od * bw), lambda b, r, i: (b * nl + i, 0))
                       for od, _, _ in others for _ in range(2)]
        out_specs, out_shape = tile, jax.ShapeDtypeStruct((rows, cols), BF16)
        scratch = [pltpu.VMEM((heads, lt, HEAD_DIM), F32)] * len(extra)
    else:
        extra, extra_specs, scratch = [], [], []
        out_specs = [tile, tile]
        out_shape = [jax.ShapeDtypeStruct((rows, cols), F32)] * 2
    return pl.pallas_call(
        functools.partial(_dil_band_kernel, lt=lt, heads=heads, others=tuple(od for od, _, _ in others)),
        grid=(batch, dil // classes, nl),
        in_specs=[tile, whole, whole] + extra_specs,
        out_specs=out_specs,
        out_shape=out_shape,
        scratch_shapes=scratch,
        compiler_params=_params("parallel", "parallel", "arbitrary"),
        name=f"dil_band_{dil}",
    )(q, k, v, *extra)


def _dil_attn(cq, ck, cv, *, batch, seq):
    dils = [d for _, d in DIL_PATTERNS]
    assert all(w // d == DIL_SPAN for w, d in DIL_PATTERNS) and dils[0] == 1
    others = []
    heads = cq[0].shape[1] // HEAD_DIM
    for idx in range(1, len(dils)):
        dil = dils[idx]
        lt = min(seq // dil, 8 * DIL_BLOCK)
        classes = math.gcd(dil, max(1, 32 // (lt // DIL_BLOCK * heads)))
        o, lse = _dil_band(cq[idx], ck[idx], cv[idx], [], batch=batch, seq=seq, dil=dil, lt=lt, classes=classes)
        others.append((dil, o, lse))
    return _dil_band(cq[0], ck[0], cv[0], others, batch=batch, seq=seq, dil=1, lt=512)


def _diff_attn_kernel(q_ref, k_ref, v_ref, lq1_ref, lk1_ref, lq2_ref, lk2_ref, sub_ref, o_ref,
                      s_ref, m_ref, l_ref, acc_ref, *, tq, hps, lam_init):
    i = pl.program_id(2)
    tiles = tq // LANES
    head_lanes = [slice(hd * HEAD_DIM, (hd + 1) * HEAD_DIM) for hd in range(hps)]

    def stacked_q(lanes):
        q = q_ref[:, lanes]
        lane = lax.broadcasted_iota(jnp.int32, q.shape, 1)
        zero = jnp.zeros_like(q)
        return jnp.concatenate([jnp.where(lane < DIFF_QK_DIM, q, zero), jnp.where(lane < DIFF_QK_DIM, zero, q)],
                               axis=0)

    qq = [stacked_q(lanes) for lanes in head_lanes]

    def lane_fold(x, op):
        r = x[:, 0:LANES]
        for t in range(1, tiles):
            r = op(r, x[:, t * LANES:(t + 1) * LANES])
        return r

    m_ref[...] = jnp.full(m_ref.shape, NEG_INF, F32)

    def scores(j, masked):
        k0 = pl.multiple_of(j * tq, tq)
        for hd, lanes in enumerate(head_lanes):
            s = _dot_nt(qq[hd], k_ref[pl.ds(k0, tq), lanes]) * np.float32(math.log2(math.e))
            if masked:
                row = lax.broadcasted_iota(jnp.int32, s.shape, 0)
                col = lax.broadcasted_iota(jnp.int32, s.shape, 1)
                row = jnp.where(row >= tq, row - tq, row)
                s = jnp.where(col <= row, s, NEG_INF)
            s_ref[hd, j] = s
            m_ref[hd] = jnp.maximum(m_ref[hd], lane_fold(s, jnp.maximum))

    def accumulate(j):
        k0 = pl.multiple_of(j * tq, tq)
        for hd, lanes in enumerate(head_lanes):
            s = s_ref[hd, j]
            mx = m_ref[hd]
            ps = [jnp.exp2(s[:, t * LANES:(t + 1) * LANES] - mx) for t in range(tiles)]
            part = ps[0]
            for t in range(1, tiles):
                part = part + ps[t]
            l_ref[hd] += part
            p = jnp.concatenate(ps, axis=1).astype(BF16)
            acc_ref[hd] += _dot(p, v_ref[pl.ds(k0, tq), lanes])

    def in_pairs(fn, count):
        def body(jp, carry):
            fn(2 * jp)
            fn(2 * jp + 1)
            return carry

        lax.fori_loop(0, count // 2, body, 0)

        @pl.when(count % 2 == 1)
        def _():
            fn(count - 1)

    in_pairs(lambda j: scores(j, False), i)
    scores(i, True)
    for hd in range(hps):
        m_ref[hd] = jnp.broadcast_to(jnp.max(m_ref[hd], axis=-1, keepdims=True), m_ref.shape[1:])

    l_ref[...] = jnp.zeros(l_ref.shape, F32)
    acc_ref[...] = jnp.zeros(acc_ref.shape, F32)
    in_pairs(accumulate, i + 1)

    lam = (jnp.exp(jnp.sum(lq1_ref[...] * lk1_ref[...], axis=-1, keepdims=True))
           - jnp.exp(jnp.sum(lq2_ref[...] * lk2_ref[...], axis=-1, keepdims=True)) + np.float32(lam_init))
    for hd, lanes in enumerate(head_lanes):
        o = acc_ref[hd] / jnp.sum(l_ref[hd], axis=-1, keepdims=True)
        od = o[0:tq, :] - lam * o[tq:2 * tq, :]
        o_ref[:, lanes] = (_rms(od, sub_ref[...]) * np.float32(1.0 - lam_init)).astype(o_ref.dtype)


def _diff_attn(q, k, v, lq1, lk1, lq2, lk2, subln, *, batch, seq, tq, hps, lam_init):
    m, bw = q.shape
    heads = bw // HEAD_DIM
    nq = seq // tq
    assert heads % hps == 0
    q_spec = pl.BlockSpec((tq, hps * HEAD_DIM), lambda b, h, i: (b * nq + i, h))
    kv_spec = pl.BlockSpec((seq, hps * HEAD_DIM), lambda b, h, i: (b, h))
    lam_spec = pl.BlockSpec((1, DIFF_QK_DIM), lambda b, h, i: (0, 0))
    stat = pltpu.VMEM((hps, 2 * tq, LANES), F32)
    return pl.pallas_call(
        functools.partial(_diff_attn_kernel, tq=tq, hps=hps, lam_init=lam_init),
        grid=(batch, heads // hps, nq),
        in_specs=[q_spec, kv_spec, kv_spec, lam_spec, lam_spec, lam_spec, lam_spec,
                  pl.BlockSpec((1, HEAD_DIM), lambda b, h, i: (0, 0))],
        out_specs=q_spec,
        out_shape=jax.ShapeDtypeStruct((m, bw), BF16),
        scratch_shapes=[pltpu.VMEM((hps, nq, 2 * tq, tq), F32), stat, stat, stat],
        compiler_params=_params("parallel", "parallel", "arbitrary"),
        name="diff_attn",
    )(q, k, v, lq1, lk1, lq2, lk2, subln)


def _merge_kernel(ya_ref, yb_ref, yc_ref, yd_ref, ga_ref, gb_ref, gc_ref, gd_ref, wb_ref, o_ref, *, tn):
    branches = ((ya_ref, ga_ref), (yb_ref, gb_ref), (yc_ref, gc_ref), (yd_ref, gd_ref))
    for c in range(o_ref.shape[1] // tn):
        cols = slice(c * tn, (c + 1) * tn)
        acc = None
        for b, (y_ref, gate_ref) in enumerate(branches):
            term = gate_ref[:, cols].astype(F32) * _dot(y_ref[...], wb_ref[b, :, cols])
            acc = term if acc is None else acc + term
        o_ref[:, cols] = acc.astype(BF16)


def _merge(ys, gates, w_branch_layers, layer, *, tm, tn):
    m, bw = ys[0].shape
    _, nb, _, d = w_branch_layers.shape
    assert nb == len(ys) == 4
    y_spec = pl.BlockSpec((tm, bw), lambda r: (r, 0))
    gate_specs = [pl.BlockSpec((tm, d), lambda r, b=b: (r, b)) for b in range(nb)]
    return pl.pallas_call(
        functools.partial(_merge_kernel, tn=tn),
        grid=(m // tm,),
        in_specs=[y_spec] * nb + gate_specs + [_resident((None, nb, bw, d), lambda r: (layer, 0, 0, 0))],
        out_specs=pl.BlockSpec((tm, d), lambda r: (r, 0)),
        out_shape=jax.ShapeDtypeStruct((m, d), BF16),
        compiler_params=_params("parallel"),
        name="merge_branches",
    )(*ys, gates, gates, gates, gates, w_branch_layers)


def kernel(x, positions, ffn1_norm, ffn1_w13, ffn1_w2, mix_norm, w_in, conv_w, conv_b, conv_ln_g, conv_ln_b, sgu_ln_g, sgu_ln_b, sgu_w, sgu_b, dil_q_norm, dil_k_norm, diff_q_norm, diff_k_norm, diff_lq1, diff_lk1, diff_lq2, diff_lk2, diff_subln, w_branch, w_out, ffn2_norm, ffn2_w13, ffn2_w2):
    batch, seq, d = x.shape
    depth = w_in.shape[0]
    m = batch * seq
    bw = d // 4
    n_mix = 10 * bw
    assert bw % HEAD_DIM == 0 and w_in.shape[2] == n_mix + 4 * d

    tm = min(1024, m)
    dff = ffn1_w2.shape[1]
    tf = 512 if dff % 512 == 0 else 256
    tn = min(512, d)

    def row(v, l):
        return v[l].reshape(1, -1)

    def ffn(x2, h, w13_layers, w2_layers, l, g_next):
        act = _ffn_up(h, w13_layers, l, tm=tm, tf=tf)
        return _residual_matmul(act, w2_layers, l, x2, g_next, tm=min(512, m), name="ffn_down")

    x2 = x.reshape(m, d)
    cos_c, sin_c, cos_d, sin_d = _rope_tables(positions, ts=min(512, m))
    ffn1_w2_bf, ffn2_w2_bf = ffn1_w2.astype(BF16), ffn2_w2.astype(BF16)
    w_mix_bf = _cast_layers(w_in, n_mix, tr=min(512, d))
    w_out_bf = _cast_layers(w_out, d, tr=min(1024, d))
    w_branch_bf = _cast_layers(w_branch.reshape(depth, 4 * bw, d), d, tr=min(1024, d)).reshape(w_branch.shape)

    h = _rmsnorm(x2, row(ffn1_norm, 0), tm=min(512, m))
    for l in range(depth):
        lam_init = 0.8 - 0.6 * math.exp(-0.3 * l)
        x2, h = ffn(x2, h, ffn1_w13, ffn1_w2_bf, l, row(mix_norm, l))

        z, yb, (cq, ck, cv), (dq, dk, dv) = _mix_proj(
            h, w_mix_bf, l, row(sgu_ln_g, l), row(sgu_ln_b, l),
            sgu_w[l], sgu_b[l], row(dil_q_norm, l), row(dil_k_norm, l), cos_c, sin_c,
            row(diff_q_norm, l), row(diff_k_norm, l), cos_d, sin_d, bw=bw, tm=min(512, m))
        gates = _gate_proj(h, w_in, l, n_mix=n_mix, tm=min(2048, m), tn=math.gcd(n_mix, 1024))

        ya = _conv_mixer(z, conv_w[l], row(conv_b, l), row(conv_ln_g, l), row(conv_ln_b, l),
                         batch=batch, seq=seq, ts=min(1024, seq))
        yc = _dil_attn(cq, ck, cv, batch=batch, seq=seq)
        yd = _diff_attn(dq, dk, dv, row(diff_lq1, l), row(diff_lk1, l), row(diff_lq2, l), row(diff_lk2, l),
                        row(diff_subln, l), batch=batch, seq=seq, tq=min(512, seq), hps=2, lam_init=lam_init)

        merged = _merge((ya, yb, yc, yd), gates, w_branch_bf, l, tm=min(512, m), tn=tn)
        x2, h = _residual_matmul(merged, w_out_bf, l, x2, row(ffn2_norm, l), tm=min(512, m), name="out_proj")

        g_next = row(ffn1_norm, l + 1) if l + 1 < depth else row(ffn2_norm, l)
        x2, h = ffn(x2, h, ffn2_w13, ffn2_w2_bf, l, g_next)
    return x2.reshape(batch, seq, d)
```

```python
import functools
import math

import jax
import jax.numpy as jnp
import numpy as np
from jax import lax
from jax.experimental import pallas as pl
from jax.experimental.pallas import tpu as pltpu

HEAD_DIM = 128
LANES = 128
SUBLANES = 8
DIFF_QK_DIM = HEAD_DIM // 2
SGU_CHUNK = 128
DIL_PATTERNS = ((128, 1), (512, 4), (2048, 16))
DIL_SPAN = 128
DIL_BLOCK = 128
ROPE_THETA = 500000.0
ROPE_FRACTION = 4
EPS = 1e-6
NEG_INF = -1e30
CONV_HALO = 32

VMEM_LIMIT_BYTES = 56 * 1024 * 1024

BF16 = jnp.bfloat16
F32 = jnp.float32


def _params(*sem):
    return pltpu.CompilerParams(dimension_semantics=sem, vmem_limit_bytes=VMEM_LIMIT_BYTES)


def _resident(shape, index_map):
    return pl.BlockSpec(shape, index_map, pipeline_mode=pl.Buffered(1))


def _dot(a, b):
    return jnp.dot(a, b, preferred_element_type=F32)


def _dot_nt(a, b):
    return lax.dot_general(a, b, (((1,), (1,)), ((), ())), preferred_element_type=F32)


def _rms(x, g):
    return x * lax.rsqrt(jnp.mean(x * x, axis=-1, keepdims=True) + EPS) * g


def _layernorm(x, g, b):
    mu = jnp.mean(x, axis=-1, keepdims=True)
    xc = x - mu
    return xc * lax.rsqrt(jnp.mean(xc * xc, axis=-1, keepdims=True) + EPS) * g + b


def _gelu(x):
    return 0.5 * x * (1.0 + lax.erf(x * np.float32(math.sqrt(0.5))))


def _cast_kernel(w_ref, o_ref):
    o_ref[...] = w_ref[...].astype(BF16)


def _cast_layers(w_layers, ncols, *, tr):
    depth, rows, _ = w_layers.shape
    return pl.pallas_call(
        _cast_kernel,
        grid=(depth, rows // tr),
        in_specs=[pl.BlockSpec((None, tr, ncols), lambda l, i: (l, i, 0))],
        out_specs=pl.BlockSpec((None, tr, ncols), lambda l, i: (l, i, 0)),
        out_shape=jax.ShapeDtypeStruct((depth, rows, ncols), BF16),
        compiler_params=_params("parallel", "parallel"),
        name="cast_weights",
    )(w_layers)


def _rmsnorm_kernel(x_ref, g_ref, o_ref):
    o_ref[...] = _rms(x_ref[...], g_ref[...]).astype(BF16)


def _rmsnorm(x2, g, *, tm):
    m, d = x2.shape
    return pl.pallas_call(
        _rmsnorm_kernel,
        grid=(m // tm,),
        in_specs=[pl.BlockSpec((tm, d), lambda i: (i, 0)), pl.BlockSpec((1, d), lambda i: (0, 0))],
        out_specs=pl.BlockSpec((tm, d), lambda i: (i, 0)),
        out_shape=jax.ShapeDtypeStruct((m, d), BF16),
        compiler_params=_params("parallel"),
        name="rmsnorm",
    )(x2, g)


def _ffn_up_kernel(h_ref, wg_ref, wu_ref, a_ref):
    h = h_ref[...]
    gate = _dot(h, wg_ref[...].astype(BF16))
    up = _dot(h, wu_ref[...].astype(BF16))
    a_ref[...] = (0.5 * gate * jax.nn.sigmoid(gate) * up).astype(BF16)


def _ffn_up(h, w13_layers, layer, *, tm, tf):
    m, d = h.shape
    dff = w13_layers.shape[2] // 2
    nf = dff // tf
    return pl.pallas_call(
        _ffn_up_kernel,
        grid=(m // tm, nf),
        in_specs=[
            pl.BlockSpec((tm, d), lambda i, j: (i, 0)),
            pl.BlockSpec((None, d, tf), lambda i, j: (layer, 0, j)),
            pl.BlockSpec((None, d, tf), lambda i, j: (layer, 0, j + nf)),
        ],
        out_specs=pl.BlockSpec((tm, tf), lambda i, j: (i, j)),
        out_shape=jax.ShapeDtypeStruct((m, dff), BF16),
        compiler_params=_params("parallel", "arbitrary"),
        name="ffn_up",
    )(h, w13_layers, w13_layers)


def _residual_matmul_kernel(a_ref, w_ref, x_ref, g_ref, o_ref, h_ref):
    out = x_ref[...] + _dot(a_ref[...], w_ref[...])
    o_ref[...] = out
    h_ref[...] = _rms(out, g_ref[...]).astype(BF16)


def _residual_matmul(a, w_layers, layer, x2, g_next, *, tm, name):
    m, kdim = a.shape
    d = w_layers.shape[2]
    tile = pl.BlockSpec((tm, d), lambda i: (i, 0))
    return pl.pallas_call(
        _residual_matmul_kernel,
        grid=(m // tm,),
        in_specs=[pl.BlockSpec((tm, kdim), lambda i: (i, 0)),
                  _resident((None, kdim, d), lambda i: (layer, 0, 0)), tile,
                  pl.BlockSpec((1, d), lambda i: (0, 0))],
        out_specs=[tile, tile],
        out_shape=[jax.ShapeDtypeStruct((m, d), F32), jax.ShapeDtypeStruct((m, d), BF16)],
        compiler_params=_params("parallel"),
        name=name,
    )(a, w_layers, x2, g_next)


def _rope_table_kernel(pos_ref, inv_c_ref, sgn_c_ref, inv_d_ref, sgn_d_ref,
                       cos_c_ref, sin_c_ref, cos_d_ref, sin_d_ref):
    p = pos_ref[...].astype(F32)
    ang_c = p * inv_c_ref[...]
    cos_c_ref[...] = jnp.cos(ang_c)
    sin_c_ref[...] = jnp.sin(ang_c) * sgn_c_ref[...]
    ang_d = p * inv_d_ref[...]
    cos_d_ref[...] = jnp.cos(ang_d)
    sin_d_ref[...] = jnp.sin(ang_d) * sgn_d_ref[...]


def _rope_lane_consts(width):
    rot = width // ROPE_FRACTION
    half = rot // 2
    inv_freq = 1.0 / (ROPE_THETA ** (jnp.arange(half, dtype=F32) * 2.0 / rot))
    lane = np.arange(LANES) % width
    in_rot = lane < rot
    inv = jnp.where(in_rot, inv_freq[lane % half], 0.0).astype(F32)
    sgn = np.where(lane < half, -1.0, np.where(in_rot, 1.0, 0.0)).astype(np.float32)
    return inv.reshape(1, LANES), jnp.asarray(sgn).reshape(1, LANES)


def _rope_tables(positions, *, ts):
    m = positions.size
    inv_c, sgn_c = _rope_lane_consts(HEAD_DIM)
    inv_d, sgn_d = _rope_lane_consts(DIFF_QK_DIM)
    lane_spec = pl.BlockSpec((1, LANES), lambda i: (0, 0))
    tab_spec = pl.BlockSpec((ts, LANES), lambda i: (i, 0))
    tab = jax.ShapeDtypeStruct((m, LANES), F32)
    return pl.pallas_call(
        _rope_table_kernel,
        grid=(m // ts,),
        in_specs=[pl.BlockSpec((ts, 1), lambda i: (i, 0)), lane_spec, lane_spec, lane_spec, lane_spec],
        out_specs=[tab_spec] * 4,
        out_shape=[tab] * 4,
        compiler_params=_params("parallel"),
        name="rope_tables",
    )(positions.reshape(m, 1), inv_c, sgn_c, inv_d, sgn_d)


def _rope(x, cos, sin, half):
    lane = lax.broadcasted_iota(jnp.int32, x.shape, 1)
    lower = (lane % (2 * half)) < half
    partner = jnp.where(lower, pltpu.roll(x, LANES - half, 1), pltpu.roll(x, half, 1))
    return x * cos + partner * sin


def _segment_rms(x, g):
    lane = lax.broadcasted_iota(jnp.int32, x.shape, 1)
    lower = lane < DIFF_QK_DIM
    sq = x * x
    lo = jnp.sum(jnp.where(lower, sq, 0.0), axis=-1, keepdims=True)
    hi = jnp.sum(jnp.where(lower, 0.0, sq), axis=-1, keepdims=True)
    ms = jnp.where(lower, lo, hi) * np.float32(1.0 / DIFF_QK_DIM)
    return x * lax.rsqrt(ms + EPS) * g


def _mix_proj_kernel(h_ref, w_ref, sg_ref, sb_ref, ws_ref, bias_ref, cqg_ref, ckg_ref, cos_c_ref, sin_c_ref,
                     dqg_ref, dkg_ref, cos_d_ref, sin_d_ref,
                     z_ref, yb_ref, dq_ref, dk_ref, dv_ref, *c_refs_and_scratch, bw, tm, dils):
    c_refs, stage_ref = c_refs_and_scratch[:-1], c_refs_and_scratch[-1]
    h = h_ref[...]
    heads = bw // HEAD_DIM

    def seg(c0, n):
        return _dot(h, w_ref[:, c0 * bw:(c0 + n) * bw])

    r = seg(2, 2)
    u = _gelu(r[:, :bw])
    v = _layernorm(_gelu(r[:, bw:]), sg_ref[...], sb_ref[...]).astype(BF16)
    row = lax.broadcasted_iota(jnp.int32, (SGU_CHUNK, SGU_CHUNK), 0)
    col = lax.broadcasted_iota(jnp.int32, (SGU_CHUNK, SGU_CHUNK), 1)
    for g in range(heads):
        lanes = slice(g * HEAD_DIM, (g + 1) * HEAD_DIM)
        w = jnp.where(row >= col, ws_ref[g], 0.0).astype(BF16)
        for c in range(tm // SGU_CHUNK):
            rws = slice(c * SGU_CHUNK, (c + 1) * SGU_CHUNK)
            mixed = _dot(w, v[rws, lanes]) + bias_ref[:, lanes]
            yb_ref[rws, lanes] = (u[rws, lanes] * mixed).astype(BF16)

    r = seg(4, 3)
    cos, sin = cos_c_ref[...], sin_c_ref[...]
    half = HEAD_DIM // ROPE_FRACTION // 2
    for part in range(3):
        outs = c_refs[part * len(dils):(part + 1) * len(dils)]
        for hd in range(heads):
            lanes = slice(hd * HEAD_DIM, (hd + 1) * HEAD_DIM)
            t = r[:, part * bw + hd * HEAD_DIM:part * bw + (hd + 1) * HEAD_DIM]
            if part < 2:
                t = _rope(_rms(t, (cqg_ref, ckg_ref)[part][...]), cos, sin, half)
            stage_ref[hd] = t
            for dil, o_ref in zip(dils, outs):
                if dil == 1:
                    o_ref[:, lanes] = t.astype(BF16)
        for dil, o_ref in zip(dils, outs):
            if dil == 1:
                continue
            for res in range(dil):
                for hd in range(heads):
                    o_ref[:, res * bw + hd * HEAD_DIM:res * bw + (hd + 1) * HEAD_DIM] = (
                        stage_ref[hd, pl.ds(res, tm // dil, stride=dil), :].astype(BF16))

    r = seg(7, 3)
    cos, sin = cos_d_ref[...], sin_d_ref[...]
    half = DIFF_QK_DIM // ROPE_FRACTION // 2
    scale = np.float32(DIFF_QK_DIM ** -0.5)
    for hd in range(heads):
        lanes = slice(hd * HEAD_DIM, (hd + 1) * HEAD_DIM)
        klanes = slice(bw + hd * HEAD_DIM, bw + (hd + 1) * HEAD_DIM)
        q = _rope(_segment_rms(r[:, lanes], dqg_ref[...]), cos, sin, half)
        dq_ref[:, lanes] = (q * scale).astype(BF16)
        dk_ref[:, lanes] = _rope(_segment_rms(r[:, klanes], dkg_ref[...]), cos, sin, half).astype(BF16)
    dv_ref[...] = r[:, 2 * bw:].astype(BF16)

    r = seg(0, 2)
    z_ref[...] = r[:, :bw] * jax.nn.sigmoid(r[:, bw:])


def _mix_proj(h, w_mix_layers, layer, sgu_g, sgu_b, w_s, b_s, cq_g, ck_g, cos_c, sin_c, dq_g, dk_g, cos_d, sin_d,
              *, bw, tm):
    m, d = h.shape
    groups = w_s.shape[0]
    heads = bw // HEAD_DIM
    dils = tuple(dil for _, dil in DIL_PATTERNS)
    assert groups == heads and w_mix_layers.shape[2] == 10 * bw
    assert all(tm % (dil * 2 * SUBLANES) == 0 for dil in dils)
    bias = jnp.repeat(b_s.T, HEAD_DIM, axis=1)
    vec = pl.BlockSpec((1, bw), lambda i: (0, 0))
    lane_spec = pl.BlockSpec((1, LANES), lambda i: (0, 0))
    tab_spec = pl.BlockSpec((tm, LANES), lambda i: (i, 0))
    out_spec = pl.BlockSpec((tm, bw), lambda i: (i, 0))
    out_bf = jax.ShapeDtypeStruct((m, bw), BF16)
    view_specs = [pl.BlockSpec((tm // dil, dil * bw), lambda i: (i, 0)) for dil in dils] * 3
    view_shapes = [jax.ShapeDtypeStruct((m // dil, dil * bw), BF16) for dil in dils] * 3
    outs = pl.pallas_call(
        functools.partial(_mix_proj_kernel, bw=bw, tm=tm, dils=dils),
        grid=(m // tm,),
        in_specs=[
            pl.BlockSpec((tm, d), lambda i: (i, 0)),
            _resident((None, d, 10 * bw), lambda i: (layer, 0, 0)),
            vec, vec,
            pl.BlockSpec((groups, SGU_CHUNK, SGU_CHUNK), lambda i: (0, 0, 0)),
            pl.BlockSpec((SGU_CHUNK, bw), lambda i: (0, 0)),
            lane_spec, lane_spec, tab_spec, tab_spec,
            lane_spec, lane_spec, tab_spec, tab_spec,
        ],
        out_specs=[out_spec] * 5 + view_specs,
        out_shape=[jax.ShapeDtypeStruct((m, bw), F32)] + [out_bf] * 4 + view_shapes,
        scratch_shapes=[pltpu.VMEM((heads, tm, HEAD_DIM), F32)],
        compiler_params=_params("parallel"),
        name="mix_proj",
    )(h, w_mix_layers, sgu_g, sgu_b, w_s, bias, cq_g, ck_g, cos_c, sin_c,
      jnp.tile(dq_g, (1, 2)), jnp.tile(dk_g, (1, 2)), cos_d, sin_d)
    n = len(dils)
    z, yb, dq, dk, dv = outs[:5]
    cq, ck, cv = outs[5:5 + n], outs[5 + n:5 + 2 * n], outs[5 + 2 * n:]
    return z, yb, (cq, ck, cv), (dq, dk, dv)


def _gate_proj_kernel(h_ref, w_ref, o_ref):
    o_ref[...] = jax.nn.sigmoid(_dot(h_ref[...], w_ref[...].astype(BF16))).astype(BF16)


def _gate_proj(h, w_in_layers, layer, *, n_mix, tm, tn):
    m, d = h.shape
    n = w_in_layers.shape[2] - n_mix
    assert n_mix % tn == 0 and n % tn == 0
    first = n_mix // tn
    return pl.pallas_call(
        _gate_proj_kernel,
        grid=(m // tm, n // tn),
        in_specs=[pl.BlockSpec((tm, d), lambda i, j: (i, 0)),
                  pl.BlockSpec((None, d, tn), lambda i, j: (layer, 0, first + j))],
        out_specs=pl.BlockSpec((tm, tn), lambda i, j: (i, j)),
        out_shape=jax.ShapeDtypeStruct((m, n), BF16),
        compiler_params=_params("parallel", "arbitrary"),
        name="gate_proj",
    )(h, w_in_layers)


def _conv_kernel(zin_ref, w_ref, b_ref, lg_ref, lb_ref, o_ref, z_ref, *, ts, width, rows):
    ext = ts + CONV_HALO

    @pl.when(pl.program_id(1) == 0)
    def _():
        z_ref[0, 0:CONV_HALO, :] = jnp.zeros((CONV_HALO, z_ref.shape[2]), F32)

    z_ref[0, CONV_HALO:ext, :] = zin_ref[...]
    for s in range(1, SUBLANES):
        z_ref[s, 0:ext - SUBLANES, :] = z_ref[0, s:s + ext - SUBLANES, :]
    ch = z_ref.shape[2]
    for c in range(ts // rows):
        r0 = c * rows
        acc = jnp.broadcast_to(b_ref[...], (rows, ch))
        for k in range(width):
            off = CONV_HALO - (width - 1) + k
            base = r0 + off // SUBLANES * SUBLANES
            acc = acc + w_ref[k:k + 1, :] * z_ref[off % SUBLANES, base:base + rows, :]
        y = _layernorm(acc, lg_ref[...], lb_ref[...])
        o_ref[r0:r0 + rows, :] = (y * jax.nn.sigmoid(y)).astype(o_ref.dtype)
    z_ref[0, 0:CONV_HALO, :] = z_ref[0, ts:ext, :]


def _conv_mixer(z, conv_w, conv_b, ln_g, ln_b, *, batch, seq, ts):
    m, bw = z.shape
    width = conv_w.shape[0]
    assert width - 1 <= CONV_HALO
    ns = seq // ts
    vec = pl.BlockSpec((1, bw), lambda b, s: (0, 0))
    tile = pl.BlockSpec((ts, bw), lambda b, s: (b * ns + s, 0))
    return pl.pallas_call(
        functools.partial(_conv_kernel, ts=ts, width=width, rows=64),
        grid=(batch, ns),
        in_specs=[tile, pl.BlockSpec((width, bw), lambda b, s: (0, 0)), vec, vec, vec],
        out_specs=tile,
        out_shape=jax.ShapeDtypeStruct((m, bw), BF16),
        scratch_shapes=[pltpu.VMEM((SUBLANES, ts + CONV_HALO, bw), F32)],
        compiler_params=_params("arbitrary", "arbitrary"),
        name="conv_mixer",
    )(z, conv_w, conv_b, ln_g, ln_b)


def _dil_band_kernel(*refs, lt, heads, others):
    bw = heads * HEAD_DIM
    if others:
        q_ref, k_ref, v_ref = refs[:3]
        other_refs = refs[3:3 + 2 * len(others)]
        y_ref = refs[3 + 2 * len(others)]
        stage_refs = refs[4 + 2 * len(others):]
        for dil, o_ref, lse_ref, o_st, lse_st in zip(others, other_refs[0::2], other_refs[1::2],
                                                     stage_refs[0::2], stage_refs[1::2]):
            for res in range(dil):
                for hd in range(heads):
                    cols = slice(res * bw + hd * HEAD_DIM, res * bw + (hd + 1) * HEAD_DIM)
                    o_st[hd, pl.ds(res, lt // dil, stride=dil), :] = o_ref[:, cols]
                    lse_st[hd, pl.ds(res, lt // dil, stride=dil), :] = lse_ref[:, cols]
    else:
        q_ref, k_ref, v_ref, o_ref, lse_ref = refs
    i = pl.program_id(2)
    blk = DIL_BLOCK
    scale = np.float32(HEAD_DIM ** -0.5)
    ones = jnp.ones((2 * blk, HEAD_DIM), BF16)
    row = lax.broadcasted_iota(jnp.int32, (blk, 2 * blk), 0)
    col = lax.broadcasted_iota(jnp.int32, (blk, 2 * blk), 1)
    for jb in range(lt // blk):
        rws = slice(jb * blk, (jb + 1) * blk)
        l0 = i * lt + jb * blk
        ks = pl.multiple_of(jnp.maximum(l0 - blk, 0), blk)
        dist = (l0 - ks) + row - col
        valid = (dist >= 0) & (dist <= DIL_SPAN)
        for hd in range(heads):
            lanes = slice(hd * HEAD_DIM, (hd + 1) * HEAD_DIM)
            s = _dot_nt(q_ref[rws, lanes], k_ref[pl.ds(ks, 2 * blk), lanes]) * scale
            s = jnp.where(valid, s, NEG_INF)
            mx = jnp.max(s, axis=-1, keepdims=True)
            p = jnp.exp(s - mx).astype(BF16)
            r = _dot(p, jnp.concatenate([v_ref[pl.ds(ks, 2 * blk), lanes], ones], axis=1))
            den = r[:, HEAD_DIM:]
            o = r[:, :HEAD_DIM] / den
            lse = mx + jnp.log(den)
            if others:
                o_all = [o] + [st[hd, rws, :] for st in stage_refs[0::2]]
                lse_all = [lse] + [st[hd, rws, :] for st in stage_refs[1::2]]
                top = functools.reduce(jnp.maximum, lse_all)
                wts = [jnp.exp(t - top) for t in lse_all]
                num = wts[0] * o_all[0]
                tot = wts[0]
                for wg, og in zip(wts[1:], o_all[1:]):
                    num = num + wg * og
                    tot = tot + wg
                y_ref[rws, lanes] = (num / tot).astype(BF16)
            else:
                o_ref[rws, lanes] = o
                lse_ref[rws, lanes] = lse


def _dil_band(q, k, v, others, *, batch, seq, dil, lt, classes=1):
    rows, cols = q.shape
    bw = cols // dil * classes
    heads = bw // HEAD_DIM
    ln = seq // dil
    lt = min(lt, ln)
    nl = ln // lt
    assert ln % lt == 0 and lt % DIL_BLOCK == 0 and ln >= 2 * DIL_BLOCK and dil % classes == 0
    tile = pl.BlockSpec((lt, bw), lambda b, r, i: (b * nl + i, r))
    whole = pl.BlockSpec((ln, bw), lambda b, r, i: (b, r))
    if others:
        assert dil == 1 and all(lt % (od * SUBLANES) == 0 for od, _, _ in others)
        extra = [a for _, o, lse in others for a in (o, lse)]
        extra_specs = [pl.BlockSpec((lt // od, od * bw), lambda b, r, i: (b * nl + i, 0))
                       for od, _, _ in others for _ in range(2)]
        out_specs, out_shape = tile, jax.ShapeDtypeStruct((rows, cols), BF16)
        scratch = [pltpu.VMEM((heads, lt, HEAD_DIM), F32)] * len(extra)
    else:
        extra, extra_specs, scratch = [], [], []
        out_specs = [tile, tile]
        out_shape = [jax.ShapeDtypeStruct((rows, cols), F32)] * 2
    return pl.pallas_call(
        functools.partial(_dil_band_kernel, lt=lt, heads=heads, others=tuple(od for od, _, _ in others)),
        grid=(batch, dil // classes, nl),
        in_specs=[tile, whole, whole] + extra_specs,
        out_specs=out_specs,
        out_shape=out_shape,
        scratch_shapes=scratch,
        compiler_params=_params("parallel", "parallel", "arbitrary"),
        name=f"dil_band_{dil}",
    )(q, k, v, *extra)


def _dil_attn(cq, ck, cv, *, batch, seq):
    dils = [d for _, d in DIL_PATTERNS]
    assert all(w // d == DIL_SPAN for w, d in DIL_PATTERNS) and dils[0] == 1
    others = []
    heads = cq[0].shape[1] // HEAD_DIM
    for idx in range(1, len(dils)):
        dil = dils[idx]
        lt = min(seq // dil, 8 * DIL_BLOCK)
        classes = math.gcd(dil, max(1, 32 // (lt // DIL_BLOCK * heads)))
        o, lse = _dil_band(cq[idx], ck[idx], cv[idx], [], batch=batch, seq=seq, dil=dil, lt=lt, classes=classes)
        others.append((dil, o, lse))
    return _dil_band(cq[0], ck[0], cv[0], others, batch=batch, seq=seq, dil=1, lt=1024)


def _diff_attn_kernel(q_ref, k_ref, v_ref, lq1_ref, lk1_ref, lq2_ref, lk2_ref, sub_ref, o_ref,
                      s_ref, m_ref, l_ref, acc_ref, *, tq, hps, lam_init):
    i = pl.program_id(2)
    tiles = tq // LANES
    head_lanes = [slice(hd * HEAD_DIM, (hd + 1) * HEAD_DIM) for hd in range(hps)]

    def stacked_q(lanes):
        q = q_ref[:, lanes]
        lane = lax.broadcasted_iota(jnp.int32, q.shape, 1)
        zero = jnp.zeros_like(q)
        return jnp.concatenate([jnp.where(lane < DIFF_QK_DIM, q, zero), jnp.where(lane < DIFF_QK_DIM, zero, q)],
                               axis=0)

    qq = [stacked_q(lanes) for lanes in head_lanes]

    def lane_fold(x, op):
        r = x[:, 0:LANES]
        for t in range(1, tiles):
            r = op(r, x[:, t * LANES:(t + 1) * LANES])
        return r

    m_ref[...] = jnp.full(m_ref.shape, NEG_INF, F32)

    def scores(j, masked):
        k0 = pl.multiple_of(j * tq, tq)
        for hd, lanes in enumerate(head_lanes):
            s = _dot_nt(qq[hd], k_ref[pl.ds(k0, tq), lanes]) * np.float32(math.log2(math.e))
            if masked:
                row = lax.broadcasted_iota(jnp.int32, s.shape, 0)
                col = lax.broadcasted_iota(jnp.int32, s.shape, 1)
                row = jnp.where(row >= tq, row - tq, row)
                s = jnp.where(col <= row, s, NEG_INF)
            s_ref[hd, j] = s
            m_ref[hd] = jnp.maximum(m_ref[hd], lane_fold(s, jnp.maximum))

    def accumulate(j):
        k0 = pl.multiple_of(j * tq, tq)
        for hd, lanes in enumerate(head_lanes):
            s = s_ref[hd, j]
            mx = m_ref[hd]
            ps = [jnp.exp2(s[:, t * LANES:(t + 1) * LANES] - mx) for t in range(tiles)]
            part = ps[0]
            for t in range(1, tiles):
                part = part + ps[t]
            l_ref[hd] += part
            p = jnp.concatenate(ps, axis=1).astype(BF16)
            acc_ref[hd] += _dot(p, v_ref[pl.ds(k0, tq), lanes])

    def in_pairs(fn, count):
        def body(jp, carry):
            fn(2 * jp)
            fn(2 * jp + 1)
            return carry

        lax.fori_loop(0, count // 2, body, 0)

        @pl.when(count % 2 == 1)
        def _():
            fn(count - 1)

    in_pairs(lambda j: scores(j, False), i)
    scores(i, True)
    for hd in range(hps):
        m_ref[hd] = jnp.broadcast_to(jnp.max(m_ref[hd], axis=-1, keepdims=True), m_ref.shape[1:])

    l_ref[...] = jnp.zeros(l_ref.shape, F32)
    acc_ref[...] = jnp.zeros(acc_ref.shape, F32)
    in_pairs(accumulate, i + 1)

    lam = (jnp.exp(jnp.sum(lq1_ref[...] * lk1_ref[...], axis=-1, keepdims=True))
           - jnp.exp(jnp.sum(lq2_ref[...] * lk2_ref[...], axis=-1, keepdims=True)) + np.float32(lam_init))
    for hd, lanes in enumerate(head_lanes):
        o = acc_ref[hd] / jnp.sum(l_ref[hd], axis=-1, keepdims=True)
        od = o[0:tq, :] - lam * o[tq:2 * tq, :]
        o_ref[:, lanes] = (_rms(od, sub_ref[...]) * np.float32(1.0 - lam_init)).astype(o_ref.dtype)


def _diff_attn(q, k, v, lq1, lk1, lq2, lk2, subln, *, batch, seq, tq, hps, lam_init):
    m, bw = q.shape
    heads = bw // HEAD_DIM
    nq = seq // tq
    assert heads % hps == 0
    q_spec = pl.BlockSpec((tq, hps * HEAD_DIM), lambda b, h, i: (b * nq + i, h))
    kv_spec = pl.BlockSpec((seq, hps * HEAD_DIM), lambda b, h, i: (b, h))
    lam_spec = pl.BlockSpec((1, DIFF_QK_DIM), lambda b, h, i: (0, 0))
    stat = pltpu.VMEM((hps, 2 * tq, LANES), F32)
    return pl.pallas_call(
        functools.partial(_diff_attn_kernel, tq=tq, hps=hps, lam_init=lam_init),
        grid=(batch, heads // hps, nq),
        in_specs=[q_spec, kv_spec, kv_spec, lam_spec, lam_spec, lam_spec, lam_spec,
                  pl.BlockSpec((1, HEAD_DIM), lambda b, h, i: (0, 0))],
        out_specs=q_spec,
        out_shape=jax.ShapeDtypeStruct((m, bw), BF16),
        scratch_shapes=[pltpu.VMEM((hps, nq, 2 * tq, tq), F32), stat, stat, stat],
        compiler_params=_params("parallel", "parallel", "arbitrary"),
        name="diff_attn",
    )(q, k, v, lq1, lk1, lq2, lk2, subln)


def _merge_kernel(ya_ref, yb_ref, yc_ref, yd_ref, ga_ref, gb_ref, gc_ref, gd_ref, wb_ref, o_ref, *, tn):
    branches = ((ya_ref, ga_ref), (yb_ref, gb_ref), (yc_ref, gc_ref), (yd_ref, gd_ref))
    for c in range(o_ref.shape[1] // tn):
        cols = slice(c * tn, (c + 1) * tn)
        acc = None
        for b, (y_ref, gate_ref) in enumerate(branches):
            term = gate_ref[:, cols].astype(F32) * _dot(y_ref[...], wb_ref[b, :, cols])
            acc = term if acc is None else acc + term
        o_ref[:, cols] = acc.astype(BF16)


def _merge(ys, gates, w_branch_layers, layer, *, tm, tn):
    m, bw = ys[0].shape
    _, nb, _, d = w_branch_layers.shape
    assert nb == len(ys) == 4
    y_spec = pl.BlockSpec((tm, bw), lambda r: (r, 0))
    gate_specs = [pl.BlockSpec((tm, d), lambda r, b=b: (r, b)) for b in range(nb)]
    return pl.pallas_call(
        functools.partial(_merge_kernel, tn=tn),
        grid=(m // tm,),
        in_specs=[y_spec] * nb + gate_specs + [_resident((None, nb, bw, d), lambda r: (layer, 0, 0, 0))],
        out_specs=pl.BlockSpec((tm, d), lambda r: (r, 0)),
        out_shape=jax.ShapeDtypeStruct((m, d), BF16),
        compiler_params=_params("parallel"),
        name="merge_branches",
    )(*ys, gates, gates, gates, gates, w_branch_layers)


def kernel(x, positions, ffn1_norm, ffn1_w13, ffn1_w2, mix_norm, w_in, conv_w, conv_b, conv_ln_g, conv_ln_b, sgu_ln_g, sgu_ln_b, sgu_w, sgu_b, dil_q_norm, dil_k_norm, diff_q_norm, diff_k_norm, diff_lq1, diff_lk1, diff_lq2, diff_lk2, diff_subln, w_branch, w_out, ffn2_norm, ffn2_w13, ffn2_w2):
    batch, seq, d = x.shape
    depth = w_in.shape[0]
    m = batch * seq
    bw = d // 4
    n_mix = 10 * bw
    assert bw % HEAD_DIM == 0 and w_in.shape[2] == n_mix + 4 * d

    tm = min(1024, m)
    dff = ffn1_w2.shape[1]
    tf = 512 if dff % 512 == 0 else 256
    tn = min(512, d)

    def row(v, l):
        return v[l].reshape(1, -1)

    def ffn(x2, h, w13_layers, w2_layers, l, g_next):
        act = _ffn_up(h, w13_layers, l, tm=tm, tf=tf)
        return _residual_matmul(act, w2_layers, l, x2, g_next, tm=min(512, m), name="ffn_down")

    x2 = x.reshape(m, d)
    cos_c, sin_c, cos_d, sin_d = _rope_tables(positions, ts=min(512, m))
    ffn1_w2_bf, ffn2_w2_bf = ffn1_w2.astype(BF16), ffn2_w2.astype(BF16)
    w_mix_bf = _cast_layers(w_in, n_mix, tr=min(512, d))
    w_out_bf = _cast_layers(w_out, d, tr=min(1024, d))
    w_branch_bf = _cast_layers(w_branch.reshape(depth, 4 * bw, d), d, tr=min(1024, d)).reshape(w_branch.shape)

    h = _rmsnorm(x2, row(ffn1_norm, 0), tm=min(512, m))
    for l in range(depth):
        lam_init = 0.8 - 0.6 * math.exp(-0.3 * l)
        x2, h = ffn(x2, h, ffn1_w13, ffn1_w2_bf, l, row(mix_norm, l))

        z, yb, (cq, ck, cv), (dq, dk, dv) = _mix_proj(
            h, w_mix_bf, l, row(sgu_ln_g, l), row(sgu_ln_b, l),
            sgu_w[l], sgu_b[l], row(dil_q_norm, l), row(dil_k_norm, l), cos_c, sin_c,
            row(diff_q_norm, l), row(diff_k_norm, l), cos_d, sin_d, bw=bw, tm=min(512, m))
        gates = _gate_proj(h, w_in, l, n_mix=n_mix, tm=min(2048, m), tn=math.gcd(n_mix, 1024))

        ya = _conv_mixer(z, conv_w[l], row(conv_b, l), row(conv_ln_g, l), row(conv_ln_b, l),
                         batch=batch, seq=seq, ts=min(1024, seq))
        yc = _dil_attn(cq, ck, cv, batch=batch, seq=seq)
        yd = _diff_attn(dq, dk, dv, row(diff_lq1, l), row(diff_lk1, l), row(diff_lq2, l), row(diff_lk2, l),
                        row(diff_subln, l), batch=batch, seq=seq, tq=min(512, seq), hps=2, lam_init=lam_init)

        merged = _merge((ya, yb, yc, yd), gates, w_branch_bf, l, tm=min(512, m), tn=tn)
        x2, h = _residual_matmul(merged, w_out_bf, l, x2, row(ffn2_norm, l), tm=min(512, m), name="out_proj")

        g_next = row(ffn1_norm, l + 1) if l + 1 < depth else row(ffn2_norm, l)
        x2, h = ffn(x2, h, ffn2_w13, ffn2_w2_bf, l, g_next)
    return x2.reshape(batch, seq, d)
```

```python
import functools
import math

import jax
import jax.numpy as jnp
import numpy as np
from jax import lax
from jax.experimental import pallas as pl
from jax.experimental.pallas import tpu as pltpu

HEAD_DIM = 128
LANES = 128
SUBLANES = 8
DIFF_QK_DIM = HEAD_DIM // 2
SGU_CHUNK = 128
DIL_PATTERNS = ((128, 1), (512, 4), (2048, 16))
DIL_SPAN = 128
DIL_BLOCK = 128
ROPE_THETA = 500000.0
ROPE_FRACTION = 4
EPS = 1e-6
NEG_INF = -1e30
CONV_HALO = 32

VMEM_LIMIT_BYTES = 56 * 1024 * 1024

BF16 = jnp.bfloat16
F32 = jnp.float32


def _params(*sem):
    return pltpu.CompilerParams(dimension_semantics=sem, vmem_limit_bytes=VMEM_LIMIT_BYTES)


def _resident(shape, index_map):
    return pl.BlockSpec(shape, index_map, pipeline_mode=pl.Buffered(1))


def _dot(a, b):
    return jnp.dot(a, b, preferred_element_type=F32)


def _dot_nt(a, b):
    return lax.dot_general(a, b, (((1,), (1,)), ((), ())), preferred_element_type=F32)


def _rms(x, g):
    return x * lax.rsqrt(jnp.mean(x * x, axis=-1, keepdims=True) + EPS) * g


def _layernorm(x, g, b):
    mu = jnp.mean(x, axis=-1, keepdims=True)
    xc = x - mu
    return xc * lax.rsqrt(jnp.mean(xc * xc, axis=-1, keepdims=True) + EPS) * g + b


def _gelu(x):
    return 0.5 * x * (1.0 + lax.erf(x * np.float32(math.sqrt(0.5))))


def _cast_kernel(w_ref, o_ref):
    o_ref[...] = w_ref[...].astype(BF16)


def _cast_layers(w_layers, ncols, *, tr):
    depth, rows, _ = w_layers.shape
    return pl.pallas_call(
        _cast_kernel,
        grid=(depth, rows // tr),
        in_specs=[pl.BlockSpec((None, tr, ncols), lambda l, i: (l, i, 0))],
        out_specs=pl.BlockSpec((None, tr, ncols), lambda l, i: (l, i, 0)),
        out_shape=jax.ShapeDtypeStruct((depth, rows, ncols), BF16),
        compiler_params=_params("parallel", "parallel"),
        name="cast_weights",
    )(w_layers)


def _rmsnorm_kernel(x_ref, g_ref, o_ref):
    o_ref[...] = _rms(x_ref[...], g_ref[...]).astype(BF16)


def _rmsnorm(x2, g, *, tm):
    m, d = x2.shape
    return pl.pallas_call(
        _rmsnorm_kernel,
        grid=(m // tm,),
        in_specs=[pl.BlockSpec((tm, d), lambda i: (i, 0)), pl.BlockSpec((1, d), lambda i: (0, 0))],
        out_specs=pl.BlockSpec((tm, d), lambda i: (i, 0)),
        out_shape=jax.ShapeDtypeStruct((m, d), BF16),
        compiler_params=_params("parallel"),
        name="rmsnorm",
    )(x2, g)


def _ffn_up_kernel(h_ref, wg_ref, wu_ref, a_ref):
    h = h_ref[...]
    gate = _dot(h, wg_ref[...].astype(BF16))
    up = _dot(h, wu_ref[...].astype(BF16))
    a_ref[...] = (0.5 * gate * jax.nn.sigmoid(gate) * up).astype(BF16)


def _ffn_up(h, w13_layers, layer, *, tm, tf):
    m, d = h.shape
    dff = w13_layers.shape[2] // 2
    nf = dff // tf
    return pl.pallas_call(
        _ffn_up_kernel,
        grid=(m // tm, nf),
        in_specs=[
            pl.BlockSpec((tm, d), lambda i, j: (i, 0)),
            pl.BlockSpec((None, d, tf), lambda i, j: (layer, 0, j)),
            pl.BlockSpec((None, d, tf), lambda i, j: (layer, 0, j + nf)),
        ],
        out_specs=pl.BlockSpec((tm, tf), lambda i, j: (i, j)),
        out_shape=jax.ShapeDtypeStruct((m, dff), BF16),
        compiler_params=_params("parallel", "arbitrary"),
        name="ffn_up",
    )(h, w13_layers, w13_layers)


def _residual_matmul_kernel(a_ref, w_ref, x_ref, g_ref, o_ref, h_ref):
    out = x_ref[...] + _dot(a_ref[...], w_ref[...])
    o_ref[...] = out
    h_ref[...] = _rms(out, g_ref[...]).astype(BF16)


def _residual_matmul(a, w_layers, layer, x2, g_next, *, tm, name):
    m, kdim = a.shape
    d = w_layers.shape[2]
    tile = pl.BlockSpec((tm, d), lambda i: (i, 0))
    return pl.pallas_call(
        _residual_matmul_kernel,
        grid=(m // tm,),
        in_specs=[pl.BlockSpec((tm, kdim), lambda i: (i, 0)),
                  _resident((None, kdim, d), lambda i: (layer, 0, 0)), tile,
                  pl.BlockSpec((1, d), lambda i: (0, 0))],
        out_specs=[tile, tile],
        out_shape=[jax.ShapeDtypeStruct((m, d), F32), jax.ShapeDtypeStruct((m, d), BF16)],
        compiler_params=_params("parallel"),
        name=name,
    )(a, w_layers, x2, g_next)


def _rope_table_kernel(pos_ref, inv_c_ref, sgn_c_ref, inv_d_ref, sgn_d_ref,
                       cos_c_ref, sin_c_ref, cos_d_ref, sin_d_ref):
    p = pos_ref[...].astype(F32)
    ang_c = p * inv_c_ref[...]
    cos_c_ref[...] = jnp.cos(ang_c)
    sin_c_ref[...] = jnp.sin(ang_c) * sgn_c_ref[...]
    ang_d = p * inv_d_ref[...]
    cos_d_ref[...] = jnp.cos(ang_d)
    sin_d_ref[...] = jnp.sin(ang_d) * sgn_d_ref[...]


def _rope_lane_consts(width):
    rot = width // ROPE_FRACTION
    half = rot // 2
    inv_freq = 1.0 / (ROPE_THETA ** (jnp.arange(half, dtype=F32) * 2.0 / rot))
    lane = np.arange(LANES) % width
    in_rot = lane < rot
    inv = jnp.where(in_rot, inv_freq[lane % half], 0.0).astype(F32)
    sgn = np.where(lane < half, -1.0, np.where(in_rot, 1.0, 0.0)).astype(np.float32)
    return inv.reshape(1, LANES), jnp.asarray(sgn).reshape(1, LANES)


def _rope_tables(positions, *, ts):
    m = positions.size
    inv_c, sgn_c = _rope_lane_consts(HEAD_DIM)
    inv_d, sgn_d = _rope_lane_consts(DIFF_QK_DIM)
    lane_spec = pl.BlockSpec((1, LANES), lambda i: (0, 0))
    tab_spec = pl.BlockSpec((ts, LANES), lambda i: (i, 0))
    tab = jax.ShapeDtypeStruct((m, LANES), F32)
    return pl.pallas_call(
        _rope_table_kernel,
        grid=(m // ts,),
        in_specs=[pl.BlockSpec((ts, 1), lambda i: (i, 0)), lane_spec, lane_spec, lane_spec, lane_spec],
        out_specs=[tab_spec] * 4,
        out_shape=[tab] * 4,
        compiler_params=_params("parallel"),
        name="rope_tables",
    )(positions.reshape(m, 1), inv_c, sgn_c, inv_d, sgn_d)


def _rope(x, cos, sin, half):
    lane = lax.broadcasted_iota(jnp.int32, x.shape, 1)
    lower = (lane % (2 * half)) < half
    partner = jnp.where(lower, pltpu.roll(x, LANES - half, 1), pltpu.roll(x, half, 1))
    return x * cos + partner * sin


def _segment_rms(x, g):
    lane = lax.broadcasted_iota(jnp.int32, x.shape, 1)
    lower = lane < DIFF_QK_DIM
    sq = x * x
    lo = jnp.sum(jnp.where(lower, sq, 0.0), axis=-1, keepdims=True)
    hi = jnp.sum(jnp.where(lower, 0.0, sq), axis=-1, keepdims=True)
    ms = jnp.where(lower, lo, hi) * np.float32(1.0 / DIFF_QK_DIM)
    return x * lax.rsqrt(ms + EPS) * g


def _mix_proj_kernel(h_ref, w_ref, sg_ref, sb_ref, ws_ref, bias_ref, cqg_ref, ckg_ref, cos_c_ref, sin_c_ref,
                     dqg_ref, dkg_ref, cos_d_ref, sin_d_ref,
                     z_ref, yb_ref, dq_ref, dk_ref, dv_ref, *c_refs_and_scratch, bw, tm, dils):
    c_refs, stage_ref = c_refs_and_scratch[:-1], c_refs_and_scratch[-1]
    h = h_ref[...]
    heads = bw // HEAD_DIM

    def seg(c0, n):
        return _dot(h, w_ref[:, c0 * bw:(c0 + n) * bw])

    r = seg(2, 2)
    u = _gelu(r[:, :bw])
    v = _layernorm(_gelu(r[:, bw:]), sg_ref[...], sb_ref[...]).astype(BF16)
    row = lax.broadcasted_iota(jnp.int32, (SGU_CHUNK, SGU_CHUNK), 0)
    col = lax.broadcasted_iota(jnp.int32, (SGU_CHUNK, SGU_CHUNK), 1)
    for g in range(heads):
        lanes = slice(g * HEAD_DIM, (g + 1) * HEAD_DIM)
        w = jnp.where(row >= col, ws_ref[g], 0.0).astype(BF16)
        for c in range(tm // SGU_CHUNK):
            rws = slice(c * SGU_CHUNK, (c + 1) * SGU_CHUNK)
            mixed = _dot(w, v[rws, lanes]) + bias_ref[:, lanes]
            yb_ref[rws, lanes] = (u[rws, lanes] * mixed).astype(BF16)

    r = seg(4, 3)
    cos, sin = cos_c_ref[...], sin_c_ref[...]
    half = HEAD_DIM // ROPE_FRACTION // 2
    for part in range(3):
        outs = c_refs[part * len(dils):(part + 1) * len(dils)]
        for hd in range(heads):
            lanes = slice(hd * HEAD_DIM, (hd + 1) * HEAD_DIM)
            t = r[:, part * bw + hd * HEAD_DIM:part * bw + (hd + 1) * HEAD_DIM]
            if part < 2:
                t = _rope(_rms(t, (cqg_ref, ckg_ref)[part][...]), cos, sin, half)
            stage_ref[hd] = t
            for dil, o_ref in zip(dils, outs):
                if dil == 1:
                    o_ref[:, lanes] = t.astype(BF16)
        for dil, o_ref in zip(dils, outs):
            if dil == 1:
                continue
            for res in range(dil):
                for hd in range(heads):
                    o_ref[:, res * bw + hd * HEAD_DIM:res * bw + (hd + 1) * HEAD_DIM] = (
                        stage_ref[hd, pl.ds(res, tm // dil, stride=dil), :].astype(BF16))

    r = seg(7, 3)
    cos, sin = cos_d_ref[...], sin_d_ref[...]
    half = DIFF_QK_DIM // ROPE_FRACTION // 2
    scale = np.float32(DIFF_QK_DIM ** -0.5)
    for hd in range(heads):
        lanes = slice(hd * HEAD_DIM, (hd + 1) * HEAD_DIM)
        klanes = slice(bw + hd * HEAD_DIM, bw + (hd + 1) * HEAD_DIM)
        q = _rope(_segment_rms(r[:, lanes], dqg_ref[...]), cos, sin, half)
        dq_ref[:, lanes] = (q * scale).astype(BF16)
        dk_ref[:, lanes] = _rope(_segment_rms(r[:, klanes], dkg_ref[...]), cos, sin, half).astype(BF16)
    dv_ref[...] = r[:, 2 * bw:].astype(BF16)

    r = seg(0, 2)
    z_ref[...] = r[:, :bw] * jax.nn.sigmoid(r[:, bw:])


def _mix_proj(h, w_mix_layers, layer, sgu_g, sgu_b, w_s, b_s, cq_g, ck_g, cos_c, sin_c, dq_g, dk_g, cos_d, sin_d,
              *, bw, tm):
    m, d = h.shape
    groups = w_s.shape[0]
    heads = bw // HEAD_DIM
    dils = tuple(dil for _, dil in DIL_PATTERNS)
    assert groups == heads and w_mix_layers.shape[2] == 10 * bw
    assert all(tm % (dil * 2 * SUBLANES) == 0 for dil in dils)
    bias = jnp.repeat(b_s.T, HEAD_DIM, axis=1)
    vec = pl.BlockSpec((1, bw), lambda i: (0, 0))
    lane_spec = pl.BlockSpec((1, LANES), lambda i: (0, 0))
    tab_spec = pl.BlockSpec((tm, LANES), lambda i: (i, 0))
    out_spec = pl.BlockSpec((tm, bw), lambda i: (i, 0))
    out_bf = jax.ShapeDtypeStruct((m, bw), BF16)
    view_specs = [pl.BlockSpec((tm // dil, dil * bw), lambda i: (i, 0)) for dil in dils] * 3
    view_shapes = [jax.ShapeDtypeStruct((m // dil, dil * bw), BF16) for dil in dils] * 3
    outs = pl.pallas_call(
        functools.partial(_mix_proj_kernel, bw=bw, tm=tm, dils=dils),
        grid=(m // tm,),
        in_specs=[
            pl.BlockSpec((tm, d), lambda i: (i, 0)),
            _resident((None, d, 10 * bw), lambda i: (layer, 0, 0)),
            vec, vec,
            pl.BlockSpec((groups, SGU_CHUNK, SGU_CHUNK), lambda i: (0, 0, 0)),
            pl.BlockSpec((SGU_CHUNK, bw), lambda i: (0, 0)),
            lane_spec, lane_spec, tab_spec, tab_spec,
            lane_spec, lane_spec, tab_spec, tab_spec,
        ],
        out_specs=[out_spec] * 5 + view_specs,
        out_shape=[jax.ShapeDtypeStruct((m, bw), F32)] + [out_bf] * 4 + view_shapes,
        scratch_shapes=[pltpu.VMEM((heads, tm, HEAD_DIM), F32)],
        compiler_params=_params("parallel"),
        name="mix_proj",
    )(h, w_mix_layers, sgu_g, sgu_b, w_s, bias, cq_g, ck_g, cos_c, sin_c,
      jnp.tile(dq_g, (1, 2)), jnp.tile(dk_g, (1, 2)), cos_d, sin_d)
    n = len(dils)
    z, yb, dq, dk, dv = outs[:5]
    cq, ck, cv = outs[5:5 + n], outs[5 + n:5 + 2 * n], outs[5 + 2 * n:]
    return z, yb, (cq, ck, cv), (dq, dk, dv)


def _gate_proj_kernel(h_ref, w_ref, o_ref):
    o_ref[...] = jax.nn.sigmoid(_dot(h_ref[...], w_ref[...].astype(BF16))).astype(BF16)


def _gate_proj(h, w_in_layers, layer, *, n_mix, tm, tn):
    m, d = h.shape
    n = w_in_layers.shape[2] - n_mix
    assert n_mix % tn == 0 and n % tn == 0
    first = n_mix // tn
    return pl.pallas_call(
        _gate_proj_kernel,
        grid=(m // tm, n // tn),
        in_specs=[pl.BlockSpec((tm, d), lambda i, j: (i, 0)),
                  pl.BlockSpec((None, d, tn), lambda i, j: (layer, 0, first + j))],
        out_specs=pl.BlockSpec((tm, tn), lambda i, j: (i, j)),
        out_shape=jax.ShapeDtypeStruct((m, n), BF16),
        compiler_params=_params("parallel", "arbitrary"),
        name="gate_proj",
    )(h, w_in_layers)


def _conv_kernel(zin_ref, w_ref, b_ref, lg_ref, lb_ref, o_ref, z_ref, *, ts, width, rows):
    ext = ts + CONV_HALO

    @pl.when(pl.program_id(1) == 0)
    def _():
        z_ref[0, 0:CONV_HALO, :] = jnp.zeros((CONV_HALO, z_ref.shape[2]), F32)

    z_ref[0, CONV_HALO:ext, :] = zin_ref[...]
    for s in range(1, SUBLANES):
        z_ref[s, 0:ext - SUBLANES, :] = z_ref[0, s:s + ext - SUBLANES, :]
    ch = z_ref.shape[2]
    for c in range(ts // rows):
        r0 = c * rows
        acc = jnp.broadcast_to(b_ref[...], (rows, ch))
        for k in range(width):
            off = CONV_HALO - (width - 1) + k
            base = r0 + off // SUBLANES * SUBLANES
            acc = acc + w_ref[k:k + 1, :] * z_ref[off % SUBLANES, base:base + rows, :]
        y = _layernorm(acc, lg_ref[...], lb_ref[...])
        o_ref[r0:r0 + rows, :] = (y * jax.nn.sigmoid(y)).astype(o_ref.dtype)
    z_ref[0, 0:CONV_HALO, :] = z_ref[0, ts:ext, :]


def _conv_mixer(z, conv_w, conv_b, ln_g, ln_b, *, batch, seq, ts):
    m, bw = z.shape
    width = conv_w.shape[0]
    assert width - 1 <= CONV_HALO
    ns = seq // ts
    vec = pl.BlockSpec((1, bw), lambda b, s: (0, 0))
    tile = pl.BlockSpec((ts, bw), lambda b, s: (b * ns + s, 0))
    return pl.pallas_call(
        functools.partial(_conv_kernel, ts=ts, width=width, rows=64),
        grid=(batch, ns),
        in_specs=[tile, pl.BlockSpec((width, bw), lambda b, s: (0, 0)), vec, vec, vec],
        out_specs=tile,
        out_shape=jax.ShapeDtypeStruct((m, bw), BF16),
        scratch_shapes=[pltpu.VMEM((SUBLANES, ts + CONV_HALO, bw), F32)],
        compiler_params=_params("arbitrary", "arbitrary"),
        name="conv_mixer",
    )(z, conv_w, conv_b, ln_g, ln_b)


def _dil_band_kernel(*refs, lt, heads, others):
    bw = heads * HEAD_DIM
    if others:
        q_ref, k_ref, v_ref = refs[:3]
        other_refs = refs[3:3 + 2 * len(others)]
        y_ref = refs[3 + 2 * len(others)]
        stage_refs = refs[4 + 2 * len(others):]
        for dil, o_ref, lse_ref, o_st, lse_st in zip(others, other_refs[0::2], other_refs[1::2],
                                                     stage_refs[0::2], stage_refs[1::2]):
            for res in range(dil):
                for hd in range(heads):
                    cols = slice(res * bw + hd * HEAD_DIM, res * bw + (hd + 1) * HEAD_DIM)
                    o_st[hd, pl.ds(res, lt // dil, stride=dil), :] = o_ref[:, cols]
                    lse_st[hd, pl.ds(res, lt // dil, stride=dil), :] = lse_ref[:, cols]
    else:
        q_ref, k_ref, v_ref, o_ref, lse_ref = refs
    i = pl.program_id(2)
    blk = DIL_BLOCK
    scale = np.float32(HEAD_DIM ** -0.5)
    ones = jnp.ones((2 * blk, HEAD_DIM), BF16)
    row = lax.broadcasted_iota(jnp.int32, (blk, 2 * blk), 0)
    col = lax.broadcasted_iota(jnp.int32, (blk, 2 * blk), 1)
    for jb in range(lt // blk):
        rws = slice(jb * blk, (jb + 1) * blk)
        l0 = i * lt + jb * blk
        ks = pl.multiple_of(jnp.maximum(l0 - blk, 0), blk)
        dist = (l0 - ks) + row - col
        valid = (dist >= 0) & (dist <= DIL_SPAN)
        for hd in range(heads):
            lanes = slice(hd * HEAD_DIM, (hd + 1) * HEAD_DIM)
            s = _dot_nt(q_ref[rws, lanes], k_ref[pl.ds(ks, 2 * blk), lanes]) * scale
            s = jnp.where(valid, s, NEG_INF)
            mx = jnp.max(s, axis=-1, keepdims=True)
            p = jnp.exp(s - mx).astype(BF16)
            r = _dot(p, jnp.concatenate([v_ref[pl.ds(ks, 2 * blk), lanes], ones], axis=1))
            den = r[:, HEAD_DIM:]
            o = r[:, :HEAD_DIM] / den
            lse = mx + jnp.log(den)
            if others:
                o_all = [o] + [st[hd, rws, :] for st in stage_refs[0::2]]
                lse_all = [lse] + [st[hd, rws, :] for st in stage_refs[1::2]]
                top = functools.reduce(jnp.maximum, lse_all)
                wts = [jnp.exp(t - top) for t in lse_all]
                num = wts[0] * o_all[0]
                tot = wts[0]
                for wg, og in zip(wts[1:], o_all[1:]):
                    num = num + wg * og
                    tot = tot + wg
                y_ref[rws, lanes] = (num / tot).astype(BF16)
            else:
                o_ref[rws, lanes] = o
                lse_ref[rws, lanes] = lse


def _dil_band(q, k, v, others, *, batch, seq, dil, lt, classes=1):
    rows, cols = q.shape
    bw = cols // dil * classes
    heads = bw // HEAD_DIM
    ln = seq // dil
    lt = min(lt, ln)
    nl = ln // lt
    assert ln % lt == 0 and lt % DIL_BLOCK == 0 and ln >= 2 * DIL_BLOCK and dil % classes == 0
    tile = pl.BlockSpec((lt, bw), lambda b, r, i: (b * nl + i, r))
    whole = pl.BlockSpec((ln, bw), lambda b, r, i: (b, r))
    if others:
        assert dil == 1 and all(lt % (od * SUBLANES) == 0 for od, _, _ in others)
        extra = [a for _, o, lse in others for a in (o, lse)]
        extra_specs = [pl.BlockSpec((lt // od, od * bw), lambda b, r, i: (b * nl + i, 0))
                       for od, _, _ in others for _ in range(2)]
        out_specs, out_shape = tile, jax.ShapeDtypeStruct((rows, cols), BF16)
        scratch = [pltpu.VMEM((heads, lt, HEAD_DIM), F32)] * len(extra)
    else:
        extra, extra_specs, scratch = [], [], []
        out_specs = [tile, tile]
        out_shape = [jax.ShapeDtypeStruct((rows, cols), F32)] * 2
    return pl.pallas_call(
        functools.partial(_dil_band_kernel, lt=lt, heads=heads, others=tuple(od for od, _, _ in others)),
        grid=(batch, dil // classes, nl),
        in_specs=[tile, whole, whole] + extra_specs,
        out_specs=out_specs,
        out_shape=out_shape,
        scratch_shapes=scratch,
        compiler_params=_params("parallel", "parallel", "arbitrary"),
        name=f"dil_band_{dil}",
    )(q, k, v, *extra)


def _dil_attn(cq, ck, cv, *, batch, seq):
    dils = [d for _, d in DIL_PATTERNS]
    assert all(w // d == DIL_SPAN for w, d in DIL_PATTERNS) and dils[0] == 1
    others = []
    heads = cq[0].shape[1] // HEAD_DIM
    for idx in range(1, len(dils)):
        dil = dils[idx]
        lt = min(seq // dil, 8 * DIL_BLOCK)
        classes = math.gcd(dil, max(1, 64 // (lt // DIL_BLOCK * heads)))
        o, lse = _dil_band(cq[idx], ck[idx], cv[idx], [], batch=batch, seq=seq, dil=dil, lt=lt, classes=classes)
        others.append((dil, o, lse))
    return _dil_band(cq[0], ck[0], cv[0], others, batch=batch, seq=seq, dil=1, lt=1024)


def _diff_attn_kernel(q_ref, k_ref, v_ref, lq1_ref, lk1_ref, lq2_ref, lk2_ref, sub_ref, o_ref,
                      s_ref, m_ref, l_ref, acc_ref, *, tq, hps, lam_init):
    i = pl.program_id(2)
    tiles = tq // LANES
    head_lanes = [slice(hd * HEAD_DIM, (hd + 1) * HEAD_DIM) for hd in range(hps)]

    def stacked_q(lanes):
        q = q_ref[:, lanes]
        lane = lax.broadcasted_iota(jnp.int32, q.shape, 1)
        zero = jnp.zeros_like(q)
        return jnp.concatenate([jnp.where(lane < DIFF_QK_DIM, q, zero), jnp.where(lane < DIFF_QK_DIM, zero, q)],
                               axis=0)

    qq = [stacked_q(lanes) for lanes in head_lanes]

    def lane_fold(x, op):
        r = x[:, 0:LANES]
        for t in range(1, tiles):
            r = op(r, x[:, t * LANES:(t + 1) * LANES])
        return r

    m_ref[...] = jnp.full(m_ref.shape, NEG_INF, F32)

    def scores(j, masked):
        k0 = pl.multiple_of(j * tq, tq)
        for hd, lanes in enumerate(head_lanes):
            s = _dot_nt(qq[hd], k_ref[pl.ds(k0, tq), lanes]) * np.float32(math.log2(math.e))
            if masked:
                row = lax.broadcasted_iota(jnp.int32, s.shape, 0)
                col = lax.broadcasted_iota(jnp.int32, s.shape, 1)
                row = jnp.where(row >= tq, row - tq, row)
                s = jnp.where(col <= row, s, NEG_INF)
            s_ref[hd, j] = s
            m_ref[hd] = jnp.maximum(m_ref[hd], lane_fold(s, jnp.maximum))

    def accumulate(j):
        k0 = pl.multiple_of(j * tq, tq)
        for hd, lanes in enumerate(head_lanes):
            s = s_ref[hd, j]
            mx = m_ref[hd]
            ps = [jnp.exp2(s[:, t * LANES:(t + 1) * LANES] - mx) for t in range(tiles)]
            part = ps[0]
            for t in range(1, tiles):
                part = part + ps[t]
            l_ref[hd] += part
            p = jnp.concatenate(ps, axis=1).astype(BF16)
            acc_ref[hd] += _dot(p, v_ref[pl.ds(k0, tq), lanes])

    def in_pairs(fn, count):
        def body(jp, carry):
            fn(2 * jp)
            fn(2 * jp + 1)
            return carry

        lax.fori_loop(0, count // 2, body, 0)

        @pl.when(count % 2 == 1)
        def _():
            fn(count - 1)

    in_pairs(lambda j: scores(j, False), i)
    scores(i, True)
    for hd in range(hps):
        m_ref[hd] = jnp.broadcast_to(jnp.max(m_ref[hd], axis=-1, keepdims=True), m_ref.shape[1:])

    l_ref[...] = jnp.zeros(l_ref.shape, F32)
    acc_ref[...] = jnp.zeros(acc_ref.shape, F32)
    in_pairs(accumulate, i + 1)

    lam = (jnp.exp(jnp.sum(lq1_ref[...] * lk1_ref[...], axis=-1, keepdims=True))
           - jnp.exp(jnp.sum(lq2_ref[...] * lk2_ref[...], axis=-1, keepdims=True)) + np.float32(lam_init))
    for hd, lanes in enumerate(head_lanes):
        o = acc_ref[hd] / jnp.sum(l_ref[hd], axis=-1, keepdims=True)
        od = o[0:tq, :] - lam * o[tq:2 * tq, :]
        o_ref[:, lanes] = (_rms(od, sub_ref[...]) * np.float32(1.0 - lam_init)).astype(o_ref.dtype)


def _diff_attn(q, k, v, lq1, lk1, lq2, lk2, subln, *, batch, seq, tq, hps, lam_init):
    m, bw = q.shape
    heads = bw // HEAD_DIM
    nq = seq // tq
    assert heads % hps == 0
    q_spec = pl.BlockSpec((tq, hps * HEAD_DIM), lambda b, h, i: (b * nq + i, h))
    kv_spec = pl.BlockSpec((seq, hps * HEAD_DIM), lambda b, h, i: (b, h))
    lam_spec = pl.BlockSpec((1, DIFF_QK_DIM), lambda b, h, i: (0, 0))
    stat = pltpu.VMEM((hps, 2 * tq, LANES), F32)
    return pl.pallas_call(
        functools.partial(_diff_attn_kernel, tq=tq, hps=hps, lam_init=lam_init),
        grid=(batch, heads // hps, nq),
        in_specs=[q_spec, kv_spec, kv_spec, lam_spec, lam_spec, lam_spec, lam_spec,
                  pl.BlockSpec((1, HEAD_DIM), lambda b, h, i: (0, 0))],
        out_specs=q_spec,
        out_shape=jax.ShapeDtypeStruct((m, bw), BF16),
        scratch_shapes=[pltpu.VMEM((hps, nq, 2 * tq, tq), F32), stat, stat, stat],
        compiler_params=_params("parallel", "parallel", "arbitrary"),
        name="diff_attn",
    )(q, k, v, lq1, lk1, lq2, lk2, subln)


def _merge_kernel(ya_ref, yb_ref, yc_ref, yd_ref, ga_ref, gb_ref, gc_ref, gd_ref, wb_ref, o_ref, *, tn):
    branches = ((ya_ref, ga_ref), (yb_ref, gb_ref), (yc_ref, gc_ref), (yd_ref, gd_ref))
    for c in range(o_ref.shape[1] // tn):
        cols = slice(c * tn, (c + 1) * tn)
        acc = None
        for b, (y_ref, gate_ref) in enumerate(branches):
            term = gate_ref[:, cols].astype(F32) * _dot(y_ref[...], wb_ref[b, :, cols])
            acc = term if acc is None else acc + term
        o_ref[:, cols] = acc.astype(BF16)


def _merge(ys, gates, w_branch_layers, layer, *, tm, tn):
    m, bw = ys[0].shape
    _, nb, _, d = w_branch_layers.shape
    assert nb == len(ys) == 4
    y_spec = pl.BlockSpec((tm, bw), lambda r: (r, 0))
    gate_specs = [pl.BlockSpec((tm, d), lambda r, b=b: (r, b)) for b in range(nb)]
    return pl.pallas_call(
        functools.partial(_merge_kernel, tn=tn),
        grid=(m // tm,),
        in_specs=[y_spec] * nb + gate_specs + [_resident((None, nb, bw, d), lambda r: (layer, 0, 0, 0))],
        out_specs=pl.BlockSpec((tm, d), lambda r: (r, 0)),
        out_shape=jax.ShapeDtypeStruct((m, d), BF16),
        compiler_params=_params("parallel"),
        name="merge_branches",
    )(*ys, gates, gates, gates, gates, w_branch_layers)


def kernel(x, positions, ffn1_norm, ffn1_w13, ffn1_w2, mix_norm, w_in, conv_w, conv_b, conv_ln_g, conv_ln_b, sgu_ln_g, sgu_ln_b, sgu_w, sgu_b, dil_q_norm, dil_k_norm, diff_q_norm, diff_k_norm, diff_lq1, diff_lk1, diff_lq2, diff_lk2, diff_subln, w_branch, w_out, ffn2_norm, ffn2_w13, ffn2_w2):
    batch, seq, d = x.shape
    depth = w_in.shape[0]
    m = batch * seq
    bw = d // 4
    n_mix = 10 * bw
    assert bw % HEAD_DIM == 0 and w_in.shape[2] == n_mix + 4 * d

    tm = min(1024, m)
    dff = ffn1_w2.shape[1]
    tf = 512 if dff % 512 == 0 else 256
    tn = min(512, d)

    def row(v, l):
        return v[l].reshape(1, -1)

    def ffn(x2, h, w13_layers, w2_layers, l, g_next):
        act = _ffn_up(h, w13_layers, l, tm=tm, tf=tf)
        return _residual_matmul(act, w2_layers, l, x2, g_next, tm=min(512, m), name="ffn_down")

    x2 = x.reshape(m, d)
    cos_c, sin_c, cos_d, sin_d = _rope_tables(positions, ts=min(512, m))
    ffn1_w2_bf, ffn2_w2_bf = ffn1_w2.astype(BF16), ffn2_w2.astype(BF16)
    w_mix_bf = _cast_layers(w_in, n_mix, tr=min(512, d))
    w_out_bf = _cast_layers(w_out, d, tr=min(1024, d))
    w_branch_bf = _cast_layers(w_branch.reshape(depth, 4 * bw, d), d, tr=min(1024, d)).reshape(w_branch.shape)

    h = _rmsnorm(x2, row(ffn1_norm, 0), tm=min(512, m))
    for l in range(depth):
        lam_init = 0.8 - 0.6 * math.exp(-0.3 * l)
        x2, h = ffn(x2, h, ffn1_w13, ffn1_w2_bf, l, row(mix_norm, l))

        z, yb, (cq, ck, cv), (dq, dk, dv) = _mix_proj(
            h, w_mix_bf, l, row(sgu_ln_g, l), row(sgu_ln_b, l),
            sgu_w[l], sgu_b[l], row(dil_q_norm, l), row(dil_k_norm, l), cos_c, sin_c,
            row(diff_q_norm, l), row(diff_k_norm, l), cos_d, sin_d, bw=bw, tm=min(512, m))
        gates = _gate_proj(h, w_in, l, n_mix=n_mix, tm=min(2048, m), tn=math.gcd(n_mix, 1024))

        ya = _conv_mixer(z, conv_w[l], row(conv_b, l), row(conv_ln_g, l), row(conv_ln_b, l),
                         batch=batch, seq=seq, ts=min(1024, seq))
        yc = _dil_attn(cq, ck, cv, batch=batch, seq=seq)
        yd = _diff_attn(dq, dk, dv, row(diff_lq1, l), row(diff_lk1, l), row(diff_lq2, l), row(diff_lk2, l),
                        row(diff_subln, l), batch=batch, seq=seq, tq=min(512, seq), hps=2, lam_init=lam_init)

        merged = _merge((ya, yb, yc, yd), gates, w_branch_bf, l, tm=min(512, m), tn=tn)
        x2, h = _residual_matmul(merged, w_out_bf, l, x2, row(ffn2_norm, l), tm=min(512, m), name="out_proj")

        g_next = row(ffn1_norm, l + 1) if l + 1 < depth else row(ffn2_norm, l)
        x2, h = ffn(x2, h, ffn2_w13, ffn2_w2_bf, l, g_next)
    return x2.reshape(batch, seq, d)
```
